```python
import functools
import jax
import jax.numpy as jnp
from jax import lax
import numpy as np

D_MODEL = 1024
BATCH = 8
SEQ = 2048
DEPTH = 4

CTX_LEN = 256
GRID_W = 64
N_MIXERS = 3
EPS = 1e-6

GDN_HEAD_DIM = 128
GDN_QK_HEADS = D_MODEL // GDN_HEAD_DIM
GDN_V_HEADS = 2 * GDN_QK_HEADS
GDN_KEY_DIM = GDN_QK_HEADS * GDN_HEAD_DIM
GDN_VAL_DIM = GDN_V_HEADS * GDN_HEAD_DIM
GDN_CONV_DIM = 2 * GDN_KEY_DIM + GDN_VAL_DIM
GDN_IN_DIM = GDN_CONV_DIM + GDN_VAL_DIM + 4 * GDN_V_HEADS
GDN_CONV_WIDTH = 4
GDN_CHUNK = 64

LRU_WIDTH = D_MODEL
LRU_BLOCKS = 8
LRU_BLOCK = LRU_WIDTH // LRU_BLOCKS
LRU_CONV_WIDTH = 4
LRU_C = 8.0

ATT_HEAD_DIM = 64
ATT_Q_HEADS = D_MODEL // ATT_HEAD_DIM
ATT_KV_HEADS = ATT_Q_HEADS // 4
ATT_GROUP = ATT_Q_HEADS // ATT_KV_HEADS
ATT_Q_DIM = ATT_Q_HEADS * ATT_HEAD_DIM
ATT_KV_DIM = ATT_KV_HEADS * ATT_HEAD_DIM
ATT_IN_DIM = 2 * ATT_Q_DIM + 2 * ATT_KV_DIM
ATT_WINDOW = 128
ATT_BLOCK = 128
ROPE_BASE = 10000.0

N_GDN_LAYERS = (DEPTH + 2) // 3
N_LRU_LAYERS = (DEPTH + 1) // 3
N_ATT_LAYERS = DEPTH // 3

kernel_name = "hybrid_gdn_rglru_swa_prefix_dit"


def rmsnorm(x, g):
    xf = x.astype(jnp.float32)
    y = xf * lax.rsqrt(jnp.mean(xf * xf, axis=-1, keepdims=True) + EPS)
    return (y * g.astype(jnp.float32)).astype(x.dtype)


def l2norm(x):
    xf = x.astype(jnp.float32)
    return (xf * lax.rsqrt(jnp.sum(xf * xf, axis=-1, keepdims=True) + 1e-6)).astype(x.dtype)


def centred_dwconv(x, w):
    k_w, ch = w.shape
    left = k_w // 2
    return lax.conv_general_dilated(
        x, w[:, None, :].astype(x.dtype), window_strides=(1,),
        padding=[(left, k_w - 1 - left)], dimension_numbers=("NWC", "WIO", "NWC"),
        feature_group_count=ch)


def gated_delta_chunked(q, k, v, beta, g, s0):
    bn, nh, length, dk = q.shape
    dv = v.shape[-1]
    cs = GDN_CHUNK
    n = length // cs
    f32 = jnp.float32
    q = q.astype(f32).reshape(bn, nh, n, cs, dk)
    k = k.astype(f32).reshape(bn, nh, n, cs, dk)
    v = v.astype(f32).reshape(bn, nh, n, cs, dv)
    beta = beta.astype(f32).reshape(bn, nh, n, cs)
    gc = jnp.cumsum(g.astype(f32).reshape(bn, nh, n, cs), axis=-1)
    idx = jnp.arange(cs)
    incl = idx[:, None] >= idx[None, :]
    strict = idx[:, None] > idx[None, :]
    decay = jnp.exp(jnp.where(incl, gc[..., :, None] - gc[..., None, :], -jnp.inf))
    kb = k * beta[..., None]
    a_mat = jnp.where(strict, jnp.einsum("bhncd,bhnmd->bhncm", kb, k) * decay, 0.0)
    t_mat = a_mat + jnp.eye(cs, dtype=f32)
    rhs = jnp.concatenate([v * beta[..., None], kb * jnp.exp(gc)[..., None]], axis=-1)
    sol = lax.linalg.triangular_solve(t_mat, rhs, left_side=True, lower=True, unit_diagonal=True)
    u, w = sol[..., :dv], sol[..., dv:]
    qk = jnp.einsum("bhncd,bhnmd->bhncm", q, k) * decay
    q_dec = q * jnp.exp(gc)[..., None]
    k_dec = k * jnp.exp(gc[..., -1:] - gc)[..., None]
    chunk_decay = jnp.exp(gc[..., -1])

    def step(s, inp):
        u_i, w_i, qk_i, q_i, k_i, d_i = inp
        v_new = u_i - jnp.einsum("bhcd,bhde->bhce", w_i, s)
        o_i = jnp.einsum("bhcd,bhde->bhce", q_i, s) + jnp.einsum("bhcm,bhme->bhce", qk_i, v_new)
        s = s * d_i[..., None, None] + jnp.einsum("bhcd,bhce->bhde", k_i, v_new)
        return s, o_i

    xs = tuple(jnp.moveaxis(t, 2, 0) for t in (u, w, qk, q_dec, k_dec, chunk_decay))
    s_final, o = lax.scan(step, s0.astype(f32), xs)
    o = jnp.moveaxis(o, 0, 2).reshape(bn, nh, length, dv)
    return o, s_final


def gdn_mixer(h_ctx, h_lat, with_ctx_out, w_in, conv_w, a_log, dt_bias, g_norm, w_out):
    def streams(h):
        bn, length, _ = h.shape
        p = h @ w_in
        qkv = jax.nn.silu(centred_dwconv(p[..., :GDN_CONV_DIM], conv_w))
        z = p[..., GDN_CONV_DIM:GDN_CONV_DIM + GDN_VAL_DIM]
        ab = p[..., GDN_CONV_DIM + GDN_VAL_DIM:].reshape(bn, length, 2, 2, GDN_V_HEADS)
        rep = GDN_V_HEADS // GDN_QK_HEADS
        q = l2norm(qkv[..., :GDN_KEY_DIM].reshape(bn, length, GDN_QK_HEADS, GDN_HEAD_DIM)) * (GDN_HEAD_DIM ** -0.5)
        k = l2norm(qkv[..., GDN_KEY_DIM:2 * GDN_KEY_DIM].reshape(bn, length, GDN_QK_HEADS, GDN_HEAD_DIM))
        v = qkv[..., 2 * GDN_KEY_DIM:].reshape(bn, length, GDN_V_HEADS, GDN_HEAD_DIM)
        q = jnp.repeat(q, rep, axis=2).transpose(0, 2, 1, 3)
        k = jnp.repeat(k, rep, axis=2).transpose(0, 2, 1, 3)
        v = v.transpose(0, 2, 1, 3)
        a = ab[:, :, 0].astype(jnp.float32).transpose(0, 2, 3, 1)
        b = ab[:, :, 1].astype(jnp.float32).transpose(0, 2, 3, 1)
        beta = jax.nn.sigmoid(b)
        g = -jnp.exp(a_log.astype(jnp.float32))[:, :, None] * jax.nn.softplus(a + dt_bias.astype(jnp.float32)[:, :, None])
        return q, k, v, z, beta, g

    qc, kc, vc, zc, bc, gcx = streams(h_ctx)
    ql, kl, vl, zl, bl, gl = streams(h_lat)
    bn = h_lat.shape[0]
    zero = jnp.zeros((bn, GDN_V_HEADS, GDN_HEAD_DIM, GDN_HEAD_DIM), jnp.float32)
    flip = lambda t: jnp.flip(t, axis=2)
    oc_f, s_f = gated_delta_chunked(qc, kc, vc, bc[:, 0], gcx[:, 0], zero)
    ol_f, _ = gated_delta_chunked(ql, kl, vl, bl[:, 0], gl[:, 0], s_f)
    oc_b, s_b = gated_delta_chunked(flip(qc), flip(kc), flip(vc), flip(bc[:, 1]), flip(gcx[:, 1]), zero)
    ol_b, _ = gated_delta_chunked(flip(ql), flip(kl), flip(vl), flip(bl[:, 1]), flip(gl[:, 1]), s_b)

    def finish(o, z):
        bn_, _, length, _ = o.shape
        o = rmsnorm(o.transpose(0, 2, 1, 3), g_norm).reshape(bn_, length, GDN_VAL_DIM)
        return (o.astype(z.dtype) * jax.nn.silu(z)) @ w_out

    y_lat = finish(ol_f + flip(ol_b), zl)
    y_ctx = finish(oc_f + flip(oc_b), zc) if with_ctx_out else None
    return y_ctx, y_lat


def _lin_combine(left, right):
    a1, b1 = left
    a2, b2 = right
    return a1 * a2, a2 * b1 + b2


def rglru_mixer(h_ctx, h_lat, with_ctx_out, w_in, conv_w, conv_b, w_ra, b_ra, w_ri, b_ri, lam, w_out):
    def streams(h):
        p = h @ w_in
        gate = p[..., :LRU_WIDTH]
        u = centred_dwconv(p[..., LRU_WIDTH:], conv_w) + conv_b
        return gate, u

    def coeffs(u, d):
        bn, length, _ = u.shape
        uf = u.astype(jnp.float32)
        ub = uf.reshape(bn, length, LRU_BLOCKS, LRU_BLOCK)
        r = jax.nn.sigmoid(jnp.einsum("blnc,ncd->blnd", ub, w_ra[d].astype(jnp.float32)).reshape(bn, length, LRU_WIDTH) + b_ra[d])
        i = jax.nn.sigmoid(jnp.einsum("blnc,ncd->blnd", ub, w_ri[d].astype(jnp.float32)).reshape(bn, length, LRU_WIDTH) + b_ri[d])
        log_a = -LRU_C * r * jax.nn.softplus(-lam[d].astype(jnp.float32))
        a = jnp.exp(log_a)
        b = jnp.sqrt(-jnp.expm1(2.0 * log_a)) * (i * uf)
        return a, b

    gate_c, uc = streams(h_ctx)
    gate_l, ul = streams(h_lat)

    def direction(d, rev):
        a, b = coeffs(uc, d)
        _, hc = lax.associative_scan(_lin_combine, (a, b), axis=1, reverse=rev)
        h0 = hc[:, 0] if rev else hc[:, -1]
        a, b = coeffs(ul, d)
        acum, hl = lax.associative_scan(_lin_combine, (a, b), axis=1, reverse=rev)
        return hc, hl + acum * h0[:, None, :]

    hc_f, hl_f = direction(0, False)
    hc_b, hl_b = direction(1, True)
    y_lat = ((hl_f + hl_b).astype(gate_l.dtype) * jax.nn.silu(gate_l)) @ w_out
    y_ctx = (((hc_f + hc_b).astype(gate_c.dtype) * jax.nn.silu(gate_c)) @ w_out) if with_ctx_out else None
    return y_ctx, y_lat


def axial_rope_tables(length):
    rows = length // GRID_W
    row = jnp.repeat(jnp.arange(rows), GRID_W)
    col = jnp.tile(jnp.arange(GRID_W), rows)
    n_freq = ATT_HEAD_DIM // 4
    inv = ROPE_BASE ** (-jnp.arange(n_freq, dtype=jnp.float32) / n_freq)
    ang = jnp.concatenate([row[:, None] * inv, col[:, None] * inv], axis=-1)
    return jnp.cos(ang), jnp.sin(ang)


def apply_rope(x, cos, sin):
    half = x.shape[-1] // 2
    xf = x.astype(jnp.float32)
    x1, x2 = xf[..., :half], xf[..., half:]
    return jnp.concatenate([x1 * cos - x2 * sin, x2 * cos + x1 * sin], axis=-1).astype(x.dtype)


def sink_attention(q, k, v, mask, sink):
    s = jnp.einsum("bqhgd,bkhd->bhgqk", q, k).astype(jnp.float32) * (ATT_HEAD_DIM ** -0.5)
    if mask is not None:
        s = jnp.where(mask, s, -jnp.inf)
    mx = jnp.maximum(jnp.max(s, axis=-1, keepdims=True), sink)
    e = jnp.exp(s - mx)
    p = e / (jnp.sum(e, axis=-1, keepdims=True) + jnp.exp(sink - mx))
    return jnp.einsum("bhgqk,bkhd->bqhgd", p.astype(v.dtype), v)


def window_attn_mixer(h_ctx, h_lat, with_ctx_out, w_in, sinks, w_out):
    def streams(h):
        bn, length, _ = h.shape
        p = h @ w_in
        q = p[..., :ATT_Q_DIM].reshape(bn, length, ATT_KV_HEADS, ATT_GROUP, ATT_HEAD_DIM)
        k = p[..., ATT_Q_DIM:ATT_Q_DIM + ATT_KV_DIM].reshape(bn, length, ATT_KV_HEADS, ATT_HEAD_DIM)
        v = p[..., ATT_Q_DIM + ATT_KV_DIM:ATT_Q_DIM + 2 * ATT_KV_DIM].reshape(bn, length, ATT_KV_HEADS, ATT_HEAD_DIM)
        z = p[..., ATT_Q_DIM + 2 * ATT_KV_DIM:]
        return q, k, v, z

    qc, kc, vc, zc = streams(h_ctx)
    ql, kl, vl, zl = streams(h_lat)
    bn, length, _ = h_lat.shape
    lc = h_ctx.shape[1]
    cos, sin = axial_rope_tables(length)
    ql = apply_rope(ql, cos[:, None, None, :], sin[:, None, None, :])
    kl = apply_rope(kl, cos[:, None, :], sin[:, None, :])
    sink = sinks.astype(jnp.float32).reshape(ATT_KV_HEADS, ATT_GROUP)[None, :, :, None, None]

    nb = length // ATT_BLOCK
    pad = ((0, 0), (ATT_BLOCK, ATT_BLOCK), (0, 0), (0, 0))
    kp = jnp.pad(kl, pad)
    vp = jnp.pad(vl, pad)
    q_blocks = jnp.moveaxis(ql.reshape(bn, nb, ATT_BLOCK, ATT_KV_HEADS, ATT_GROUP, ATT_HEAD_DIM), 1, 0)
    qi_loc = jnp.arange(ATT_BLOCK)
    kj_loc = jnp.arange(3 * ATT_BLOCK) - ATT_BLOCK
    ctx_mask = jnp.ones((ATT_BLOCK, lc), bool)

    def block(args):
        bi, qb = args
        start = bi * ATT_BLOCK
        kb = lax.dynamic_slice_in_dim(kp, start, 3 * ATT_BLOCK, axis=1)
        vb = lax.dynamic_slice_in_dim(vp, start, 3 * ATT_BLOCK, axis=1)
        qi = start + qi_loc
        kj = start + kj_loc
        band = (jnp.abs(qi[:, None] - kj[None, :]) <= ATT_WINDOW) & (kj[None, :] >= 0) & (kj[None, :] < length)
        mask = jnp.concatenate([ctx_mask, band], axis=-1)
        return sink_attention(qb, jnp.concatenate([kc, kb], axis=1), jnp.concatenate([vc, vb], axis=1), mask, sink)

    ol = lax.map(block, (jnp.arange(nb), q_blocks))
    ol = jnp.moveaxis(ol, 0, 1).reshape(bn, length, ATT_Q_DIM)
    y_lat = (ol * jax.nn.silu(zl)) @ w_out
    y_ctx = None
    if with_ctx_out:
        oc = sink_attention(qc, kc, vc, None, sink).reshape(bn, lc, ATT_Q_DIM)
        y_ctx = (oc * jax.nn.silu(zc)) @ w_out
    return y_ctx, y_lat


def setup_inputs(seed: int = 0) -> dict:
    key = jax.random.key(seed)
    ks = iter(jax.random.split(key, 40))
    D = D_MODEL

    def nrm(shape, s):
        return jax.random.normal(next(ks), shape, jnp.float32) * s

    def unif(shape, lo, hi):
        return jax.random.uniform(next(ks), shape, jnp.float32, minval=lo, maxval=hi)

    dt = jnp.exp(unif((N_GDN_LAYERS, 2, GDN_V_HEADS), float(np.log(1e-3)), float(np.log(1e-1))))
    a_pow = unif((N_LRU_LAYERS, 2, LRU_WIDTH), 0.9, 0.999)
    sig_lam = a_pow ** (1.0 / LRU_C)
    return {
        "x": nrm((BATCH, SEQ, D), 1.0),
        "c": nrm((BATCH, D), 1.0),
        "ctx": nrm((BATCH, CTX_LEN, D), 1.0),
        "c_ctx": nrm((D,), 1.0),
        "ada_w": nrm((DEPTH, D, 3 * D), 0.5 * D ** -0.5),
        "ada_b": nrm((DEPTH, 3 * D), 0.02),
        "norm_pre": 1.0 + nrm((DEPTH, D), 0.05),
        "norm_post": 1.0 + nrm((DEPTH, D), 0.05),
        "gdn_w_in": nrm((N_GDN_LAYERS, D, GDN_IN_DIM), D ** -0.5),
        "gdn_conv_w": nrm((N_GDN_LAYERS, GDN_CONV_WIDTH, GDN_CONV_DIM), GDN_CONV_WIDTH ** -0.5),
        "gdn_a_log": jnp.log(unif((N_GDN_LAYERS, 2, GDN_V_HEADS), 1.0, 16.0)),
        "gdn_dt_bias": dt + jnp.log(-jnp.expm1(-dt)),
        "gdn_g_norm": 1.0 + nrm((N_GDN_LAYERS, GDN_HEAD_DIM), 0.05),
        "gdn_w_out": nrm((N_GDN_LAYERS, GDN_VAL_DIM, D), GDN_VAL_DIM ** -0.5),
        "lru_w_in": nrm((N_LRU_LAYERS, D, 2 * LRU_WIDTH), D ** -0.5),
        "lru_conv_w": nrm((N_LRU_LAYERS, LRU_CONV_WIDTH, LRU_WIDTH), LRU_CONV_WIDTH ** -0.5),
        "lru_conv_b": nrm((N_LRU_LAYERS, LRU_WIDTH), 0.02),
        "lru_w_ra": nrm((N_LRU_LAYERS, 2, LRU_BLOCKS, LRU_BLOCK, LRU_BLOCK), LRU_BLOCK ** -0.5),
        "lru_b_ra": nrm((N_LRU_LAYERS, 2, LRU_WIDTH), 0.02),
        "lru_w_ri": nrm((N_LRU_LAYERS, 2, LRU_BLOCKS, LRU_BLOCK, LRU_BLOCK), LRU_BLOCK ** -0.5),
        "lru_b_ri": nrm((N_LRU_LAYERS, 2, LRU_WIDTH), 0.02),
        "lru_lam": jnp.log(sig_lam) - jnp.log1p(-sig_lam),
        "lru_w_out": nrm((N_LRU_LAYERS, LRU_WIDTH, D), LRU_WIDTH ** -0.5),
        "att_w_in": nrm((N_ATT_LAYERS, D, ATT_IN_DIM), D ** -0.5),
        "att_sinks": nrm((N_ATT_LAYERS, ATT_Q_HEADS), 0.5),
        "att_w_out": nrm((N_ATT_LAYERS, ATT_Q_DIM, D), ATT_Q_DIM ** -0.5),
    }


def reference(x, c, ctx, c_ctx, ada_w, ada_b, norm_pre, norm_post,
              gdn_w_in, gdn_conv_w, gdn_a_log, gdn_dt_bias, gdn_g_norm, gdn_w_out,
              lru_w_in, lru_conv_w, lru_conv_b, lru_w_ra, lru_b_ra, lru_w_ri, lru_b_ri, lru_lam, lru_w_out,
              att_w_in, att_sinks, att_w_out):
    x_lat, x_ctx = x, ctx
    for i in range(DEPTH):
        kind, j = i % N_MIXERS, i // N_MIXERS
        with_ctx_out = i < DEPTH - 1
        if kind == 0:
            mixer = functools.partial(gdn_mixer, w_in=gdn_w_in[j], conv_w=gdn_conv_w[j], a_log=gdn_a_log[j],
                                      dt_bias=gdn_dt_bias[j], g_norm=gdn_g_norm[j], w_out=gdn_w_out[j])
        elif kind == 1:
            mixer = functools.partial(rglru_mixer, w_in=lru_w_in[j], conv_w=lru_conv_w[j], conv_b=lru_conv_b[j],
                                      w_ra=lru_w_ra[j], b_ra=lru_b_ra[j], w_ri=lru_w_ri[j], b_ri=lru_b_ri[j],
                                      lam=lru_lam[j], w_out=lru_w_out[j])
        else:
            mixer = functools.partial(window_attn_mixer, w_in=att_w_in[j], sinks=att_sinks[j], w_out=att_w_out[j])
        mod = jax.nn.silu(c) @ ada_w[i] + ada_b[i]
        mod_c = jax.nn.silu(c_ctx) @ ada_w[i] + ada_b[i]
        shift, scale, gate = jnp.split(mod[:, None, :], 3, axis=-1)
        shift_c, scale_c, gate_c = jnp.split(mod_c, 3, axis=-1)
        h_lat = rmsnorm(x_lat, norm_pre[i]) * (1.0 + scale) + shift
        h_ctx = rmsnorm(x_ctx, norm_pre[i]) * (1.0 + scale_c) + shift_c
        y_ctx, y_lat = mixer(h_ctx, h_lat, with_ctx_out)
        x_lat = x_lat + gate * rmsnorm(y_lat, norm_post[i])
        if with_ctx_out:
            x_ctx = x_ctx + gate_c * rmsnorm(y_ctx, norm_post[i])
    return x_lat
```

```python
import functools

import numpy as np
import jax
import jax.numpy as jnp
from jax import lax
from jax.experimental import pallas as pl
from jax.experimental.pallas import tpu as pltpu

F32 = jnp.float32
BF16 = jnp.bfloat16

D_MODEL = 1024
BATCH = 8
SEQ = 2048
CTX_LEN = 256
L_ALL = CTX_LEN + SEQ
DEPTH = 4
GRID_W = 64
EPS = 1e-6

ROW_TILE = 256
N_ROW_TILES = L_ALL // ROW_TILE
MOD_ROWS = 16
CTX_MOD_ROW = BATCH
VMEM_LIMIT = 56 * 1024 * 1024

GDN_HD = 128
GDN_QK_HEADS = 8
GDN_V_HEADS = 16
GDN_KEY_DIM = 1024
GDN_VAL_DIM = 2048
GDN_MAIN = 2 * GDN_KEY_DIM + 2 * GDN_VAL_DIM
GDN_AB = 64
CHUNK = 64
N_CHUNKS = L_ALL // CHUNK
CTX_CHUNKS = CTX_LEN // CHUNK
PAIR = 2 * CHUNK

LRU_WIDTH = 1024
LRU_BLOCK = 128
LRU_BLOCKS = 8
LRU_C = 8.0
LRU_ROWS = 256
SUB = 8
N_GROUPS = L_ALL // SUB
CTX_GROUPS = CTX_LEN // SUB

ATT_HD = 64
ATT_KV_HEADS = 4
ATT_GROUP = 4
ATT_BLOCK = 128
ATT_WINDOW = 128
ATT_QW = ATT_GROUP * ATT_HD
ATT_COLS = 4 * D_MODEL
NEG = -1e30


def _silu(x):
    return x * jax.nn.sigmoid(x)


def _softplus(x):
    return jnp.maximum(x, 0.0) + jnp.log1p(jnp.exp(-jnp.abs(x)))


def _cparams(sem):
    return pltpu.CompilerParams(dimension_semantics=sem, vmem_limit_bytes=VMEM_LIMIT)


def _dot(a, b):
    return jnp.dot(a, b, preferred_element_type=F32)


def _dot_nt(a, b):
    return lax.dot_general(a, b, (((1,), (1,)), ((), ())), preferred_element_type=F32)


def _dot_tn(a, b):
    return lax.dot_general(a, b, (((0,), (0,)), ((), ())), preferred_element_type=F32)


def _mod_body(c_ref, w_ref, b_ref, o_ref):
    h = _silu(c_ref[...]).astype(BF16)
    o_ref[0] = _dot(h, w_ref[0].astype(BF16)) + b_ref[0]


def _modulation(cc, ada_w, ada_b):
    n_col = 3
    return pl.pallas_call(
        _mod_body,
        grid=(DEPTH, n_col),
        in_specs=[
            pl.BlockSpec((MOD_ROWS, D_MODEL), lambda i, n: (0, 0)),
            pl.BlockSpec((1, D_MODEL, D_MODEL), lambda i, n: (i, 0, n)),
            pl.BlockSpec((1, 1, D_MODEL), lambda i, n: (i, 0, n)),
        ],
        out_specs=pl.BlockSpec((1, MOD_ROWS, D_MODEL), lambda i, n: (i, 0, n)),
        out_shape=jax.ShapeDtypeStruct((DEPTH, MOD_ROWS, 3 * D_MODEL), F32),
        compiler_params=_cparams(("parallel", "parallel")),
    )(cc, ada_w, ada_b.reshape(DEPTH, 1, 3 * D_MODEL))


def _mod_index(b, t):
    return (jnp.where(t == 0, CTX_MOD_ROW, b), 0, 0)


def _in_proj_body(x_ref, mod_ref, nw_ref, w_ref, *refs, segs, rope_groups):
    if rope_groups:
        cos_ref, sin_ref = refs[:2]
        refs = refs[2:]
    x = x_ref[0]
    y = x * lax.rsqrt(jnp.mean(x * x, axis=-1, keepdims=True) + EPS) * nw_ref[...]
    shift = mod_ref[0, :, 0:D_MODEL]
    scale = mod_ref[0, :, D_MODEL:2 * D_MODEL]
    h = (y * (1.0 + scale) + shift).astype(BF16)
    for ref, (c0, width) in zip(refs, segs):
        step = min(width, 512)
        for n0 in range(0, width, step):
            acc = _dot(h, w_ref[:, c0 + n0:c0 + n0 + step])
            if (c0 + n0) // 512 in rope_groups:
                cos = cos_ref[...]
                sin = sin_ref[...]
                parts = []
                for g0 in range(0, step, 256):
                    x1 = acc[:, g0:g0 + 128]
                    x2 = acc[:, g0 + 128:g0 + 256]
                    parts += [x1 * cos - x2 * sin, x2 * cos + x1 * sin]
                acc = jnp.concatenate(parts, axis=1)
            ref[0, :, n0:n0 + step] = acc.astype(ref.dtype)


def _in_proj(x, mod, nw, w, segs, out_dtypes, rope=None):
    n_total = w.shape[1]
    in_specs = [
        pl.BlockSpec((1, ROW_TILE, D_MODEL), lambda b, t: (b, t, 0)),
        pl.BlockSpec((1, 1, 3 * D_MODEL), _mod_index),
        pl.BlockSpec((1, D_MODEL), lambda b, t: (0, 0)),
        pl.BlockSpec((D_MODEL, n_total), lambda b, t: (0, 0), pipeline_mode=pl.Buffered(1)),
    ]
    args = [x, mod, nw.reshape(1, D_MODEL), w]
    rope_groups = ()
    if rope is not None:
        cos, sin, rope_groups = rope
        in_specs += [pl.BlockSpec((ROW_TILE, 128), lambda b, t: (t, 0))] * 2
        args += [cos, sin]
    out_specs = [pl.BlockSpec((1, ROW_TILE, width), lambda b, t: (b, t, 0)) for _, width in segs]
    bsz = x.shape[0]
    out_shape = [jax.ShapeDtypeStruct((bsz, L_ALL, width), dt) for (_, width), dt in zip(segs, out_dtypes)]
    return pl.pallas_call(
        functools.partial(_in_proj_body, segs=tuple(segs), rope_groups=tuple(rope_groups)),
        grid=(bsz, N_ROW_TILES),
        in_specs=in_specs,
        out_specs=out_specs,
        out_shape=out_shape,
        compiler_params=_cparams(("parallel", "parallel")),
    )(*args)


def _out_proj_body(o_ref, z_ref, x_ref, mod_ref, nw_ref, w_ref, out_ref):
    g = (o_ref[0].astype(F32) * _silu(z_ref[0].astype(F32))).astype(BF16)
    y = _dot(g, w_ref[...])
    yn = y * lax.rsqrt(jnp.mean(y * y, axis=-1, keepdims=True) + EPS) * nw_ref[...]
    gate = mod_ref[0, :, 2 * D_MODEL:3 * D_MODEL]
    out_ref[0] = x_ref[0] + gate * yn


def _out_proj(o, z_arr, z_block, x, mod, nw, w):
    width = o.shape[-1]
    bsz = o.shape[0]
    return pl.pallas_call(
        _out_proj_body,
        grid=(bsz, N_ROW_TILES),
        in_specs=[
            pl.BlockSpec((1, ROW_TILE, width), lambda b, t: (b, t, 0)),
            pl.BlockSpec((1, ROW_TILE, width), lambda b, t: (b, t, z_block)),
            pl.BlockSpec((1, ROW_TILE, D_MODEL), lambda b, t: (b, t, 0)),
            pl.BlockSpec((1, 1, 3 * D_MODEL), _mod_index),
            pl.BlockSpec((1, D_MODEL), lambda b, t: (0, 0)),
            pl.BlockSpec((width, D_MODEL), lambda b, t: (0, 0), pipeline_mode=pl.Buffered(1)),
        ],
        out_specs=pl.BlockSpec((1, ROW_TILE, D_MODEL), lambda b, t: (b, t, 0)),
        out_shape=jax.ShapeDtypeStruct((bsz, L_ALL, D_MODEL), F32),
        compiler_params=_cparams(("parallel", "parallel")),
    )(o, z_arr, x, mod, nw.reshape(1, D_MODEL), w)


def _conv_rows(ref, r0, rows, width_sl, cw, left_ok, right_ok):
    total = ref.shape[1]
    main = ref[0, pl.ds(r0, rows), width_sl]
    prev = ref[0, pl.ds(pl.multiple_of(jnp.maximum(r0 - SUB, 0), SUB), SUB), width_sl] * left_ok
    nxt = ref[0, pl.ds(pl.multiple_of(jnp.minimum(r0 + rows, total - SUB), SUB), SUB), width_sl] * right_ok
    xw = jnp.concatenate([prev, main, nxt], axis=0)
    return (cw[0:1] * xw[SUB - 2:SUB - 2 + rows] + cw[1:2] * xw[SUB - 1:SUB - 1 + rows]
            + cw[2:3] * xw[SUB:SUB + rows] + cw[3:4] * xw[SUB + 1:SUB + 1 + rows])


def _gdn_consts():
    i = np.arange(PAIR)
    same = (i[:, None] // CHUNK) == (i[None, :] // CHUNK)
    t_i, t_m = i[:, None] % CHUNK, i[None, :] % CHUNK
    low = same & (t_m <= t_i)
    upp = same & (t_m >= t_i)
    tri_col = np.concatenate([low, upp, same], axis=0).astype(np.float32)
    expand = (i[:, None] // CHUNK) == (np.arange(2 * GDN_HD)[None, :] // GDN_HD)
    tri_row = np.concatenate([low.T, upp.T, expand], axis=1).astype(np.float32)
    mats = np.stack([
        np.where(low, 0.0, NEG), (same & (t_m < t_i)).astype(np.float32),
        np.where(upp, 0.0, NEG), (same & (t_m > t_i)).astype(np.float32),
        np.eye(PAIR, dtype=np.float32)]).astype(np.float32)
    return jnp.asarray(tri_col, BF16), jnp.asarray(tri_row, BF16), jnp.asarray(mats, F32)


def _split_bf16(x):
    hi = x.astype(BF16)
    lo = (x - hi.astype(F32)).astype(BF16)
    return hi, lo


def _gdn_body(q_ref, k_ref, v_ref, cwq_ref, cwk_ref, cwv_ref, abc_ref, abr_ref, pc_ref, pr_ref,
              tric_ref, trir_ref, mats_ref, gn_ref, o_ref,
              u_s, wq_s, kd_s, qk_s, dec_s, st_s):
    full = slice(None)

    def phase_a(c, carry):
        r0 = pl.multiple_of(c * CHUNK, CHUNK)
        left_ok = jnp.where((c == 0) | (c == CTX_CHUNKS), 0.0, 1.0)
        right_ok = jnp.where((c == CTX_CHUNKS - 1) | (c == N_CHUNKS - 1), 0.0, 1.0)
        qc = _silu(_conv_rows(q_ref, r0, CHUNK, full, cwq_ref[...], left_ok, right_ok))
        kc = _silu(_conv_rows(k_ref, r0, CHUNK, full, cwk_ref[...], left_ok, right_ok))
        vc = _silu(_conv_rows(v_ref, r0, CHUNK, full, cwv_ref[...], left_ok, right_ok))
        qn = qc * (lax.rsqrt(jnp.sum(qc * qc, axis=-1, keepdims=True) + 1e-6) * (GDN_HD ** -0.5))
        kn = kc * lax.rsqrt(jnp.sum(kc * kc, axis=-1, keepdims=True) + 1e-6)
        q2 = jnp.concatenate([qn, qn], axis=0)
        k2 = jnp.concatenate([kn, kn], axis=0)
        v2 = jnp.concatenate([vc[:, :GDN_HD], vc[:, GDN_HD:]], axis=0)
        k2b = k2.astype(BF16)
        kk = _dot_nt(k2b, k2b)
        qk = _dot_nt(q2.astype(BF16), k2b)

        xc = abc_ref[0, 0, pl.ds(pl.multiple_of(c * PAIR, PAIR), PAIR), :]
        g_c = -jnp.exp(pc_ref[0, :, 0:4]) * _softplus(xc + pc_ref[0, :, 4:8])
        beta_c = jax.nn.sigmoid(xc)
        ghi, glo = _split_bf16(g_c)
        cs = _dot(tric_ref[...], jnp.concatenate([ghi, glo], axis=1))
        cs = cs[:, 0:4] + cs[:, 4:8]
        xr = abr_ref[0, 0, c]
        g_r = -jnp.exp(pr_ref[0, 0:4, :]) * _softplus(xr + pr_ref[0, 4:8, :])
        ghi, glo = _split_bf16(g_r)
        csr = _dot(jnp.concatenate([ghi, glo], axis=0), trir_ref[...])
        csr = csr[0:4] + csr[4:8]

        for d in range(2):
            la, lb = 2 * d, 2 * d + 1
            gcc = cs[d * PAIR:(d + 1) * PAIR, la:la + 1]
            totc = cs[2 * PAIR:3 * PAIR, la:la + 1]
            gcr = csr[la:la + 1, d * PAIR:(d + 1) * PAIR]
            dec = jnp.exp(csr[la:la + 1, 2 * PAIR:2 * PAIR + 2 * GDN_HD])
            bc = beta_c[:, lb:lb + 1]
            m = jnp.exp((gcc - gcr) + mats_ref[2 * d])
            a = kk * m * mats_ref[2 * d + 1] * bc
            x = mats_ref[4] - a
            ab16 = a.astype(BF16)
            p = _dot(ab16, ab16)
            for it in range(4):
                xp = _dot(jnp.concatenate([x, p], axis=0).astype(BF16), p.astype(BF16))
                x = x + xp[:PAIR]
                p = xp[PAIR:]
            x = x + _dot(x.astype(BF16), p.astype(BF16))
            egc = jnp.exp(gcc)
            rhs = jnp.concatenate([v2 * bc, k2 * (bc * egc)], axis=1).astype(BF16)
            uw = _dot(x.astype(BF16), rhs)
            idx = d * N_CHUNKS + c
            u_s[idx] = uw[:, :GDN_HD]
            wq_s[idx] = jnp.concatenate([uw[:, GDN_HD:], q2 * egc], axis=0).astype(BF16)
            kd_s[idx] = (k2 * jnp.exp(totc - gcc)).astype(BF16)
            qk_s[idx] = (qk * m).astype(BF16)
            dec_s[idx] = jnp.broadcast_to(dec, (SUB, 2 * GDN_HD))
        return carry

    lax.fori_loop(0, N_CHUNKS, phase_a, 0)

    st_s[...] = jnp.zeros_like(st_s)
    o_ref[...] = jnp.zeros_like(o_ref)
    row_is_h0 = lax.broadcasted_iota(jnp.int32, (PAIR, GDN_HD), 0) < CHUNK

    def phase_b(s, carry):
        chunk_b = jnp.where(s < CTX_CHUNKS, CTX_CHUNKS - 1 - s, N_CHUNKS + CTX_CHUNKS - 1 - s)
        for d, c in ((0, s), (1, chunk_b)):
            idx = d * N_CHUNKS + c
            st = st_s[d]
            r = _dot(wq_s[idx], st.astype(BF16))
            ws = jnp.concatenate([r[0:CHUNK, :GDN_HD], r[CHUNK:PAIR, GDN_HD:]], axis=0)
            qs = jnp.concatenate([r[PAIR:PAIR + CHUNK, :GDN_HD], r[PAIR + CHUNK:, GDN_HD:]], axis=0)
            vnew = u_s[idx] - ws
            o = qs + _dot(qk_s[idx], vnew.astype(BF16))
            r0 = pl.multiple_of(c * CHUNK, CHUNK)
            o_ref[0, pl.ds(r0, CHUNK), 0:GDN_HD] += o[:CHUNK]
            o_ref[0, pl.ds(r0, CHUNK), GDN_HD:] += o[CHUNK:]
            vbd = jnp.concatenate([jnp.where(row_is_h0, vnew, 0.0), jnp.where(row_is_h0, 0.0, vnew)],
                                  axis=1).astype(BF16)
            st_s[d] = st * dec_s[idx][0:1] + _dot_tn(kd_s[idx], vbd)
        return carry

    lax.fori_loop(0, N_CHUNKS, phase_b, 0)

    def finish(c, carry):
        r0 = pl.multiple_of(c * CHUNK, CHUNK)
        for h in range(2):
            sl = slice(h * GDN_HD, (h + 1) * GDN_HD)
            x = o_ref[0, pl.ds(r0, CHUNK), sl]
            o_ref[0, pl.ds(r0, CHUNK), sl] = (
                x * lax.rsqrt(jnp.mean(x * x, axis=-1, keepdims=True) + EPS) * gn_ref[...])
        return carry

    lax.fori_loop(0, N_CHUNKS, finish, 0)


def _gdn_mixer(p_main, p_ab, conv_w, a_log, dt_bias, g_norm):
    tri_col, tri_row, mats = _gdn_consts()
    bsz = p_main.shape[0]
    ab = p_ab.reshape(bsz, N_CHUNKS, CHUNK, 2, 2, GDN_QK_HEADS, 2)
    abc = ab.transpose(0, 5, 1, 6, 2, 4, 3).reshape(bsz, GDN_QK_HEADS, N_CHUNKS * PAIR, 4)
    abr = ab.transpose(0, 5, 1, 4, 3, 6, 2).reshape(bsz, GDN_QK_HEADS, N_CHUNKS, 4, PAIR)
    prm = jnp.stack([a_log, dt_bias], axis=0).astype(F32).reshape(2, 2, GDN_QK_HEADS, 2)
    lane = jnp.stack([prm[0], prm[0], prm[1], prm[1]], axis=0)
    col = jnp.stack([lane[0, 0], lane[0, 0], lane[0, 1], lane[0, 1],
                     lane[2, 0], lane[2, 0], lane[2, 1], lane[2, 1]], axis=0)
    pc = jnp.repeat(col.transpose(1, 2, 0), CHUNK, axis=1)
    pr = pc.transpose(0, 2, 1)

    grid = (bsz, GDN_QK_HEADS)
    kq = GDN_KEY_DIM // GDN_HD
    in_specs = [
        pl.BlockSpec((1, L_ALL, GDN_HD), lambda b, j: (b, 0, j)),
        pl.BlockSpec((1, L_ALL, GDN_HD), lambda b, j: (b, 0, kq + j)),
        pl.BlockSpec((1, L_ALL, 2 * GDN_HD), lambda b, j: (b, 0, kq + j)),
        pl.BlockSpec((4, GDN_HD), lambda b, j: (0, j)),
        pl.BlockSpec((4, GDN_HD), lambda b, j: (0, kq + j)),
        pl.BlockSpec((4, 2 * GDN_HD), lambda b, j: (0, kq + j)),
        pl.BlockSpec((1, 1, N_CHUNKS * PAIR, 4), lambda b, j: (b, j, 0, 0)),
        pl.BlockSpec((1, 1, N_CHUNKS, 4, PAIR), lambda b, j: (b, j, 0, 0, 0)),
        pl.BlockSpec((1, PAIR, 8), lambda b, j: (j, 0, 0)),
        pl.BlockSpec((1, 8, PAIR), lambda b, j: (j, 0, 0)),
        pl.BlockSpec((3 * PAIR, PAIR), lambda b, j: (0, 0)),
        pl.BlockSpec((PAIR, 4 * PAIR), lambda b, j: (0, 0)),
        pl.BlockSpec((5, PAIR, PAIR), lambda b, j: (0, 0, 0)),
        pl.BlockSpec((1, GDN_HD), lambda b, j: (0, 0)),
    ]
    return pl.pallas_call(
        _gdn_body,
        grid=grid,
        in_specs=in_specs,
        out_specs=pl.BlockSpec((1, L_ALL, 2 * GDN_HD), lambda b, j: (b, 0, j)),
        out_shape=jax.ShapeDtypeStruct((bsz, L_ALL, GDN_VAL_DIM), F32),
        scratch_shapes=[
            pltpu.VMEM((2 * N_CHUNKS, PAIR, GDN_HD), F32),
            pltpu.VMEM((2 * N_CHUNKS, 2 * PAIR, GDN_HD), BF16),
            pltpu.VMEM((2 * N_CHUNKS, PAIR, GDN_HD), BF16),
            pltpu.VMEM((2 * N_CHUNKS, PAIR, PAIR), BF16),
            pltpu.VMEM((2 * N_CHUNKS, SUB, 2 * GDN_HD), F32),
            pltpu.VMEM((2, GDN_HD, 2 * GDN_HD), F32),
        ],
        compiler_params=_cparams(("parallel", "parallel")),
    )(p_main, p_main, p_main, conv_w, conv_w, conv_w, abc, abr, pc, pr, tri_col, tri_row, mats,
      g_norm.reshape(1, GDN_HD))


def _lru_body(u_ref, cw_ref, cb_ref, wra_ref, wri_ref, bra_ref, bri_ref, lam_ref, o_ref, a_s, b_s):
    full = slice(None)
    n_steps = L_ALL // LRU_ROWS

    def gates(c, carry):
        r0 = pl.multiple_of(c * LRU_ROWS, LRU_ROWS)
        left_ok = jnp.where(c >= 2, 1.0, 0.0)
        right_ok = jnp.where((c == 0) | (c == n_steps - 1), 0.0, 1.0)
        u = _conv_rows(u_ref, r0, LRU_ROWS, full, cw_ref[...], left_ok, right_ok) + cb_ref[...]
        ub = u.astype(BF16)
        for d in range(2):
            r = jax.nn.sigmoid(_dot(ub, wra_ref[d, 0]) + bra_ref[d:d + 1])
            i = jax.nn.sigmoid(_dot(ub, wri_ref[d, 0]) + bri_ref[d:d + 1])
            log_a = -LRU_C * r * _softplus(-lam_ref[d:d + 1])
            a = jnp.exp(log_a)
            one_m_a2 = -jnp.tanh(log_a) * (a * a + 1.0)
            a_s[d, pl.ds(r0, LRU_ROWS), :] = a
            b_s[d, pl.ds(r0, LRU_ROWS), :] = jnp.sqrt(one_m_a2) * (i * u)
        return carry

    lax.fori_loop(0, n_steps, gates, 0)

    o_ref[...] = jnp.zeros_like(o_ref)
    row = lax.broadcasted_iota(jnp.int32, (SUB, LRU_BLOCK), 0)

    def scan(g, carry):
        hf, hb = carry
        gb = jnp.where(g < CTX_GROUPS, CTX_GROUPS - 1 - g, N_GROUPS + CTX_GROUPS - 1 - g)
        r0 = pl.multiple_of(g * SUB, SUB)
        a, b = a_s[0, pl.ds(r0, SUB), :], b_s[0, pl.ds(r0, SUB), :]
        for s in (1, 2, 4):
            ok = row >= s
            b = jnp.where(ok, a * pltpu.roll(b, s, 0) + b, b)
            a = jnp.where(ok, a * pltpu.roll(a, s, 0), a)
        h = b + a * hf
        o_ref[0, pl.ds(r0, SUB), :] += h
        hf = h[SUB - 1:SUB]
        r0 = pl.multiple_of(gb * SUB, SUB)
        a, b = a_s[1, pl.ds(r0, SUB), :], b_s[1, pl.ds(r0, SUB), :]
        for s in (1, 2, 4):
            ok = row < SUB - s
            b = jnp.where(ok, a * pltpu.roll(b, SUB - s, 0) + b, b)
            a = jnp.where(ok, a * pltpu.roll(a, SUB - s, 0), a)
        h = b + a * hb
        o_ref[0, pl.ds(r0, SUB), :] += h
        hb = h[0:1]
        return hf, hb

    zero = jnp.zeros((1, LRU_BLOCK), F32)
    lax.fori_loop(0, N_GROUPS, scan, (zero, zero))


def _lru_mixer(p, conv_w, conv_b, w_ra, w_ri, b_ra, b_ri, lam):
    nb = LRU_WIDTH // LRU_BLOCK
    bsz = p.shape[0]
    vec = lambda rows: pl.BlockSpec((rows, LRU_BLOCK), lambda b, n: (0, n))
    wspec = pl.BlockSpec((2, 1, LRU_BLOCK, LRU_BLOCK), lambda b, n: (0, n, 0, 0))
    return pl.pallas_call(
        _lru_body,
        grid=(bsz, LRU_BLOCKS),
        in_specs=[
            pl.BlockSpec((1, L_ALL, LRU_BLOCK), lambda b, n: (b, 0, nb + n)),
            vec(4), vec(1), wspec, wspec, vec(2), vec(2), vec(2),
        ],
        out_specs=pl.BlockSpec((1, L_ALL, LRU_BLOCK), lambda b, n: (b, 0, n)),
        out_shape=jax.ShapeDtypeStruct((bsz, L_ALL, LRU_WIDTH), F32),
        scratch_shapes=[pltpu.VMEM((2, L_ALL, LRU_BLOCK), F32), pltpu.VMEM((2, L_ALL, LRU_BLOCK), F32)],
        compiler_params=_cparams(("parallel", "parallel")),
    )(p, conv_w, conv_b.reshape(1, LRU_WIDTH), w_ra.astype(BF16), w_ri.astype(BF16), b_ra, b_ri, lam)


def _att_body(sink_ref, q_ref, kc_ref, vc_ref, k0_ref, k1_ref, k2_ref, v0_ref, v1_ref, v2_ref, o_ref):
    i = pl.program_id(1)
    n_ctx_blocks = CTX_LEN // ATT_BLOCK
    li = i - n_ctx_blocks
    qpos = li * ATT_BLOCK + lax.broadcasted_iota(jnp.int32, (ATT_BLOCK, 3 * ATT_BLOCK), 0)
    kpos = (li - 1) * ATT_BLOCK + lax.broadcasted_iota(jnp.int32, (ATT_BLOCK, 3 * ATT_BLOCK), 1)
    band = (jnp.abs(qpos - kpos) <= ATT_WINDOW) & (kpos >= 0) & (kpos < SEQ) & (li >= 0)
    bias = jnp.concatenate([jnp.zeros((ATT_BLOCK, CTX_LEN), F32), jnp.where(band, 0.0, NEG)], axis=1)
    bias4 = jnp.concatenate([bias] * ATT_GROUP, axis=0)
    n_keys = CTX_LEN + 3 * ATT_BLOCK
    lane = lax.broadcasted_iota(jnp.int32, (ATT_BLOCK, ATT_QW), 1)
    qmask = [(lane % 128) // 32 == g for g in range(ATT_GROUP)]
    omask = [lane // ATT_HD == g for g in range(ATT_GROUP)]
    rowg = lax.broadcasted_iota(jnp.int32, (ATT_GROUP * ATT_BLOCK, 1), 0) // ATT_BLOCK
    for h in range(ATT_KV_HEADS):
        sl = slice(h * ATT_QW, (h + 1) * ATT_QW)
        qh = q_ref[0, :, sl]
        qm = jnp.concatenate([jnp.where(qmask[g], qh, 0.0) for g in range(ATT_GROUP)], axis=0).astype(BF16)
        keys = jnp.concatenate([kc_ref[0, :, sl], k0_ref[0, :, sl], k1_ref[0, :, sl], k2_ref[0, :, sl]],
                               axis=0).astype(BF16)
        vals = jnp.concatenate([vc_ref[0, :, sl], v0_ref[0, :, sl], v1_ref[0, :, sl], v2_ref[0, :, sl]],
                               axis=0).astype(BF16)
        s = _dot_nt(qm, keys) + bias4
        sink = jnp.zeros((ATT_GROUP * ATT_BLOCK, 1), F32)
        for g in range(ATT_GROUP):
            sink = jnp.where(rowg == g, sink_ref[h * ATT_GROUP + g], sink)
        mx = jnp.maximum(jnp.max(s, axis=-1, keepdims=True), sink)
        e = jnp.exp(s - mx)
        den = jnp.sum(e, axis=-1, keepdims=True) + jnp.exp(sink - mx)
        r = _dot((e / den).astype(BF16), vals)
        out = jnp.zeros((ATT_BLOCK, ATT_QW), F32)
        for g in range(ATT_GROUP):
            out = jnp.where(omask[g], r[g * ATT_BLOCK:(g + 1) * ATT_BLOCK], out)
        o_ref[0, :, sl] = out
    del n_keys


def _att_mixer(p, sinks):
    n_ctx_blocks = CTX_LEN // ATT_BLOCK
    n_lat_blocks = SEQ // ATT_BLOCK
    bsz = p.shape[0]

    def win(off, col):
        def index(b, i, sink_ref):
            blk = jnp.clip(i - n_ctx_blocks + off, 0, n_lat_blocks - 1)
            return (b, n_ctx_blocks + blk, col)
        return pl.BlockSpec((1, ATT_BLOCK, D_MODEL), index)

    grid_spec = pltpu.PrefetchScalarGridSpec(
        num_scalar_prefetch=1,
        grid=(bsz, L_ALL // ATT_BLOCK),
        in_specs=[
            pl.BlockSpec((1, ATT_BLOCK, D_MODEL), lambda b, i, s: (b, i, 0)),
            pl.BlockSpec((1, CTX_LEN, D_MODEL), lambda b, i, s: (b, 0, 2)),
            pl.BlockSpec((1, CTX_LEN, D_MODEL), lambda b, i, s: (b, 0, 3)),
            win(-1, 2), win(0, 2), win(1, 2), win(-1, 3), win(0, 3), win(1, 3),
        ],
        out_specs=pl.BlockSpec((1, ATT_BLOCK, D_MODEL), lambda b, i, s: (b, i, 0)),
    )
    return pl.pallas_call(
        _att_body,
        grid_spec=grid_spec,
        out_shape=jax.ShapeDtypeStruct((bsz, L_ALL, D_MODEL), F32),
        compiler_params=_cparams(("parallel", "arbitrary")),
    )(sinks.astype(F32), p, p, p, p, p, p, p, p, p)


def _att_weight_layout(w_in):
    half = ATT_HD // 2
    q_cols, k_cols, v_cols = [], [], []
    for h in range(ATT_KV_HEADS):
        for part in range(2):
            for g in range(ATT_GROUP):
                base = (h * ATT_GROUP + g) * ATT_HD + part * half
                q_cols += list(range(base, base + half))
                kb = D_MODEL + h * ATT_HD + part * half
                k_cols += list(range(kb, kb + half))
        vb = D_MODEL + ATT_KV_HEADS * ATT_HD + h * ATT_HD
        v_cols += list(range(vb, vb + ATT_HD)) * ATT_GROUP
    z0 = D_MODEL + 2 * ATT_KV_HEADS * ATT_HD
    z_cols = list(range(z0, z0 + D_MODEL))
    order = np.asarray(q_cols + z_cols + k_cols + v_cols, np.int32)
    scale = np.ones((ATT_COLS,), np.float32)
    scale[:D_MODEL] = ATT_HD ** -0.5
    return (w_in[:, order] * scale).astype(BF16)


def _rope_tables():
    rows = SEQ // GRID_W
    row = jnp.repeat(jnp.arange(rows), GRID_W)
    col = jnp.tile(jnp.arange(GRID_W), rows)
    n_freq = ATT_HD // 4
    inv = 10000.0 ** (-jnp.arange(n_freq, dtype=F32) / n_freq)
    ang = jnp.concatenate([row[:, None] * inv, col[:, None] * inv], axis=-1)
    cos = jnp.concatenate([jnp.ones((CTX_LEN, ATT_HD // 2), F32), jnp.cos(ang)], axis=0)
    sin = jnp.concatenate([jnp.zeros((CTX_LEN, ATT_HD // 2), F32), jnp.sin(ang)], axis=0)
    return jnp.tile(cos, (1, ATT_GROUP)), jnp.tile(sin, (1, ATT_GROUP))


def kernel(x, c, ctx, c_ctx, ada_w, ada_b, norm_pre, norm_post, gdn_w_in, gdn_conv_w, gdn_a_log, gdn_dt_bias, gdn_g_norm, gdn_w_out, lru_w_in, lru_conv_w, lru_conv_b, lru_w_ra, lru_b_ra, lru_w_ri, lru_b_ri, lru_lam, lru_w_out, att_w_in, att_sinks, att_w_out):
    cc = jnp.zeros((MOD_ROWS, D_MODEL), F32).at[:c.shape[0]].set(c).at[CTX_MOD_ROW].set(c_ctx)
    mod_all = _modulation(cc, ada_w, ada_b).reshape(DEPTH, MOD_ROWS, 1, 3 * D_MODEL)
    xs = jnp.concatenate([ctx, x], axis=1)
    cos, sin = _rope_tables()
    for i in range(DEPTH):
        kind, j = i % 3, i // 3
        mod = mod_all[i]
        if kind == 0:
            p_main, p_ab = _in_proj(xs, mod, norm_pre[i], gdn_w_in[j].astype(BF16),
                                    [(0, GDN_MAIN), (GDN_MAIN, GDN_AB)], [F32, F32])
            o = _gdn_mixer(p_main, p_ab, gdn_conv_w[j], gdn_a_log[j], gdn_dt_bias[j], gdn_g_norm[j])
            xs = _out_proj(o, p_main, 2, xs, mod, norm_post[i], gdn_w_out[j].astype(BF16))
        elif kind == 1:
            (p,) = _in_proj(xs, mod, norm_pre[i], lru_w_in[j].astype(BF16), [(0, 2 * LRU_WIDTH)], [F32])
            o = _lru_mixer(p, lru_conv_w[j], lru_conv_b[j], lru_w_ra[j], lru_w_ri[j],
                           lru_b_ra[j], lru_b_ri[j], lru_lam[j])
            xs = _out_proj(o, p, 0, xs, mod, norm_post[i], lru_w_out[j].astype(BF16))
        else:
            (p,) = _in_proj(xs, mod, norm_pre[i], _att_weight_layout(att_w_in[j]), [(0, ATT_COLS)], [F32],
                            rope=(cos, sin, (0, 1, 4, 5)))
            o = _att_mixer(p, att_sinks[j])
            xs = _out_proj(o, p, 1, xs, mod, norm_post[i], att_w_out[j].astype(BF16))
    return xs[:, CTX_LEN:]
```

```python
import functools

import numpy as np
import jax
import jax.numpy as jnp
from jax import lax
from jax.experimental import pallas as pl
from jax.experimental.pallas import tpu as pltpu

F32 = jnp.float32
BF16 = jnp.bfloat16

D_MODEL = 1024
BATCH = 8
SEQ = 2048
CTX_LEN = 256
L_ALL = CTX_LEN + SEQ
DEPTH = 4
GRID_W = 64
EPS = 1e-6

ROW_TILE = 256
N_ROW_TILES = L_ALL // ROW_TILE
MOD_ROWS = 16
CTX_MOD_ROW = BATCH
VMEM_LIMIT = 56 * 1024 * 1024

GDN_HD = 128
GDN_QK_HEADS = 8
GDN_V_HEADS = 16
GDN_KEY_DIM = 1024
GDN_VAL_DIM = 2048
GDN_MAIN = 2 * GDN_KEY_DIM + 2 * GDN_VAL_DIM
GDN_AB = 64
CHUNK = 64
N_CHUNKS = L_ALL // CHUNK
CTX_CHUNKS = CTX_LEN // CHUNK
PAIR = 2 * CHUNK
A_GROUP = 4
INV_LEAF = 8

LRU_WIDTH = 1024
LRU_BLOCK = 128
LRU_BLOCKS = 8
LRU_C = 8.0
LRU_ROWS = 256
SUB = 8
N_GROUPS = L_ALL // SUB
CTX_GROUPS = CTX_LEN // SUB

ATT_HD = 64
ATT_KV_HEADS = 4
ATT_GROUP = 4
ATT_BLOCK = 128
ATT_WINDOW = 128
ATT_QW = ATT_GROUP * ATT_HD
ATT_COLS = 4 * D_MODEL
NEG = -1e30


def _silu(x):
    return x * jax.nn.sigmoid(x)


def _softplus(x):
    return jnp.maximum(x, 0.0) + jnp.log1p(jnp.exp(-jnp.abs(x)))


def _cparams(sem):
    return pltpu.CompilerParams(dimension_semantics=sem, vmem_limit_bytes=VMEM_LIMIT)


def _dot(a, b):
    return jnp.dot(a, b, preferred_element_type=F32)


def _dot_nt(a, b):
    return lax.dot_general(a, b, (((1,), (1,)), ((), ())), preferred_element_type=F32)


def _dot_tn(a, b):
    return lax.dot_general(a, b, (((0,), (0,)), ((), ())), preferred_element_type=F32)


def _mod_body(c_ref, w_ref, b_ref, o_ref):
    h = _silu(c_ref[...]).astype(BF16)
    o_ref[0] = _dot(h, w_ref[0].astype(BF16)) + b_ref[0]


def _modulation(cc, ada_w, ada_b):
    n_col = 3
    return pl.pallas_call(
        _mod_body,
        grid=(DEPTH, n_col),
        in_specs=[
            pl.BlockSpec((MOD_ROWS, D_MODEL), lambda i, n: (0, 0)),
            pl.BlockSpec((1, D_MODEL, D_MODEL), lambda i, n: (i, 0, n)),
            pl.BlockSpec((1, 1, D_MODEL), lambda i, n: (i, 0, n)),
        ],
        out_specs=pl.BlockSpec((1, MOD_ROWS, D_MODEL), lambda i, n: (i, 0, n)),
        out_shape=jax.ShapeDtypeStruct((DEPTH, MOD_ROWS, 3 * D_MODEL), F32),
        compiler_params=_cparams(("parallel", "parallel")),
    )(cc, ada_w, ada_b.reshape(DEPTH, 1, 3 * D_MODEL))


def _mod_index(b, t):
    return (jnp.where(t == 0, CTX_MOD_ROW, b), 0, 0)


def _in_proj_body(x_ref, mod_ref, nw_ref, w_ref, *refs, segs, rope_groups):
    if rope_groups:
        cos_ref, sin_ref = refs[:2]
        refs = refs[2:]
    x = x_ref[0]
    y = x * lax.rsqrt(jnp.mean(x * x, axis=-1, keepdims=True) + EPS) * nw_ref[...]
    shift = mod_ref[0, :, 0:D_MODEL]
    scale = mod_ref[0, :, D_MODEL:2 * D_MODEL]
    h = (y * (1.0 + scale) + shift).astype(BF16)
    for ref, (c0, width) in zip(refs, segs):
        step = min(width, 512)
        for n0 in range(0, width, step):
            acc = _dot(h, w_ref[:, c0 + n0:c0 + n0 + step])
            if (c0 + n0) // 512 in rope_groups:
                cos = cos_ref[...]
                sin = sin_ref[...]
                parts = []
                for g0 in range(0, step, 256):
                    x1 = acc[:, g0:g0 + 128]
                    x2 = acc[:, g0 + 128:g0 + 256]
                    parts += [x1 * cos - x2 * sin, x2 * cos + x1 * sin]
                acc = jnp.concatenate(parts, axis=1)
            ref[0, :, n0:n0 + step] = acc.astype(ref.dtype)


def _in_proj(x, mod, nw, w, segs, out_dtypes, rope=None):
    n_total = w.shape[1]
    in_specs = [
        pl.BlockSpec((1, ROW_TILE, D_MODEL), lambda b, t: (b, t, 0)),
        pl.BlockSpec((1, 1, 3 * D_MODEL), _mod_index),
        pl.BlockSpec((1, D_MODEL), lambda b, t: (0, 0)),
        pl.BlockSpec((D_MODEL, n_total), lambda b, t: (0, 0), pipeline_mode=pl.Buffered(1)),
    ]
    args = [x, mod, nw.reshape(1, D_MODEL), w]
    rope_groups = ()
    if rope is not None:
        cos, sin, rope_groups = rope
        in_specs += [pl.BlockSpec((ROW_TILE, 128), lambda b, t: (t, 0))] * 2
        args += [cos, sin]
    out_specs = [pl.BlockSpec((1, ROW_TILE, width), lambda b, t: (b, t, 0)) for _, width in segs]
    bsz = x.shape[0]
    out_shape = [jax.ShapeDtypeStruct((bsz, L_ALL, width), dt) for (_, width), dt in zip(segs, out_dtypes)]
    return pl.pallas_call(
        functools.partial(_in_proj_body, segs=tuple(segs), rope_groups=tuple(rope_groups)),
        grid=(bsz, N_ROW_TILES),
        in_specs=in_specs,
        out_specs=out_specs,
        out_shape=out_shape,
        compiler_params=_cparams(("parallel", "parallel")),
    )(*args)


def _out_proj_body(o_ref, z_ref, x_ref, mod_ref, nw_ref, w_ref, out_ref):
    g = (o_ref[0].astype(F32) * _silu(z_ref[0].astype(F32))).astype(BF16)
    y = _dot(g, w_ref[...])
    yn = y * lax.rsqrt(jnp.mean(y * y, axis=-1, keepdims=True) + EPS) * nw_ref[...]
    gate = mod_ref[0, :, 2 * D_MODEL:3 * D_MODEL]
    out_ref[0] = x_ref[0] + gate * yn


def _out_proj(o, z_arr, z_block, x, mod, nw, w):
    width = o.shape[-1]
    bsz = o.shape[0]
    return pl.pallas_call(
        _out_proj_body,
        grid=(bsz, N_ROW_TILES),
        in_specs=[
            pl.BlockSpec((1, ROW_TILE, width), lambda b, t: (b, t, 0)),
            pl.BlockSpec((1, ROW_TILE, width), lambda b, t: (b, t, z_block)),
            pl.BlockSpec((1, ROW_TILE, D_MODEL), lambda b, t: (b, t, 0)),
            pl.BlockSpec((1, 1, 3 * D_MODEL), _mod_index),
            pl.BlockSpec((1, D_MODEL), lambda b, t: (0, 0)),
            pl.BlockSpec((width, D_MODEL), lambda b, t: (0, 0), pipeline_mode=pl.Buffered(1)),
        ],
        out_specs=pl.BlockSpec((1, ROW_TILE, D_MODEL), lambda b, t: (b, t, 0)),
        out_shape=jax.ShapeDtypeStruct((bsz, L_ALL, D_MODEL), F32),
        compiler_params=_cparams(("parallel", "parallel")),
    )(o, z_arr, x, mod, nw.reshape(1, D_MODEL), w)


def _conv_rows(ref, r0, rows, width_sl, cw, left_ok, right_ok):
    total = ref.shape[1]
    main = ref[0, pl.ds(r0, rows), width_sl]
    prev = ref[0, pl.ds(pl.multiple_of(jnp.maximum(r0 - SUB, 0), SUB), SUB), width_sl] * left_ok
    nxt = ref[0, pl.ds(pl.multiple_of(jnp.minimum(r0 + rows, total - SUB), SUB), SUB), width_sl] * right_ok
    xw = jnp.concatenate([prev, main, nxt], axis=0)
    return (cw[0:1] * xw[SUB - 2:SUB - 2 + rows] + cw[1:2] * xw[SUB - 1:SUB - 1 + rows]
            + cw[2:3] * xw[SUB:SUB + rows] + cw[3:4] * xw[SUB + 1:SUB + 1 + rows])


def _gdn_consts():
    i = np.arange(PAIR)
    same = (i[:, None] // CHUNK) == (i[None, :] // CHUNK)
    t_i, t_m = i[:, None] % CHUNK, i[None, :] % CHUNK
    low = same & (t_m <= t_i)
    upp = same & (t_m >= t_i)
    tri_col = np.concatenate([low, upp, same], axis=0).astype(np.float32)
    expand = (i[:, None] // CHUNK) == (np.arange(2 * GDN_HD)[None, :] // GDN_HD)
    tri_row = np.concatenate([low.T, upp.T, expand], axis=1).astype(np.float32)
    mats = np.stack([
        np.where(low, 0.0, NEG), (same & (t_m < t_i)).astype(np.float32),
        np.where(upp, 0.0, NEG), (same & (t_m > t_i)).astype(np.float32),
        np.eye(PAIR, dtype=np.float32)]).astype(np.float32)
    levels = [(i[:, None] // INV_LEAF) == (i[None, :] // INV_LEAF)]
    size = INV_LEAF
    while size < CHUNK:
        levels.append(((i[:, None] // (2 * size)) == (i[None, :] // (2 * size)))
                      & ((i[:, None] // size) != (i[None, :] // size)))
        size *= 2
    lvl = np.stack(levels).astype(np.float32)
    return (jnp.asarray(tri_col, BF16), jnp.asarray(tri_row, BF16), jnp.asarray(mats, F32),
            jnp.asarray(lvl, BF16))


def _split_bf16(x):
    hi = x.astype(BF16)
    lo = (x - hi.astype(F32)).astype(BF16)
    return hi, lo


def _gdn_body(q_ref, k_ref, v_ref, cwq_ref, cwk_ref, cwv_ref, abc_ref, abr_ref, pc_ref, pr_ref,
              tric_ref, trir_ref, mats_ref, lvl_ref, gn_ref, o_ref,
              u_s, wq_s, kd_s, qk_s, dec_s, st_s):
    full = slice(None)

    def chunk_inputs(c):
        r0 = pl.multiple_of(c * CHUNK, CHUNK)
        left_ok = jnp.where((c == 0) | (c == CTX_CHUNKS), 0.0, 1.0)
        right_ok = jnp.where((c == CTX_CHUNKS - 1) | (c == N_CHUNKS - 1), 0.0, 1.0)
        qc = _silu(_conv_rows(q_ref, r0, CHUNK, full, cwq_ref[...], left_ok, right_ok))
        kc = _silu(_conv_rows(k_ref, r0, CHUNK, full, cwk_ref[...], left_ok, right_ok))
        vc = _silu(_conv_rows(v_ref, r0, CHUNK, full, cwv_ref[...], left_ok, right_ok))
        qn = qc * (lax.rsqrt(jnp.sum(qc * qc, axis=-1, keepdims=True) + 1e-6) * (GDN_HD ** -0.5))
        kn = kc * lax.rsqrt(jnp.sum(kc * kc, axis=-1, keepdims=True) + 1e-6)
        q2 = jnp.concatenate([qn, qn], axis=0)
        k2 = jnp.concatenate([kn, kn], axis=0)
        v2 = jnp.concatenate([vc[:, :GDN_HD], vc[:, GDN_HD:]], axis=0)
        k2b = k2.astype(BF16)
        kk = _dot_nt(k2b, k2b)
        qk = _dot_nt(q2.astype(BF16), k2b)
        xc = abc_ref[0, 0, pl.ds(pl.multiple_of(c * PAIR, PAIR), PAIR), :]
        g_c = -jnp.exp(pc_ref[0, :, 0:4]) * _softplus(xc + pc_ref[0, :, 4:8])
        beta_c = jax.nn.sigmoid(xc)
        ghi, glo = _split_bf16(g_c)
        cs = _dot(tric_ref[...], jnp.concatenate([ghi, glo], axis=1))
        cs = cs[:, 0:4] + cs[:, 4:8]
        xr = abr_ref[0, 0, c]
        g_r = -jnp.exp(pr_ref[0, 0:4, :]) * _softplus(xr + pr_ref[0, 4:8, :])
        ghi, glo = _split_bf16(g_r)
        csr = _dot(jnp.concatenate([ghi, glo], axis=0), trir_ref[...])
        csr = csr[0:4] + csr[4:8]
        return dict(c=c, q2=q2, k2=k2, v2=v2, kk=kk, qk=qk, beta_c=beta_c, cs=cs, csr=csr)

    def chain_start(ci, d):
        la, lb = 2 * d, 2 * d + 1
        cs, csr = ci["cs"], ci["csr"]
        gcc = cs[d * PAIR:(d + 1) * PAIR, la:la + 1]
        totc = cs[2 * PAIR:3 * PAIR, la:la + 1]
        gcr = csr[la:la + 1, d * PAIR:(d + 1) * PAIR]
        dec = jnp.exp(csr[la:la + 1, 2 * PAIR:2 * PAIR + 2 * GDN_HD])
        bc = ci["beta_c"][:, lb:lb + 1]
        m = jnp.exp((gcc - gcr) + mats_ref[2 * d])
        a = ci["kk"] * m * mats_ref[2 * d + 1] * bc
        egc = jnp.exp(gcc)
        idx = d * N_CHUNKS + ci["c"]
        kd_s[idx] = (ci["k2"] * jnp.exp(totc - gcc)).astype(BF16)
        qk_s[idx] = (ci["qk"] * m).astype(BF16)
        dec_s[idx] = jnp.broadcast_to(dec, (SUB, 2 * GDN_HD))
        rhs = jnp.concatenate([ci["v2"] * bc, ci["k2"] * (bc * egc)], axis=1).astype(BF16)
        qd = (ci["q2"] * egc).astype(BF16)
        return dict(idx=idx, a=a.astype(BF16), rhs=rhs, qd=qd)

    def phase_a(it, carry):
        chunks = [chunk_inputs(it * A_GROUP + g) for g in range(A_GROUP)]
        chains = [chain_start(ci, d) for ci in chunks for d in range(2)]
        n_levels = lvl_ref.shape[0]
        for ch in chains:
            leaf = ch["a"] * lvl_ref[0]
            ch["x"] = mats_ref[4] - leaf.astype(F32)
            ch["p"] = _dot(leaf, leaf)
        for ch in chains:
            xp = _dot(jnp.concatenate([ch["x"], ch["p"]], axis=0).astype(BF16), ch["p"].astype(BF16))
            ch["x"] = ch["x"] + xp[:PAIR]
            ch["p"] = xp[PAIR:]
        for ch in chains:
            ch["x"] = ch["x"] + _dot(ch["x"].astype(BF16), ch["p"].astype(BF16))
        for lv in range(1, n_levels):
            for ch in chains:
                ch["xb"] = ch["x"].astype(BF16)
                ch["y"] = _dot(ch["a"] * lvl_ref[lv], ch["xb"])
            for ch in chains:
                ch["x"] = ch["x"] - _dot(ch["xb"], ch["y"].astype(BF16))
        for ch in chains:
            uw = _dot(ch["x"].astype(BF16), ch["rhs"])
            u_s[ch["idx"]] = uw[:, :GDN_HD]
            wq_s[ch["idx"]] = jnp.concatenate([uw[:, GDN_HD:].astype(BF16), ch["qd"]], axis=0)
        return carry

    lax.fori_loop(0, N_CHUNKS // A_GROUP, phase_a, 0)

    st_s[...] = jnp.zeros_like(st_s)
    o_ref[...] = jnp.zeros_like(o_ref)
    row_is_h0 = lax.broadcasted_iota(jnp.int32, (PAIR, GDN_HD), 0) < CHUNK

    def phase_b(s, carry):
        chunk_b = jnp.where(s < CTX_CHUNKS, CTX_CHUNKS - 1 - s, N_CHUNKS + CTX_CHUNKS - 1 - s)
        chains = [dict(d=0, c=s, idx=s), dict(d=1, c=chunk_b, idx=N_CHUNKS + chunk_b)]
        for ch in chains:
            ch["st"] = st_s[ch["d"]]
            ch["r"] = _dot(wq_s[ch["idx"]], ch["st"].astype(BF16))
        for ch in chains:
            r = ch["r"]
            ws = jnp.concatenate([r[0:CHUNK, :GDN_HD], r[CHUNK:PAIR, GDN_HD:]], axis=0)
            ch["qs"] = jnp.concatenate([r[PAIR:PAIR + CHUNK, :GDN_HD], r[PAIR + CHUNK:, GDN_HD:]], axis=0)
            ch["vnew"] = u_s[ch["idx"]] - ws
            ch["vb"] = ch["vnew"].astype(BF16)
        for ch in chains:
            ch["ov"] = _dot(qk_s[ch["idx"]], ch["vb"])
            zero = jnp.zeros_like(ch["vb"])
            vbd = jnp.concatenate([jnp.where(row_is_h0, ch["vb"], zero), jnp.where(row_is_h0, zero, ch["vb"])],
                                  axis=1)
            ch["upd"] = _dot_tn(kd_s[ch["idx"]], vbd)
        for ch in chains:
            o = ch["qs"] + ch["ov"]
            r0 = pl.multiple_of(ch["c"] * CHUNK, CHUNK)
            o_ref[0, pl.ds(r0, CHUNK), 0:GDN_HD] += o[:CHUNK]
            o_ref[0, pl.ds(r0, CHUNK), GDN_HD:] += o[CHUNK:]
            st_s[ch["d"]] = ch["st"] * dec_s[ch["idx"]][0:1] + ch["upd"]
        return carry

    lax.fori_loop(0, N_CHUNKS, phase_b, 0)

    def finish(c, carry):
        r0 = pl.multiple_of(c * CHUNK, CHUNK)
        for h in range(2):
            sl = slice(h * GDN_HD, (h + 1) * GDN_HD)
            x = o_ref[0, pl.ds(r0, CHUNK), sl]
            o_ref[0, pl.ds(r0, CHUNK), sl] = (
                x * lax.rsqrt(jnp.mean(x * x, axis=-1, keepdims=True) + EPS) * gn_ref[...])
        return carry

    lax.fori_loop(0, N_CHUNKS, finish, 0)


def _gdn_mixer(p_main, p_ab, conv_w, a_log, dt_bias, g_norm):
    tri_col, tri_row, mats, lvl = _gdn_consts()
    bsz = p_main.shape[0]
    ab = p_ab.reshape(bsz, N_CHUNKS, CHUNK, 2, 2, GDN_QK_HEADS, 2)
    abc = ab.transpose(0, 5, 1, 6, 2, 4, 3).reshape(bsz, GDN_QK_HEADS, N_CHUNKS * PAIR, 4)
    abr = ab.transpose(0, 5, 1, 4, 3, 6, 2).reshape(bsz, GDN_QK_HEADS, N_CHUNKS, 4, PAIR)
    prm = jnp.stack([a_log, dt_bias], axis=0).astype(F32).reshape(2, 2, GDN_QK_HEADS, 2)
    lane = jnp.stack([prm[0], prm[0], prm[1], prm[1]], axis=0)
    col = jnp.stack([lane[0, 0], lane[0, 0], lane[0, 1], lane[0, 1],
                     lane[2, 0], lane[2, 0], lane[2, 1], lane[2, 1]], axis=0)
    pc = jnp.repeat(col.transpose(1, 2, 0), CHUNK, axis=1)
    pr = pc.transpose(0, 2, 1)

    grid = (bsz, GDN_QK_HEADS)
    kq = GDN_KEY_DIM // GDN_HD
    in_specs = [
        pl.BlockSpec((1, L_ALL, GDN_HD), lambda b, j: (b, 0, j)),
        pl.BlockSpec((1, L_ALL, GDN_HD), lambda b, j: (b, 0, kq + j)),
        pl.BlockSpec((1, L_ALL, 2 * GDN_HD), lambda b, j: (b, 0, kq + j)),
        pl.BlockSpec((4, GDN_HD), lambda b, j: (0, j)),
        pl.BlockSpec((4, GDN_HD), lambda b, j: (0, kq + j)),
        pl.BlockSpec((4, 2 * GDN_HD), lambda b, j: (0, kq + j)),
        pl.BlockSpec((1, 1, N_CHUNKS * PAIR, 4), lambda b, j: (b, j, 0, 0)),
        pl.BlockSpec((1, 1, N_CHUNKS, 4, PAIR), lambda b, j: (b, j, 0, 0, 0)),
        pl.BlockSpec((1, PAIR, 8), lambda b, j: (j, 0, 0)),
        pl.BlockSpec((1, 8, PAIR), lambda b, j: (j, 0, 0)),
        pl.BlockSpec((3 * PAIR, PAIR), lambda b, j: (0, 0)),
        pl.BlockSpec((PAIR, 4 * PAIR), lambda b, j: (0, 0)),
        pl.BlockSpec((5, PAIR, PAIR), lambda b, j: (0, 0, 0)),
        pl.BlockSpec(lvl.shape, lambda b, j: (0, 0, 0)),
        pl.BlockSpec((1, GDN_HD), lambda b, j: (0, 0)),
    ]
    return pl.pallas_call(
        _gdn_body,
        grid=grid,
        in_specs=in_specs,
        out_specs=pl.BlockSpec((1, L_ALL, 2 * GDN_HD), lambda b, j: (b, 0, j)),
        out_shape=jax.ShapeDtypeStruct((bsz, L_ALL, GDN_VAL_DIM), F32),
        scratch_shapes=[
            pltpu.VMEM((2 * N_CHUNKS, PAIR, GDN_HD), F32),
            pltpu.VMEM((2 * N_CHUNKS, 2 * PAIR, GDN_HD), BF16),
            pltpu.VMEM((2 * N_CHUNKS, PAIR, GDN_HD), BF16),
            pltpu.VMEM((2 * N_CHUNKS, PAIR, PAIR), BF16),
            pltpu.VMEM((2 * N_CHUNKS, SUB, 2 * GDN_HD), F32),
            pltpu.VMEM((2, GDN_HD, 2 * GDN_HD), F32),
        ],
        compiler_params=_cparams(("parallel", "parallel")),
    )(p_main, p_main, p_main, conv_w, conv_w, conv_w, abc, abr, pc, pr, tri_col, tri_row, mats, lvl,
      g_norm.reshape(1, GDN_HD))


def _lru_body(u_ref, cw_ref, cb_ref, wra_ref, wri_ref, bra_ref, bri_ref, lam_ref, o_ref, a_s, b_s):
    full = slice(None)
    n_steps = L_ALL // LRU_ROWS

    def gates(c, carry):
        r0 = pl.multiple_of(c * LRU_ROWS, LRU_ROWS)
        left_ok = jnp.where(c >= 2, 1.0, 0.0)
        right_ok = jnp.where((c == 0) | (c == n_steps - 1), 0.0, 1.0)
        u = _conv_rows(u_ref, r0, LRU_ROWS, full, cw_ref[...], left_ok, right_ok) + cb_ref[...]
        ub = u.astype(BF16)
        for d in range(2):
            r = jax.nn.sigmoid(_dot(ub, wra_ref[d, 0]) + bra_ref[d:d + 1])
            i = jax.nn.sigmoid(_dot(ub, wri_ref[d, 0]) + bri_ref[d:d + 1])
            log_a = -LRU_C * r * _softplus(-lam_ref[d:d + 1])
            a = jnp.exp(log_a)
            one_m_a2 = -jnp.tanh(log_a) * (a * a + 1.0)
            a_s[d, pl.ds(r0, LRU_ROWS), :] = a
            b_s[d, pl.ds(r0, LRU_ROWS), :] = jnp.sqrt(one_m_a2) * (i * u)
        return carry

    lax.fori_loop(0, n_steps, gates, 0)

    o_ref[...] = jnp.zeros_like(o_ref)
    row = lax.broadcasted_iota(jnp.int32, (SUB, LRU_BLOCK), 0)

    def scan(g, carry):
        hf, hb = carry
        gb = jnp.where(g < CTX_GROUPS, CTX_GROUPS - 1 - g, N_GROUPS + CTX_GROUPS - 1 - g)
        r0 = pl.multiple_of(g * SUB, SUB)
        a, b = a_s[0, pl.ds(r0, SUB), :], b_s[0, pl.ds(r0, SUB), :]
        for s in (1, 2, 4):
            ok = row >= s
            b = jnp.where(ok, a * pltpu.roll(b, s, 0) + b, b)
            a = jnp.where(ok, a * pltpu.roll(a, s, 0), a)
        h = b + a * hf
        o_ref[0, pl.ds(r0, SUB), :] += h
        hf = h[SUB - 1:SUB]
        r0 = pl.multiple_of(gb * SUB, SUB)
        a, b = a_s[1, pl.ds(r0, SUB), :], b_s[1, pl.ds(r0, SUB), :]
        for s in (1, 2, 4):
            ok = row < SUB - s
            b = jnp.where(ok, a * pltpu.roll(b, SUB - s, 0) + b, b)
            a = jnp.where(ok, a * pltpu.roll(a, SUB - s, 0), a)
        h = b + a * hb
        o_ref[0, pl.ds(r0, SUB), :] += h
        hb = h[0:1]
        return hf, hb

    zero = jnp.zeros((1, LRU_BLOCK), F32)
    lax.fori_loop(0, N_GROUPS, scan, (zero, zero))


def _lru_mixer(p, conv_w, conv_b, w_ra, w_ri, b_ra, b_ri, lam):
    nb = LRU_WIDTH // LRU_BLOCK
    bsz = p.shape[0]
    vec = lambda rows: pl.BlockSpec((rows, LRU_BLOCK), lambda b, n: (0, n))
    wspec = pl.BlockSpec((2, 1, LRU_BLOCK, LRU_BLOCK), lambda b, n: (0, n, 0, 0))
    return pl.pallas_call(
        _lru_body,
        grid=(bsz, LRU_BLOCKS),
        in_specs=[
            pl.BlockSpec((1, L_ALL, LRU_BLOCK), lambda b, n: (b, 0, nb + n)),
            vec(4), vec(1), wspec, wspec, vec(2), vec(2), vec(2),
        ],
        out_specs=pl.BlockSpec((1, L_ALL, LRU_BLOCK), lambda b, n: (b, 0, n)),
        out_shape=jax.ShapeDtypeStruct((bsz, L_ALL, LRU_WIDTH), F32),
        scratch_shapes=[pltpu.VMEM((2, L_ALL, LRU_BLOCK), F32), pltpu.VMEM((2, L_ALL, LRU_BLOCK), F32)],
        compiler_params=_cparams(("parallel", "parallel")),
    )(p, conv_w, conv_b.reshape(1, LRU_WIDTH), w_ra.astype(BF16), w_ri.astype(BF16), b_ra, b_ri, lam)


def _att_body(sink_ref, q_ref, kc_ref, vc_ref, k0_ref, k1_ref, k2_ref, v0_ref, v1_ref, v2_ref, o_ref):
    i = pl.program_id(1)
    n_ctx_blocks = CTX_LEN // ATT_BLOCK
    li = i - n_ctx_blocks
    qpos = li * ATT_BLOCK + lax.broadcasted_iota(jnp.int32, (ATT_BLOCK, 3 * ATT_BLOCK), 0)
    kpos = (li - 1) * ATT_BLOCK + lax.broadcasted_iota(jnp.int32, (ATT_BLOCK, 3 * ATT_BLOCK), 1)
    band = (jnp.abs(qpos - kpos) <= ATT_WINDOW) & (kpos >= 0) & (kpos < SEQ) & (li >= 0)
    bias = jnp.concatenate([jnp.zeros((ATT_BLOCK, CTX_LEN), F32), jnp.where(band, 0.0, NEG)], axis=1)
    bias4 = jnp.concatenate([bias] * ATT_GROUP, axis=0)
    n_keys = CTX_LEN + 3 * ATT_BLOCK
    lane = lax.broadcasted_iota(jnp.int32, (ATT_BLOCK, ATT_QW), 1)
    qmask = [(lane % 128) // 32 == g for g in range(ATT_GROUP)]
    omask = [lane // ATT_HD == g for g in range(ATT_GROUP)]
    rowg = lax.broadcasted_iota(jnp.int32, (ATT_GROUP * ATT_BLOCK, 1), 0) // ATT_BLOCK
    for h in range(ATT_KV_HEADS):
        sl = slice(h * ATT_QW, (h + 1) * ATT_QW)
        qh = q_ref[0, :, sl]
        qm = jnp.concatenate([jnp.where(qmask[g], qh, 0.0) for g in range(ATT_GROUP)], axis=0).astype(BF16)
        keys = jnp.concatenate([kc_ref[0, :, sl], k0_ref[0, :, sl], k1_ref[0, :, sl], k2_ref[0, :, sl]],
                               axis=0).astype(BF16)
        vals = jnp.concatenate([vc_ref[0, :, sl], v0_ref[0, :, sl], v1_ref[0, :, sl], v2_ref[0, :, sl]],
                               axis=0).astype(BF16)
        s = _dot_nt(qm, keys) + bias4
        sink = jnp.zeros((ATT_GROUP * ATT_BLOCK, 1), F32)
        for g in range(ATT_GROUP):
            sink = jnp.where(rowg == g, sink_ref[h * ATT_GROUP + g], sink)
        mx = jnp.maximum(jnp.max(s, axis=-1, keepdims=True), sink)
        e = jnp.exp(s - mx)
        den = jnp.sum(e, axis=-1, keepdims=True) + jnp.exp(sink - mx)
        r = _dot((e / den).astype(BF16), vals)
        out = jnp.zeros((ATT_BLOCK, ATT_QW), F32)
        for g in range(ATT_GROUP):
            out = jnp.where(omask[g], r[g * ATT_BLOCK:(g + 1) * ATT_BLOCK], out)
        o_ref[0, :, sl] = out
    del n_keys


def _att_mixer(p, sinks):
    n_ctx_blocks = CTX_LEN // ATT_BLOCK
    n_lat_blocks = SEQ // ATT_BLOCK
    bsz = p.shape[0]

    def win(off, col):
        def index(b, i, sink_ref):
            blk = jnp.clip(i - n_ctx_blocks + off, 0, n_lat_blocks - 1)
            return (b, n_ctx_blocks + blk, col)
        return pl.BlockSpec((1, ATT_BLOCK, D_MODEL), index)

    grid_spec = pltpu.PrefetchScalarGridSpec(
        num_scalar_prefetch=1,
        grid=(bsz, L_ALL // ATT_BLOCK),
        in_specs=[
            pl.BlockSpec((1, ATT_BLOCK, D_MODEL), lambda b, i, s: (b, i, 0)),
            pl.BlockSpec((1, CTX_LEN, D_MODEL), lambda b, i, s: (b, 0, 2)),
            pl.BlockSpec((1, CTX_LEN, D_MODEL), lambda b, i, s: (b, 0, 3)),
            win(-1, 2), win(0, 2), win(1, 2), win(-1, 3), win(0, 3), win(1, 3),
        ],
        out_specs=pl.BlockSpec((1, ATT_BLOCK, D_MODEL), lambda b, i, s: (b, i, 0)),
    )
    return pl.pallas_call(
        _att_body,
        grid_spec=grid_spec,
        out_shape=jax.ShapeDtypeStruct((bsz, L_ALL, D_MODEL), F32),
        compiler_params=_cparams(("parallel", "arbitrary")),
    )(sinks.astype(F32), p, p, p, p, p, p, p, p, p)


def _att_weight_layout(w_in):
    half = ATT_HD // 2
    q_cols, k_cols, v_cols = [], [], []
    for h in range(ATT_KV_HEADS):
        for part in range(2):
            for g in range(ATT_GROUP):
                base = (h * ATT_GROUP + g) * ATT_HD + part * half
                q_cols += list(range(base, base + half))
                kb = D_MODEL + h * ATT_HD + part * half
                k_cols += list(range(kb, kb + half))
        vb = D_MODEL + ATT_KV_HEADS * ATT_HD + h * ATT_HD
        v_cols += list(range(vb, vb + ATT_HD)) * ATT_GROUP
    z0 = D_MODEL + 2 * ATT_KV_HEADS * ATT_HD
    z_cols = list(range(z0, z0 + D_MODEL))
    order = np.asarray(q_cols + z_cols + k_cols + v_cols, np.int32)
    scale = np.ones((ATT_COLS,), np.float32)
    scale[:D_MODEL] = ATT_HD ** -0.5
    return (w_in[:, order] * scale).astype(BF16)


def _rope_tables():
    rows = SEQ // GRID_W
    row = jnp.repeat(jnp.arange(rows), GRID_W)
    col = jnp.tile(jnp.arange(GRID_W), rows)
    n_freq = ATT_HD // 4
    inv = 10000.0 ** (-jnp.arange(n_freq, dtype=F32) / n_freq)
    ang = jnp.concatenate([row[:, None] * inv, col[:, None] * inv], axis=-1)
    cos = jnp.concatenate([jnp.ones((CTX_LEN, ATT_HD // 2), F32), jnp.cos(ang)], axis=0)
    sin = jnp.concatenate([jnp.zeros((CTX_LEN, ATT_HD // 2), F32), jnp.sin(ang)], axis=0)
    return jnp.tile(cos, (1, ATT_GROUP)), jnp.tile(sin, (1, ATT_GROUP))


def kernel(x, c, ctx, c_ctx, ada_w, ada_b, norm_pre, norm_post, gdn_w_in, gdn_conv_w, gdn_a_log, gdn_dt_bias, gdn_g_norm, gdn_w_out, lru_w_in, lru_conv_w, lru_conv_b, lru_w_ra, lru_b_ra, lru_w_ri, lru_b_ri, lru_lam, lru_w_out, att_w_in, att_sinks, att_w_out):
    cc = jnp.zeros((MOD_ROWS, D_MODEL), F32).at[:c.shape[0]].set(c).at[CTX_MOD_ROW].set(c_ctx)
    mod_all = _modulation(cc, ada_w, ada_b).reshape(DEPTH, MOD_ROWS, 1, 3 * D_MODEL)
    xs = jnp.concatenate([ctx, x], axis=1)
    cos, sin = _rope_tables()
    for i in range(DEPTH):
        kind, j = i % 3, i // 3
        mod = mod_all[i]
        if kind == 0:
            p_main, p_ab = _in_proj(xs, mod, norm_pre[i], gdn_w_in[j].astype(BF16),
                                    [(0, GDN_MAIN), (GDN_MAIN, GDN_AB)], [F32, F32])
            o = _gdn_mixer(p_main, p_ab, gdn_conv_w[j], gdn_a_log[j], gdn_dt_bias[j], gdn_g_norm[j])
            xs = _out_proj(o, p_main, 2, xs, mod, norm_post[i], gdn_w_out[j].astype(BF16))
        elif kind == 1:
            (p,) = _in_proj(xs, mod, norm_pre[i], lru_w_in[j].astype(BF16), [(0, 2 * LRU_WIDTH)], [F32])
            o = _lru_mixer(p, lru_conv_w[j], lru_conv_b[j], lru_w_ra[j], lru_w_ri[j],
                           lru_b_ra[j], lru_b_ri[j], lru_lam[j])
            xs = _out_proj(o, p, 0, xs, mod, norm_post[i], lru_w_out[j].astype(BF16))
        else:
            (p,) = _in_proj(xs, mod, norm_pre[i], _att_weight_layout(att_w_in[j]), [(0, ATT_COLS)], [F32],
                            rope=(cos, sin, (0, 1, 4, 5)))
            o = _att_mixer(p, att_sinks[j])
            xs = _out_proj(o, p, 1, xs, mod, norm_post[i], att_w_out[j].astype(BF16))
    return xs[:, CTX_LEN:]
```

```python
import functools

import numpy as np
import jax
import jax.numpy as jnp
from jax import lax
from jax.experimental import pallas as pl
from jax.experimental.pallas import tpu as pltpu

F32 = jnp.float32
BF16 = jnp.bfloat16

D_MODEL = 1024
BATCH = 8
SEQ = 2048
CTX_LEN = 256
L_ALL = CTX_LEN + SEQ
DEPTH = 4
GRID_W = 64
EPS = 1e-6

ROW_TILE = 256
N_ROW_TILES = L_ALL // ROW_TILE
MOD_ROWS = 16
CTX_MOD_ROW = BATCH
VMEM_LIMIT = 56 * 1024 * 1024

GDN_HD = 128
GDN_QK_HEADS = 8
GDN_V_HEADS = 16
GDN_KEY_DIM = 1024
GDN_VAL_DIM = 2048
GDN_MAIN = 2 * GDN_KEY_DIM + 2 * GDN_VAL_DIM
GDN_AB = 64
CHUNK = 64
N_CHUNKS = L_ALL // CHUNK
CTX_CHUNKS = CTX_LEN // CHUNK
PAIR = 2 * CHUNK
A_GROUP = 9
INV_LEAF = 8

LRU_WIDTH = 1024
LRU_BLOCK = 128
LRU_BLOCKS = 8
LRU_C = 8.0
LRU_ROWS = 256
SUB = 8
N_GROUPS = L_ALL // SUB
CTX_GROUPS = CTX_LEN // SUB

ATT_HD = 64
ATT_KV_HEADS = 4
ATT_GROUP = 4
ATT_BLOCK = 128
ATT_WINDOW = 128
ATT_QW = ATT_GROUP * ATT_HD
ATT_COLS = 4 * D_MODEL
NEG = -1e30


def _silu(x):
    return x * jax.nn.sigmoid(x)


def _softplus(x):
    return jnp.maximum(x, 0.0) + jnp.log1p(jnp.exp(-jnp.abs(x)))


def _cparams(sem):
    return pltpu.CompilerParams(dimension_semantics=sem, vmem_limit_bytes=VMEM_LIMIT)


def _dot(a, b):
    return jnp.dot(a, b, preferred_element_type=F32)


def _dot_nt(a, b):
    return lax.dot_general(a, b, (((1,), (1,)), ((), ())), preferred_element_type=F32)


def _dot_tn(a, b):
    return lax.dot_general(a, b, (((0,), (0,)), ((), ())), preferred_element_type=F32)


def _mod_body(c_ref, w_ref, b_ref, o_ref):
    h = _silu(c_ref[...]).astype(BF16)
    o_ref[0] = _dot(h, w_ref[0].astype(BF16)) + b_ref[0]


def _modulation(cc, ada_w, ada_b):
    n_col = 3
    return pl.pallas_call(
        _mod_body,
        grid=(DEPTH, n_col),
        in_specs=[
            pl.BlockSpec((MOD_ROWS, D_MODEL), lambda i, n: (0, 0)),
            pl.BlockSpec((1, D_MODEL, D_MODEL), lambda i, n: (i, 0, n)),
            pl.BlockSpec((1, 1, D_MODEL), lambda i, n: (i, 0, n)),
        ],
        out_specs=pl.BlockSpec((1, MOD_ROWS, D_MODEL), lambda i, n: (i, 0, n)),
        out_shape=jax.ShapeDtypeStruct((DEPTH, MOD_ROWS, 3 * D_MODEL), F32),
        compiler_params=_cparams(("parallel", "parallel")),
    )(cc, ada_w, ada_b.reshape(DEPTH, 1, 3 * D_MODEL))


def _mod_index(b, t):
    return (jnp.where(t == 0, CTX_MOD_ROW, b), 0, 0)


def _in_proj_body(x_ref, mod_ref, nw_ref, w_ref, *refs, segs, rope_groups):
    if rope_groups:
        cos_ref, sin_ref = refs[:2]
        refs = refs[2:]
    x = x_ref[0]
    y = x * lax.rsqrt(jnp.mean(x * x, axis=-1, keepdims=True) + EPS) * nw_ref[...]
    shift = mod_ref[0, :, 0:D_MODEL]
    scale = mod_ref[0, :, D_MODEL:2 * D_MODEL]
    h = (y * (1.0 + scale) + shift).astype(BF16)
    for ref, (c0, width) in zip(refs, segs):
        step = min(width, 512)
        for n0 in range(0, width, step):
            acc = _dot(h, w_ref[:, c0 + n0:c0 + n0 + step])
            if (c0 + n0) // 512 in rope_groups:
                cos = cos_ref[...]
                sin = sin_ref[...]
                parts = []
                for g0 in range(0, step, 256):
                    x1 = acc[:, g0:g0 + 128]
                    x2 = acc[:, g0 + 128:g0 + 256]
                    parts += [x1 * cos - x2 * sin, x2 * cos + x1 * sin]
                acc = jnp.concatenate(parts, axis=1)
            ref[0, :, n0:n0 + step] = acc.astype(ref.dtype)


def _in_proj(x, mod, nw, w, segs, out_dtypes, rope=None):
    n_total = w.shape[1]
    in_specs = [
        pl.BlockSpec((1, ROW_TILE, D_MODEL), lambda b, t: (b, t, 0)),
        pl.BlockSpec((1, 1, 3 * D_MODEL), _mod_index),
        pl.BlockSpec((1, D_MODEL), lambda b, t: (0, 0)),
        pl.BlockSpec((D_MODEL, n_total), lambda b, t: (0, 0), pipeline_mode=pl.Buffered(1)),
    ]
    args = [x, mod, nw.reshape(1, D_MODEL), w]
    rope_groups = ()
    if rope is not None:
        cos, sin, rope_groups = rope
        in_specs += [pl.BlockSpec((ROW_TILE, 128), lambda b, t: (t, 0))] * 2
        args += [cos, sin]
    out_specs = [pl.BlockSpec((1, ROW_TILE, width), lambda b, t: (b, t, 0)) for _, width in segs]
    bsz = x.shape[0]
    out_shape = [jax.ShapeDtypeStruct((bsz, L_ALL, width), dt) for (_, width), dt in zip(segs, out_dtypes)]
    return pl.pallas_call(
        functools.partial(_in_proj_body, segs=tuple(segs), rope_groups=tuple(rope_groups)),
        grid=(bsz, N_ROW_TILES),
        in_specs=in_specs,
        out_specs=out_specs,
        out_shape=out_shape,
        compiler_params=_cparams(("parallel", "parallel")),
    )(*args)


def _out_proj_body(o_ref, z_ref, x_ref, mod_ref, nw_ref, w_ref, out_ref):
    g = (o_ref[0].astype(F32) * _silu(z_ref[0].astype(F32))).astype(BF16)
    y = _dot(g, w_ref[...])
    yn = y * lax.rsqrt(jnp.mean(y * y, axis=-1, keepdims=True) + EPS) * nw_ref[...]
    gate = mod_ref[0, :, 2 * D_MODEL:3 * D_MODEL]
    out_ref[0] = x_ref[0] + gate * yn


def _out_proj(o, z_arr, z_block, x, mod, nw, w):
    width = o.shape[-1]
    bsz = o.shape[0]
    return pl.pallas_call(
        _out_proj_body,
        grid=(bsz, N_ROW_TILES),
        in_specs=[
            pl.BlockSpec((1, ROW_TILE, width), lambda b, t: (b, t, 0)),
            pl.BlockSpec((1, ROW_TILE, width), lambda b, t: (b, t, z_block)),
            pl.BlockSpec((1, ROW_TILE, D_MODEL), lambda b, t: (b, t, 0)),
            pl.BlockSpec((1, 1, 3 * D_MODEL), _mod_index),
            pl.BlockSpec((1, D_MODEL), lambda b, t: (0, 0)),
            pl.BlockSpec((width, D_MODEL), lambda b, t: (0, 0), pipeline_mode=pl.Buffered(1)),
        ],
        out_specs=pl.BlockSpec((1, ROW_TILE, D_MODEL), lambda b, t: (b, t, 0)),
        out_shape=jax.ShapeDtypeStruct((bsz, L_ALL, D_MODEL), F32),
        compiler_params=_cparams(("parallel", "parallel")),
    )(o, z_arr, x, mod, nw.reshape(1, D_MODEL), w)


def _conv_rows(ref, r0, rows, width_sl, cw, left_ok, right_ok):
    total = ref.shape[1]
    main = ref[0, pl.ds(r0, rows), width_sl]
    prev = ref[0, pl.ds(pl.multiple_of(jnp.maximum(r0 - SUB, 0), SUB), SUB), width_sl] * left_ok
    nxt = ref[0, pl.ds(pl.multiple_of(jnp.minimum(r0 + rows, total - SUB), SUB), SUB), width_sl] * right_ok
    xw = jnp.concatenate([prev, main, nxt], axis=0)
    return (cw[0:1] * xw[SUB - 2:SUB - 2 + rows] + cw[1:2] * xw[SUB - 1:SUB - 1 + rows]
            + cw[2:3] * xw[SUB:SUB + rows] + cw[3:4] * xw[SUB + 1:SUB + 1 + rows])


def _gdn_consts():
    i = np.arange(PAIR)
    same = (i[:, None] // CHUNK) == (i[None, :] // CHUNK)
    t_i, t_m = i[:, None] % CHUNK, i[None, :] % CHUNK
    low = same & (t_m <= t_i)
    upp = same & (t_m >= t_i)
    expand = (i[:, None] // CHUNK) == (np.arange(2 * GDN_HD)[None, :] // GDN_HD)
    tri_row = np.concatenate([low.T, upp.T, same, expand], axis=1).astype(np.float32)
    both = lambda f, b: np.concatenate([f, b], axis=1)
    eye = np.eye(PAIR, dtype=np.float32)
    mats = np.stack([
        both(np.where(low, 0.0, NEG), np.where(upp, 0.0, NEG)),
        both(same & (t_m < t_i), same & (t_m > t_i)).astype(np.float32),
        both(eye, eye)]).astype(np.float32)
    levels = [(i[:, None] // INV_LEAF) == (i[None, :] // INV_LEAF)]
    size = INV_LEAF
    while size < CHUNK:
        levels.append(((i[:, None] // (2 * size)) == (i[None, :] // (2 * size)))
                      & ((i[:, None] // size) != (i[None, :] // size)))
        size *= 2
    lvl = np.stack([both(m, m) for m in levels]).astype(np.float32)
    return jnp.asarray(tri_row, BF16), jnp.asarray(mats, F32), jnp.asarray(lvl, BF16)


def _block_diag(x):
    zero = jnp.zeros((PAIR, PAIR), x.dtype)
    return jnp.concatenate([jnp.concatenate([x[:, :PAIR], zero], axis=1),
                            jnp.concatenate([zero, x[:, PAIR:]], axis=1)], axis=0)


def _gdn_body(q_ref, k_ref, v_ref, cwq_ref, cwk_ref, cwv_ref, abr_ref, pr_ref,
              trir_ref, mats_ref, lvl_ref, gn_ref, o_ref,
              pad_s, lhs_s, c_s, o0_s, dec_s, st_s):
    pad_s[:, 0:SUB, :] = jnp.zeros((4, SUB, GDN_HD), F32)
    pad_s[:, SUB + CTX_LEN:2 * SUB + CTX_LEN, :] = jnp.zeros((4, SUB, GDN_HD), F32)
    pad_s[:, 2 * SUB + L_ALL:3 * SUB + L_ALL, :] = jnp.zeros((4, SUB, GDN_HD), F32)

    def fill(t, carry):
        src = pl.multiple_of(t * ROW_TILE, ROW_TILE)
        dst = pl.multiple_of(src + jnp.where(t == 0, SUB, 2 * SUB), SUB)
        pad_s[0, pl.ds(dst, ROW_TILE), :] = q_ref[0, pl.ds(src, ROW_TILE), :]
        pad_s[1, pl.ds(dst, ROW_TILE), :] = k_ref[0, pl.ds(src, ROW_TILE), :]
        pad_s[2, pl.ds(dst, ROW_TILE), :] = v_ref[0, pl.ds(src, ROW_TILE), 0:GDN_HD]
        pad_s[3, pl.ds(dst, ROW_TILE), :] = v_ref[0, pl.ds(src, ROW_TILE), GDN_HD:]
        return carry

    lax.fori_loop(0, N_ROW_TILES, fill, 0)
    row_is_h0 = lax.broadcasted_iota(jnp.int32, (PAIR, GDN_HD), 0) < CHUNK

    def conv(which, base, cw):
        taps = [pad_s[which, pl.ds(base + (k - 2), CHUNK), :] for k in range(4)]
        return _silu(cw[0:1] * taps[0] + cw[1:2] * taps[1] + cw[2:3] * taps[2] + cw[3:4] * taps[3])

    def lanes2(f, b):
        return jnp.concatenate([jnp.broadcast_to(f, (PAIR, PAIR)), jnp.broadcast_to(b, (PAIR, PAIR))], axis=1)

    def chunk_start(c):
        base = c * CHUNK + jnp.where(c < CTX_CHUNKS, SUB, 2 * SUB)
        cwv = cwv_ref[...]
        qc = conv(0, base, cwq_ref[...])
        kc = conv(1, base, cwk_ref[...])
        v2 = jnp.concatenate([conv(2, base, cwv[:, :GDN_HD]), conv(3, base, cwv[:, GDN_HD:])], axis=0)
        qn = qc * (lax.rsqrt(jnp.sum(qc * qc, axis=-1, keepdims=True) + 1e-6) * (GDN_HD ** -0.5))
        kn = kc * lax.rsqrt(jnp.sum(kc * kc, axis=-1, keepdims=True) + 1e-6)
        q2 = jnp.concatenate([qn, qn], axis=0)
        k2 = jnp.concatenate([kn, kn], axis=0)
        k2b = k2.astype(BF16)
        kk = _dot_nt(k2b, k2b)
        qk = _dot_nt(q2.astype(BF16), k2b)
        xr = abr_ref[0, 0, c]
        g_r = -jnp.exp(pr_ref[0, 0:4, :]) * _softplus(xr + pr_ref[0, 4:8, :])
        beta_r = jax.nn.sigmoid(xr)
        g_hi = g_r.astype(BF16).astype(F32)
        csr = _dot(jnp.concatenate([g_hi, g_r - g_hi], axis=0).astype(BF16), trir_ref[...])
        csr = csr[0:4] + csr[4:8]
        gc_f, gc_b = csr[0:1, 0:PAIR], csr[2:3, PAIR:2 * PAIR]
        rows = jnp.concatenate([gc_f, gc_b, csr[0:1, 2 * PAIR:3 * PAIR], csr[2:3, 2 * PAIR:3 * PAIR],
                                beta_r[1:2], beta_r[3:4], jnp.zeros((PAIR - 6, PAIR), F32)], axis=0)
        cols = rows.T
        m = jnp.exp(lanes2(cols[:, 0:1], cols[:, 1:2]) - jnp.concatenate([gc_f, gc_b], axis=1) + mats_ref[0])
        a = (jnp.concatenate([kk, kk], axis=1) * m * mats_ref[1]
             * lanes2(cols[:, 4:5], cols[:, 5:6])).astype(BF16)
        qkm = (jnp.concatenate([qk, qk], axis=1) * m).astype(BF16)
        dirs = []
        for d in range(2):
            gcc, totc, bc = cols[:, d:d + 1], cols[:, 2 + d:3 + d], cols[:, 4 + d:5 + d]
            egc = jnp.exp(gcc)
            idx = d * N_CHUNKS + c
            dec = jnp.exp(csr[2 * d:2 * d + 1, 3 * PAIR:3 * PAIR + 2 * GDN_HD])
            dec_s[idx] = jnp.broadcast_to(dec, (SUB, 2 * GDN_HD))
            dirs.append(dict(idx=idx, qd=q2 * egc, kd=(k2 * jnp.exp(totc - gcc)).astype(BF16),
                             rhs=jnp.concatenate([v2 * bc, k2 * (bc * egc)], axis=1).astype(BF16),
                             qkm=qkm[:, d * PAIR:(d + 1) * PAIR]))
        return dict(a=a, dirs=dirs)

    def mm(p, q):
        return _dot(p.astype(BF16), _block_diag(q.astype(BF16)))

    def phase_a(it, carry):
        chunks = [chunk_start(it * A_GROUP + g) for g in range(A_GROUP)]
        n_levels = lvl_ref.shape[0]
        for ch in chunks:
            leaf = ch["a"] * lvl_ref[0]
            ch["x"] = mats_ref[2] - leaf.astype(F32)
            ch["p"] = mm(leaf, leaf)
        for ch in chunks:
            xp = mm(jnp.concatenate([ch["x"], ch["p"]], axis=0), ch["p"])
            ch["x"] = ch["x"] + xp[:PAIR]
            ch["p"] = xp[PAIR:]
        for ch in chunks:
            ch["x"] = ch["x"] + mm(ch["x"], ch["p"])
        for lv in range(1, n_levels):
            for ch in chunks:
                ch["y"] = mm(ch["a"] * lvl_ref[lv], ch["x"])
            for ch in chunks:
                ch["x"] = ch["x"] - mm(ch["x"], ch["y"])
        chains = []
        for ch in chunks:
            xb = ch["x"].astype(BF16)
            for d, cd in enumerate(ch["dirs"]):
                cd["uw"] = _dot(xb[:, d * PAIR:(d + 1) * PAIR], cd["rhs"]).astype(BF16)
                chains.append(cd)
        for cd in chains:
            u, w = cd["uw"][:, :GDN_HD], cd["uw"][:, GDN_HD:]
            zero = jnp.zeros_like(u)
            wu_bd = jnp.concatenate([jnp.where(row_is_h0, w, zero), jnp.where(row_is_h0, zero, w),
                                     jnp.where(row_is_h0, u, zero), jnp.where(row_is_h0, zero, u)], axis=1)
            cd["nc"] = _dot_tn(cd["kd"], wu_bd)
            cd["qo"] = _dot(cd["qkm"], jnp.concatenate([w, u], axis=1))
        for cd in chains:
            nc, qo, idx = cd["nc"], cd["qo"], cd["idx"]
            lhs_s[idx] = jnp.concatenate([-nc[:, 0:GDN_HD], -nc[:, GDN_HD:2 * GDN_HD],
                                          cd["qd"] - qo[:, :GDN_HD]], axis=0).astype(BF16)
            c_s[idx] = nc[:, 2 * GDN_HD:]
            o0_s[idx] = qo[:, GDN_HD:]
        return carry

    lax.fori_loop(0, N_CHUNKS // A_GROUP, phase_a, 0)

    st_s[...] = jnp.zeros_like(st_s)
    o_ref[...] = jnp.zeros_like(o_ref)

    def phase_b(s, carry):
        chunk_b = jnp.where(s < CTX_CHUNKS, CTX_CHUNKS - 1 - s, N_CHUNKS + CTX_CHUNKS - 1 - s)
        chains = [dict(d=0, c=s, idx=s), dict(d=1, c=chunk_b, idx=N_CHUNKS + chunk_b)]
        for ch in chains:
            ch["st"] = st_s[ch["d"]]
            ch["r"] = _dot(lhs_s[ch["idx"]], ch["st"].astype(BF16))
        for ch in chains:
            r, idx = ch["r"], ch["idx"]
            ns = jnp.concatenate([r[0:GDN_HD, :GDN_HD], r[GDN_HD:2 * GDN_HD, GDN_HD:]], axis=1)
            st_s[ch["d"]] = ch["st"] * dec_s[idx][0:1] + ns + c_s[idx]
            o0 = o0_s[idx]
            r0 = pl.multiple_of(ch["c"] * CHUNK, CHUNK)
            o_ref[0, pl.ds(r0, CHUNK), 0:GDN_HD] += r[2 * GDN_HD:2 * GDN_HD + CHUNK, :GDN_HD] + o0[:CHUNK]
            o_ref[0, pl.ds(r0, CHUNK), GDN_HD:] += r[2 * GDN_HD + CHUNK:, GDN_HD:] + o0[CHUNK:]
        return carry

    lax.fori_loop(0, N_CHUNKS, phase_b, 0)

    def finish(t, carry):
        r0 = pl.multiple_of(t * ROW_TILE, ROW_TILE)
        for h in range(2):
            sl = slice(h * GDN_HD, (h + 1) * GDN_HD)
            x = o_ref[0, pl.ds(r0, ROW_TILE), sl]
            o_ref[0, pl.ds(r0, ROW_TILE), sl] = (
                x * lax.rsqrt(jnp.mean(x * x, axis=-1, keepdims=True) + EPS) * gn_ref[...])
        return carry

    lax.fori_loop(0, N_ROW_TILES, finish, 0)


def _gdn_mixer(p_main, p_ab, conv_w, a_log, dt_bias, g_norm):
    tri_row, mats, lvl = _gdn_consts()
    bsz = p_main.shape[0]
    ab = p_ab.reshape(bsz, N_CHUNKS, CHUNK, 2, 2, GDN_QK_HEADS, 2)
    abr = ab.transpose(0, 5, 1, 4, 3, 6, 2).reshape(bsz, GDN_QK_HEADS, N_CHUNKS, 4, PAIR)
    prm = jnp.stack([a_log, dt_bias], axis=0).astype(F32).reshape(2, 2, GDN_QK_HEADS, 2)
    rows = jnp.stack([prm[0, 0], prm[0, 0], prm[0, 1], prm[0, 1],
                      prm[1, 0], prm[1, 0], prm[1, 1], prm[1, 1]], axis=0)
    pr = jnp.repeat(rows.transpose(1, 0, 2), CHUNK, axis=2)

    grid = (bsz, GDN_QK_HEADS)
    kq = GDN_KEY_DIM // GDN_HD
    in_specs = [
        pl.BlockSpec((1, L_ALL, GDN_HD), lambda b, j: (b, 0, j)),
        pl.BlockSpec((1, L_ALL, GDN_HD), lambda b, j: (b, 0, kq + j)),
        pl.BlockSpec((1, L_ALL, 2 * GDN_HD), lambda b, j: (b, 0, kq + j)),
        pl.BlockSpec((4, GDN_HD), lambda b, j: (0, j)),
        pl.BlockSpec((4, GDN_HD), lambda b, j: (0, kq + j)),
        pl.BlockSpec((4, 2 * GDN_HD), lambda b, j: (0, kq + j)),
        pl.BlockSpec((1, 1, N_CHUNKS, 4, PAIR), lambda b, j: (b, j, 0, 0, 0)),
        pl.BlockSpec((1, 8, PAIR), lambda b, j: (j, 0, 0)),
        pl.BlockSpec(tri_row.shape, lambda b, j: (0, 0)),
        pl.BlockSpec(mats.shape, lambda b, j: (0, 0, 0)),
        pl.BlockSpec(lvl.shape, lambda b, j: (0, 0, 0)),
        pl.BlockSpec((1, GDN_HD), lambda b, j: (0, 0)),
    ]
    return pl.pallas_call(
        _gdn_body,
        grid=grid,
        in_specs=in_specs,
        out_specs=pl.BlockSpec((1, L_ALL, 2 * GDN_HD), lambda b, j: (b, 0, j)),
        out_shape=jax.ShapeDtypeStruct((bsz, L_ALL, GDN_VAL_DIM), F32),
        scratch_shapes=[
            pltpu.VMEM((4, L_ALL + 3 * SUB, GDN_HD), F32),
            pltpu.VMEM((2 * N_CHUNKS, 3 * GDN_HD, GDN_HD), BF16),
            pltpu.VMEM((2 * N_CHUNKS, GDN_HD, 2 * GDN_HD), F32),
            pltpu.VMEM((2 * N_CHUNKS, PAIR, GDN_HD), F32),
            pltpu.VMEM((2 * N_CHUNKS, SUB, 2 * GDN_HD), F32),
            pltpu.VMEM((2, GDN_HD, 2 * GDN_HD), F32),
        ],
        compiler_params=_cparams(("parallel", "parallel")),
    )(p_main, p_main, p_main, conv_w, conv_w, conv_w, abr, pr, tri_row, mats, lvl,
      g_norm.reshape(1, GDN_HD))


def _lru_body(u_ref, cw_ref, cb_ref, wra_ref, wri_ref, bra_ref, bri_ref, lam_ref, o_ref, a_s, b_s):
    full = slice(None)
    n_steps = L_ALL // LRU_ROWS

    def gates(c, carry):
        r0 = pl.multiple_of(c * LRU_ROWS, LRU_ROWS)
        left_ok = jnp.where(c >= 2, 1.0, 0.0)
        right_ok = jnp.where((c == 0) | (c == n_steps - 1), 0.0, 1.0)
        u = _conv_rows(u_ref, r0, LRU_ROWS, full, cw_ref[...], left_ok, right_ok) + cb_ref[...]
        ub = u.astype(BF16)
        for d in range(2):
            r = jax.nn.sigmoid(_dot(ub, wra_ref[d, 0]) + bra_ref[d:d + 1])
            i = jax.nn.sigmoid(_dot(ub, wri_ref[d, 0]) + bri_ref[d:d + 1])
            log_a = -LRU_C * r * _softplus(-lam_ref[d:d + 1])
            a = jnp.exp(log_a)
            one_m_a2 = -jnp.tanh(log_a) * (a * a + 1.0)
            a_s[d, pl.ds(r0, LRU_ROWS), :] = a
            b_s[d, pl.ds(r0, LRU_ROWS), :] = jnp.sqrt(one_m_a2) * (i * u)
        return carry

    lax.fori_loop(0, n_steps, gates, 0)

    o_ref[...] = jnp.zeros_like(o_ref)
    row = lax.broadcasted_iota(jnp.int32, (SUB, LRU_BLOCK), 0)

    def scan(g, carry):
        hf, hb = carry
        gb = jnp.where(g < CTX_GROUPS, CTX_GROUPS - 1 - g, N_GROUPS + CTX_GROUPS - 1 - g)
        r0 = pl.multiple_of(g * SUB, SUB)
        a, b = a_s[0, pl.ds(r0, SUB), :], b_s[0, pl.ds(r0, SUB), :]
        for s in (1, 2, 4):
            ok = row >= s
            b = jnp.where(ok, a * pltpu.roll(b, s, 0) + b, b)
            a = jnp.where(ok, a * pltpu.roll(a, s, 0), a)
        h = b + a * hf
        o_ref[0, pl.ds(r0, SUB), :] += h
        hf = h[SUB - 1:SUB]
        r0 = pl.multiple_of(gb * SUB, SUB)
        a, b = a_s[1, pl.ds(r0, SUB), :], b_s[1, pl.ds(r0, SUB), :]
        for s in (1, 2, 4):
            ok = row < SUB - s
            b = jnp.where(ok, a * pltpu.roll(b, SUB - s, 0) + b, b)
            a = jnp.where(ok, a * pltpu.roll(a, SUB - s, 0), a)
        h = b + a * hb
        o_ref[0, pl.ds(r0, SUB), :] += h
        hb = h[0:1]
        return hf, hb

    zero = jnp.zeros((1, LRU_BLOCK), F32)
    lax.fori_loop(0, N_GROUPS, scan, (zero, zero))


def _lru_mixer(p, conv_w, conv_b, w_ra, w_ri, b_ra, b_ri, lam):
    nb = LRU_WIDTH // LRU_BLOCK
    bsz = p.shape[0]
    vec = lambda rows: pl.BlockSpec((rows, LRU_BLOCK), lambda b, n: (0, n))
    wspec = pl.BlockSpec((2, 1, LRU_BLOCK, LRU_BLOCK), lambda b, n: (0, n, 0, 0))
    return pl.pallas_call(
        _lru_body,
        grid=(bsz, LRU_BLOCKS),
        in_specs=[
            pl.BlockSpec((1, L_ALL, LRU_BLOCK), lambda b, n: (b, 0, nb + n)),
            vec(4), vec(1), wspec, wspec, vec(2), vec(2), vec(2),
        ],
        out_specs=pl.BlockSpec((1, L_ALL, LRU_BLOCK), lambda b, n: (b, 0, n)),
        out_shape=jax.ShapeDtypeStruct((bsz, L_ALL, LRU_WIDTH), F32),
        scratch_shapes=[pltpu.VMEM((2, L_ALL, LRU_BLOCK), F32), pltpu.VMEM((2, L_ALL, LRU_BLOCK), F32)],
        compiler_params=_cparams(("parallel", "parallel")),
    )(p, conv_w, conv_b.reshape(1, LRU_WIDTH), w_ra.astype(BF16), w_ri.astype(BF16), b_ra, b_ri, lam)


def _att_body(sink_ref, q_ref, kc_ref, vc_ref, k0_ref, k1_ref, k2_ref, v0_ref, v1_ref, v2_ref, o_ref):
    i = pl.program_id(1)
    n_ctx_blocks = CTX_LEN // ATT_BLOCK
    li = i - n_ctx_blocks
    qpos = li * ATT_BLOCK + lax.broadcasted_iota(jnp.int32, (ATT_BLOCK, 3 * ATT_BLOCK), 0)
    kpos = (li - 1) * ATT_BLOCK + lax.broadcasted_iota(jnp.int32, (ATT_BLOCK, 3 * ATT_BLOCK), 1)
    band = (jnp.abs(qpos - kpos) <= ATT_WINDOW) & (kpos >= 0) & (kpos < SEQ) & (li >= 0)
    bias = jnp.concatenate([jnp.zeros((ATT_BLOCK, CTX_LEN), F32), jnp.where(band, 0.0, NEG)], axis=1)
    bias4 = jnp.concatenate([bias] * ATT_GROUP, axis=0)
    n_keys = CTX_LEN + 3 * ATT_BLOCK
    lane = lax.broadcasted_iota(jnp.int32, (ATT_BLOCK, ATT_QW), 1)
    qmask = [(lane % 128) // 32 == g for g in range(ATT_GROUP)]
    omask = [lane // ATT_HD == g for g in range(ATT_GROUP)]
    rowg = lax.broadcasted_iota(jnp.int32, (ATT_GROUP * ATT_BLOCK, 1), 0) // ATT_BLOCK
    for h in range(ATT_KV_HEADS):
        sl = slice(h * ATT_QW, (h + 1) * ATT_QW)
        qh = q_ref[0, :, sl]
        qm = jnp.concatenate([jnp.where(qmask[g], qh, 0.0) for g in range(ATT_GROUP)], axis=0).astype(BF16)
        keys = jnp.concatenate([kc_ref[0, :, sl], k0_ref[0, :, sl], k1_ref[0, :, sl], k2_ref[0, :, sl]],
                               axis=0).astype(BF16)
        vals = jnp.concatenate([vc_ref[0, :, sl], v0_ref[0, :, sl], v1_ref[0, :, sl], v2_ref[0, :, sl]],
                               axis=0).astype(BF16)
        s = _dot_nt(qm, keys) + bias4
        sink = jnp.zeros((ATT_GROUP * ATT_BLOCK, 1), F32)
        for g in range(ATT_GROUP):
            sink = jnp.where(rowg == g, sink_ref[h * ATT_GROUP + g], sink)
        mx = jnp.maximum(jnp.max(s, axis=-1, keepdims=True), sink)
        e = jnp.exp(s - mx)
        den = jnp.sum(e, axis=-1, keepdims=True) + jnp.exp(sink - mx)
        r = _dot((e / den).astype(BF16), vals)
        out = jnp.zeros((ATT_BLOCK, ATT_QW), F32)
        for g in range(ATT_GROUP):
            out = jnp.where(omask[g], r[g * ATT_BLOCK:(g + 1) * ATT_BLOCK], out)
        o_ref[0, :, sl] = out
    del n_keys


def _att_mixer(p, sinks):
    n_ctx_blocks = CTX_LEN // ATT_BLOCK
    n_lat_blocks = SEQ // ATT_BLOCK
    bsz = p.shape[0]

    def win(off, col):
        def index(b, i, sink_ref):
            blk = jnp.clip(i - n_ctx_blocks + off, 0, n_lat_blocks - 1)
            return (b, n_ctx_blocks + blk, col)
        return pl.BlockSpec((1, ATT_BLOCK, D_MODEL), index)

    grid_spec = pltpu.PrefetchScalarGridSpec(
        num_scalar_prefetch=1,
        grid=(bsz, L_ALL // ATT_BLOCK),
        in_specs=[
            pl.BlockSpec((1, ATT_BLOCK, D_MODEL), lambda b, i, s: (b, i, 0)),
            pl.BlockSpec((1, CTX_LEN, D_MODEL), lambda b, i, s: (b, 0, 2)),
            pl.BlockSpec((1, CTX_LEN, D_MODEL), lambda b, i, s: (b, 0, 3)),
            win(-1, 2), win(0, 2), win(1, 2), win(-1, 3), win(0, 3), win(1, 3),
        ],
        out_specs=pl.BlockSpec((1, ATT_BLOCK, D_MODEL), lambda b, i, s: (b, i, 0)),
    )
    return pl.pallas_call(
        _att_body,
        grid_spec=grid_spec,
        out_shape=jax.ShapeDtypeStruct((bsz, L_ALL, D_MODEL), F32),
        compiler_params=_cparams(("parallel", "arbitrary")),
    )(sinks.astype(F32), p, p, p, p, p, p, p, p, p)


def _att_weight_layout(w_in):
    half = ATT_HD // 2
    q_cols, k_cols, v_cols = [], [], []
    for h in range(ATT_KV_HEADS):
        for part in range(2):
            for g in range(ATT_GROUP):
                base = (h * ATT_GROUP + g) * ATT_HD + part * half
                q_cols += list(range(base, base + half))
                kb = D_MODEL + h * ATT_HD + part * half
                k_cols += list(range(kb, kb + half))
        vb = D_MODEL + ATT_KV_HEADS * ATT_HD + h * ATT_HD
        v_cols += list(range(vb, vb + ATT_HD)) * ATT_GROUP
    z0 = D_MODEL + 2 * ATT_KV_HEADS * ATT_HD
    z_cols = list(range(z0, z0 + D_MODEL))
    order = np.asarray(q_cols + z_cols + k_cols + v_cols, np.int32)
    scale = np.ones((ATT_COLS,), np.float32)
    scale[:D_MODEL] = ATT_HD ** -0.5
    return (w_in[:, order] * scale).astype(BF16)


def _rope_tables():
    rows = SEQ // GRID_W
    row = jnp.repeat(jnp.arange(rows), GRID_W)
    col = jnp.tile(jnp.arange(GRID_W), rows)
    n_freq = ATT_HD // 4
    inv = 10000.0 ** (-jnp.arange(n_freq, dtype=F32) / n_freq)
    ang = jnp.concatenate([row[:, None] * inv, col[:, None] * inv], axis=-1)
    cos = jnp.concatenate([jnp.ones((CTX_LEN, ATT_HD // 2), F32), jnp.cos(ang)], axis=0)
    sin = jnp.concatenate([jnp.zeros((CTX_LEN, ATT_HD // 2), F32), jnp.sin(ang)], axis=0)
    return jnp.tile(cos, (1, ATT_GROUP)), jnp.tile(sin, (1, ATT_GROUP))


def kernel(x, c, ctx, c_ctx, ada_w, ada_b, norm_pre, norm_post, gdn_w_in, gdn_conv_w, gdn_a_log, gdn_dt_bias, gdn_g_norm, gdn_w_out, lru_w_in, lru_conv_w, lru_conv_b, lru_w_ra, lru_b_ra, lru_w_ri, lru_b_ri, lru_lam, lru_w_out, att_w_in, att_sinks, att_w_out):
    cc = jnp.zeros((MOD_ROWS, D_MODEL), F32).at[:c.shape[0]].set(c).at[CTX_MOD_ROW].set(c_ctx)
    mod_all = _modulation(cc, ada_w, ada_b).reshape(DEPTH, MOD_ROWS, 1, 3 * D_MODEL)
    xs = jnp.concatenate([ctx, x], axis=1)
    cos, sin = _rope_tables()
    for i in range(DEPTH):
        kind, j = i % 3, i // 3
        mod = mod_all[i]
        if kind == 0:
            p_main, p_ab = _in_proj(xs, mod, norm_pre[i], gdn_w_in[j].astype(BF16),
                                    [(0, GDN_MAIN), (GDN_MAIN, GDN_AB)], [F32, F32])
            o = _gdn_mixer(p_main, p_ab, gdn_conv_w[j], gdn_a_log[j], gdn_dt_bias[j], gdn_g_norm[j])
            xs = _out_proj(o, p_main, 2, xs, mod, norm_post[i], gdn_w_out[j].astype(BF16))
        elif kind == 1:
            (p,) = _in_proj(xs, mod, norm_pre[i], lru_w_in[j].astype(BF16), [(0, 2 * LRU_WIDTH)], [F32])
            o = _lru_mixer(p, lru_conv_w[j], lru_conv_b[j], lru_w_ra[j], lru_w_ri[j],
                           lru_b_ra[j], lru_b_ri[j], lru_lam[j])
            xs = _out_proj(o, p, 0, xs, mod, norm_post[i], lru_w_out[j].astype(BF16))
        else:
            (p,) = _in_proj(xs, mod, norm_pre[i], _att_weight_layout(att_w_in[j]), [(0, ATT_COLS)], [F32],
                            rope=(cos, sin, (0, 1, 4, 5)))
            o = _att_mixer(p, att_sinks[j])
            xs = _out_proj(o, p, 1, xs, mod, norm_post[i], att_w_out[j].astype(BF16))
    return xs[:, CTX_LEN:]
```

```python
import functools

import numpy as np
import jax
import jax.numpy as jnp
from jax import lax
from jax.experimental import pallas as pl
from jax.experimental.pallas import tpu as pltpu

F32 = jnp.float32
BF16 = jnp.bfloat16
ACT = jnp.bfloat16

D_MODEL = 1024
BATCH = 8
SEQ = 2048
CTX_LEN = 256
L_ALL = CTX_LEN + SEQ
DEPTH = 4
GRID_W = 64
EPS = 1e-6

ROW_TILE = 256
N_ROW_TILES = L_ALL // ROW_TILE
MOD_ROWS = 16
CTX_MOD_ROW = BATCH
VMEM_LIMIT = 56 * 1024 * 1024

GDN_HD = 128
GDN_QK_HEADS = 8
GDN_V_HEADS = 16
GDN_KEY_DIM = 1024
GDN_VAL_DIM = 2048
GDN_MAIN = 2 * GDN_KEY_DIM + 2 * GDN_VAL_DIM
GDN_AB = 64
CHUNK = 64
N_CHUNKS = L_ALL // CHUNK
CTX_CHUNKS = CTX_LEN // CHUNK
PAIR = 2 * CHUNK
A_GROUP = 9
INV_LEAF = 8

LRU_WIDTH = 1024
LRU_BLOCK = 128
LRU_BLOCKS = 8
LRU_C = 8.0
LRU_ROWS = 256
SUB = 8
N_GROUPS = L_ALL // SUB
CTX_GROUPS = CTX_LEN // SUB

ATT_HD = 64
ATT_KV_HEADS = 4
ATT_GROUP = 4
ATT_BLOCK = 128
ATT_WINDOW = 128
ATT_QW = ATT_GROUP * ATT_HD
ATT_COLS = 4 * D_MODEL
NEG = -1e30


def _silu(x):
    return x * jax.nn.sigmoid(x)


def _softplus(x):
    return jnp.maximum(x, 0.0) + jnp.log1p(jnp.exp(-jnp.abs(x)))


def _cparams(sem):
    return pltpu.CompilerParams(dimension_semantics=sem, vmem_limit_bytes=VMEM_LIMIT)


def _dot(a, b):
    return jnp.dot(a, b, preferred_element_type=F32)


def _dot_nt(a, b):
    return lax.dot_general(a, b, (((1,), (1,)), ((), ())), preferred_element_type=F32)


def _dot_tn(a, b):
    return lax.dot_general(a, b, (((0,), (0,)), ((), ())), preferred_element_type=F32)


def _mod_body(c_ref, w_ref, b_ref, o_ref):
    h = _silu(c_ref[...]).astype(BF16)
    o_ref[0] = _dot(h, w_ref[0].astype(BF16)) + b_ref[0]


def _modulation(cc, ada_w, ada_b):
    n_col = 3
    return pl.pallas_call(
        _mod_body,
        grid=(DEPTH, n_col),
        in_specs=[
            pl.BlockSpec((MOD_ROWS, D_MODEL), lambda i, n: (0, 0)),
            pl.BlockSpec((1, D_MODEL, D_MODEL), lambda i, n: (i, 0, n)),
            pl.BlockSpec((1, 1, D_MODEL), lambda i, n: (i, 0, n)),
        ],
        out_specs=pl.BlockSpec((1, MOD_ROWS, D_MODEL), lambda i, n: (i, 0, n)),
        out_shape=jax.ShapeDtypeStruct((DEPTH, MOD_ROWS, 3 * D_MODEL), F32),
        compiler_params=_cparams(("parallel", "parallel")),
    )(cc, ada_w, ada_b.reshape(DEPTH, 1, 3 * D_MODEL))


def _mod_index(b, t):
    return (jnp.where(t == 0, CTX_MOD_ROW, b), 0, 0)


def _in_proj_body(x_ref, mod_ref, nw_ref, w_ref, *refs, segs, rope_groups):
    if rope_groups:
        cos_ref, sin_ref = refs[:2]
        refs = refs[2:]
    x = x_ref[0]
    y = x * lax.rsqrt(jnp.mean(x * x, axis=-1, keepdims=True) + EPS) * nw_ref[...]
    shift = mod_ref[0, :, 0:D_MODEL]
    scale = mod_ref[0, :, D_MODEL:2 * D_MODEL]
    h = (y * (1.0 + scale) + shift).astype(BF16)
    for ref, (c0, width) in zip(refs, segs):
        step = min(width, 512)
        for n0 in range(0, width, step):
            acc = _dot(h, w_ref[:, c0 + n0:c0 + n0 + step])
            if (c0 + n0) // 512 in rope_groups:
                cos = cos_ref[...]
                sin = sin_ref[...]
                parts = []
                for g0 in range(0, step, 256):
                    x1 = acc[:, g0:g0 + 128]
                    x2 = acc[:, g0 + 128:g0 + 256]
                    parts += [x1 * cos - x2 * sin, x2 * cos + x1 * sin]
                acc = jnp.concatenate(parts, axis=1)
            ref[0, :, n0:n0 + step] = acc.astype(ref.dtype)


def _in_proj(x, mod, nw, w, segs, out_dtypes, rope=None):
    n_total = w.shape[1]
    in_specs = [
        pl.BlockSpec((1, ROW_TILE, D_MODEL), lambda b, t: (b, t, 0)),
        pl.BlockSpec((1, 1, 3 * D_MODEL), _mod_index),
        pl.BlockSpec((1, D_MODEL), lambda b, t: (0, 0)),
        pl.BlockSpec((D_MODEL, n_total), lambda b, t: (0, 0), pipeline_mode=pl.Buffered(1)),
    ]
    args = [x, mod, nw.reshape(1, D_MODEL), w]
    rope_groups = ()
    if rope is not None:
        cos, sin, rope_groups = rope
        in_specs += [pl.BlockSpec((ROW_TILE, 128), lambda b, t: (t, 0))] * 2
        args += [cos, sin]
    out_specs = [pl.BlockSpec((1, ROW_TILE, width), lambda b, t: (b, t, 0)) for _, width in segs]
    bsz = x.shape[0]
    out_shape = [jax.ShapeDtypeStruct((bsz, L_ALL, width), dt) for (_, width), dt in zip(segs, out_dtypes)]
    return pl.pallas_call(
        functools.partial(_in_proj_body, segs=tuple(segs), rope_groups=tuple(rope_groups)),
        grid=(bsz, N_ROW_TILES),
        in_specs=in_specs,
        out_specs=out_specs,
        out_shape=out_shape,
        compiler_params=_cparams(("parallel", "parallel")),
    )(*args)


def _out_proj_body(o_ref, z_ref, x_ref, mod_ref, nw_ref, w_ref, out_ref):
    g = (o_ref[0].astype(F32) * _silu(z_ref[0].astype(F32))).astype(BF16)
    y = _dot(g, w_ref[...])
    yn = y * lax.rsqrt(jnp.mean(y * y, axis=-1, keepdims=True) + EPS) * nw_ref[...]
    gate = mod_ref[0, :, 2 * D_MODEL:3 * D_MODEL]
    out_ref[0] = x_ref[0] + gate * yn


def _out_proj(o, z_arr, z_block, x, mod, nw, w, latent_only=False):
    width = o.shape[-1]
    bsz = o.shape[0]
    skip = CTX_LEN // ROW_TILE if latent_only else 0
    return pl.pallas_call(
        _out_proj_body,
        grid=(bsz, N_ROW_TILES - skip),
        in_specs=[
            pl.BlockSpec((1, ROW_TILE, width), lambda b, t: (b, t + skip, 0)),
            pl.BlockSpec((1, ROW_TILE, width), lambda b, t: (b, t + skip, z_block)),
            pl.BlockSpec((1, ROW_TILE, D_MODEL), lambda b, t: (b, t + skip, 0)),
            pl.BlockSpec((1, 1, 3 * D_MODEL), lambda b, t: _mod_index(b, t + skip)),
            pl.BlockSpec((1, D_MODEL), lambda b, t: (0, 0)),
            pl.BlockSpec((width, D_MODEL), lambda b, t: (0, 0), pipeline_mode=pl.Buffered(1)),
        ],
        out_specs=pl.BlockSpec((1, ROW_TILE, D_MODEL), lambda b, t: (b, t, 0)),
        out_shape=jax.ShapeDtypeStruct((bsz, L_ALL - skip * ROW_TILE, D_MODEL), F32),
        compiler_params=_cparams(("parallel", "parallel")),
    )(o, z_arr, x, mod, nw.reshape(1, D_MODEL), w)


def _conv_rows(ref, r0, rows, width_sl, cw, left_ok, right_ok):
    total = ref.shape[1]
    halo = 2 * SUB
    main = ref[0, pl.ds(r0, rows), width_sl].astype(F32)
    prev = ref[0, pl.ds(pl.multiple_of(jnp.maximum(r0 - halo, 0), halo), halo), width_sl].astype(F32) * left_ok
    nxt = ref[0, pl.ds(pl.multiple_of(jnp.minimum(r0 + rows, total - halo), halo), halo),
              width_sl].astype(F32) * right_ok
    xw = jnp.concatenate([prev, main, nxt], axis=0)
    return (cw[0:1] * xw[halo - 2:halo - 2 + rows] + cw[1:2] * xw[halo - 1:halo - 1 + rows]
            + cw[2:3] * xw[halo:halo + rows] + cw[3:4] * xw[halo + 1:halo + 1 + rows])


def _gdn_consts():
    i = np.arange(PAIR)
    same = (i[:, None] // CHUNK) == (i[None, :] // CHUNK)
    t_i, t_m = i[:, None] % CHUNK, i[None, :] % CHUNK
    low = same & (t_m <= t_i)
    upp = same & (t_m >= t_i)
    expand = (i[:, None] // CHUNK) == (np.arange(2 * GDN_HD)[None, :] // GDN_HD)
    tri_row = np.concatenate([low.T, upp.T, same, expand], axis=1).astype(np.float32)
    both = lambda f, b: np.concatenate([f, b], axis=1)
    eye = np.eye(PAIR, dtype=np.float32)
    mats = np.stack([
        both(np.where(low, 0.0, NEG), np.where(upp, 0.0, NEG)),
        both(same & (t_m < t_i), same & (t_m > t_i)).astype(np.float32),
        both(eye, eye)]).astype(np.float32)
    levels = [(i[:, None] // INV_LEAF) == (i[None, :] // INV_LEAF)]
    size = INV_LEAF
    while size < CHUNK:
        levels.append(((i[:, None] // (2 * size)) == (i[None, :] // (2 * size)))
                      & ((i[:, None] // size) != (i[None, :] // size)))
        size *= 2
    lvl = np.stack([both(m, m) for m in levels]).astype(np.float32)
    return jnp.asarray(tri_row, BF16), jnp.asarray(mats, F32), jnp.asarray(lvl, BF16)


def _block_diag(x):
    zero = jnp.zeros((PAIR, PAIR), x.dtype)
    return jnp.concatenate([jnp.concatenate([x[:, :PAIR], zero], axis=1),
                            jnp.concatenate([zero, x[:, PAIR:]], axis=1)], axis=0)


def _gdn_body(q_ref, k_ref, v_ref, cwq_ref, cwk_ref, cwv_ref, abr_ref, pr_ref,
              trir_ref, mats_ref, lvl_ref, gn_ref, o_ref,
              pad_s, lhs_s, c_s, o0_s, dec_s, st_s, acc_s):
    pad_s[:, 0:SUB, :] = jnp.zeros((4, SUB, GDN_HD), F32)
    pad_s[:, SUB + CTX_LEN:2 * SUB + CTX_LEN, :] = jnp.zeros((4, SUB, GDN_HD), F32)
    pad_s[:, 2 * SUB + L_ALL:3 * SUB + L_ALL, :] = jnp.zeros((4, SUB, GDN_HD), F32)

    def fill(t, carry):
        src = pl.multiple_of(t * ROW_TILE, ROW_TILE)
        dst = pl.multiple_of(src + jnp.where(t == 0, SUB, 2 * SUB), SUB)
        pad_s[0, pl.ds(dst, ROW_TILE), :] = q_ref[0, pl.ds(src, ROW_TILE), :].astype(F32)
        pad_s[1, pl.ds(dst, ROW_TILE), :] = k_ref[0, pl.ds(src, ROW_TILE), :].astype(F32)
        pad_s[2, pl.ds(dst, ROW_TILE), :] = v_ref[0, pl.ds(src, ROW_TILE), 0:GDN_HD].astype(F32)
        pad_s[3, pl.ds(dst, ROW_TILE), :] = v_ref[0, pl.ds(src, ROW_TILE), GDN_HD:].astype(F32)
        return carry

    lax.fori_loop(0, N_ROW_TILES, fill, 0)
    row_is_h0 = lax.broadcasted_iota(jnp.int32, (PAIR, GDN_HD), 0) < CHUNK

    def conv(which, base, cw):
        taps = [pad_s[which, pl.ds(base + (k - 2), CHUNK), :] for k in range(4)]
        return _silu(cw[0:1] * taps[0] + cw[1:2] * taps[1] + cw[2:3] * taps[2] + cw[3:4] * taps[3])

    def lanes2(f, b):
        return jnp.concatenate([jnp.broadcast_to(f, (PAIR, PAIR)), jnp.broadcast_to(b, (PAIR, PAIR))], axis=1)

    def chunk_start(c):
        base = c * CHUNK + jnp.where(c < CTX_CHUNKS, SUB, 2 * SUB)
        cwv = cwv_ref[...]
        qc = conv(0, base, cwq_ref[...])
        kc = conv(1, base, cwk_ref[...])
        v2 = jnp.concatenate([conv(2, base, cwv[:, :GDN_HD]), conv(3, base, cwv[:, GDN_HD:])], axis=0)
        qn = qc * (lax.rsqrt(jnp.sum(qc * qc, axis=-1, keepdims=True) + 1e-6) * (GDN_HD ** -0.5))
        kn = kc * lax.rsqrt(jnp.sum(kc * kc, axis=-1, keepdims=True) + 1e-6)
        q2 = jnp.concatenate([qn, qn], axis=0)
        k2 = jnp.concatenate([kn, kn], axis=0)
        k2b = k2.astype(BF16)
        kk = _dot_nt(k2b, k2b)
        qk = _dot_nt(q2.astype(BF16), k2b)
        xr = abr_ref[0, 0, c]
        g_r = -jnp.exp(pr_ref[0, 0:4, :]) * _softplus(xr + pr_ref[0, 4:8, :])
        beta_r = jax.nn.sigmoid(xr)
        g_hi = g_r.astype(BF16).astype(F32)
        csr = _dot(jnp.concatenate([g_hi, g_r - g_hi], axis=0).astype(BF16), trir_ref[...])
        csr = csr[0:4] + csr[4:8]
        gc_f, gc_b = csr[0:1, 0:PAIR], csr[2:3, PAIR:2 * PAIR]
        rows = jnp.concatenate([gc_f, gc_b, csr[0:1, 2 * PAIR:3 * PAIR], csr[2:3, 2 * PAIR:3 * PAIR],
                                beta_r[1:2], beta_r[3:4], jnp.zeros((PAIR - 6, PAIR), F32)], axis=0)
        cols = rows.T
        m = jnp.exp(lanes2(cols[:, 0:1], cols[:, 1:2]) - jnp.concatenate([gc_f, gc_b], axis=1) + mats_ref[0])
        a = (jnp.concatenate([kk, kk], axis=1) * m * mats_ref[1]
             * lanes2(cols[:, 4:5], cols[:, 5:6])).astype(BF16)
        qkm = (jnp.concatenate([qk, qk], axis=1) * m).astype(BF16)
        dirs = []
        for d in range(2):
            gcc, totc, bc = cols[:, d:d + 1], cols[:, 2 + d:3 + d], cols[:, 4 + d:5 + d]
            egc = jnp.exp(gcc)
            idx = d * N_CHUNKS + c
            dec = jnp.exp(csr[2 * d:2 * d + 1, 3 * PAIR:3 * PAIR + 2 * GDN_HD])
            dec_s[idx] = jnp.broadcast_to(dec, (SUB, 2 * GDN_HD))
            dirs.append(dict(idx=idx, qd=q2 * egc, kd=(k2 * jnp.exp(totc - gcc)).astype(BF16),
                             rhs=jnp.concatenate([v2 * bc, k2 * (bc * egc)], axis=1).astype(BF16),
                             qkm=qkm[:, d * PAIR:(d + 1) * PAIR]))
        return dict(a=a, dirs=dirs)

    def mm(p, q):
        return _dot(p.astype(BF16), _block_diag(q.astype(BF16)))

    def phase_a(it, carry):
        chunks = [chunk_start(it * A_GROUP + g) for g in range(A_GROUP)]
        n_levels = lvl_ref.shape[0]
        for ch in chunks:
            leaf = ch["a"] * lvl_ref[0]
            ch["x"] = mats_ref[2] - leaf.astype(F32)
            ch["p"] = mm(leaf, leaf)
        for ch in chunks:
            xp = mm(jnp.concatenate([ch["x"], ch["p"]], axis=0), ch["p"])
            ch["x"] = ch["x"] + xp[:PAIR]
            ch["p"] = xp[PAIR:]
        for ch in chunks:
            ch["x"] = ch["x"] + mm(ch["x"], ch["p"])
        for lv in range(1, n_levels):
            for ch in chunks:
                ch["y"] = mm(ch["a"] * lvl_ref[lv], ch["x"])
            for ch in chunks:
                ch["x"] = ch["x"] - mm(ch["x"], ch["y"])
        chains = []
        for ch in chunks:
            xb = ch["x"].astype(BF16)
            for d, cd in enumerate(ch["dirs"]):
                cd["uw"] = _dot(xb[:, d * PAIR:(d + 1) * PAIR], cd["rhs"]).astype(BF16)
                chains.append(cd)
        for cd in chains:
            u, w = cd["uw"][:, :GDN_HD], cd["uw"][:, GDN_HD:]
            zero = jnp.zeros_like(u)
            wu_bd = jnp.concatenate([jnp.where(row_is_h0, w, zero), jnp.where(row_is_h0, zero, w),
                                     jnp.where(row_is_h0, u, zero), jnp.where(row_is_h0, zero, u)], axis=1)
            cd["nc"] = _dot_tn(cd["kd"], wu_bd)
            cd["qo"] = _dot(cd["qkm"], jnp.concatenate([w, u], axis=1))
        for cd in chains:
            nc, qo, idx = cd["nc"], cd["qo"], cd["idx"]
            lhs_s[idx] = jnp.concatenate([-nc[:, 0:GDN_HD], -nc[:, GDN_HD:2 * GDN_HD],
                                          cd["qd"] - qo[:, :GDN_HD]], axis=0).astype(BF16)
            c_s[idx] = nc[:, 2 * GDN_HD:]
            o0_s[idx] = qo[:, GDN_HD:]
        return carry

    lax.fori_loop(0, N_CHUNKS // A_GROUP, phase_a, 0)

    st_s[...] = jnp.zeros_like(st_s)
    acc_s[...] = jnp.zeros_like(acc_s)

    def phase_b(s, carry):
        chunk_b = jnp.where(s < CTX_CHUNKS, CTX_CHUNKS - 1 - s, N_CHUNKS + CTX_CHUNKS - 1 - s)
        chains = [dict(d=0, c=s, idx=s), dict(d=1, c=chunk_b, idx=N_CHUNKS + chunk_b)]
        for ch in chains:
            ch["st"] = st_s[ch["d"]]
            ch["r"] = _dot(lhs_s[ch["idx"]], ch["st"].astype(BF16))
        for ch in chains:
            r, idx = ch["r"], ch["idx"]
            ns = jnp.concatenate([r[0:GDN_HD, :GDN_HD], r[GDN_HD:2 * GDN_HD, GDN_HD:]], axis=1)
            st_s[ch["d"]] = ch["st"] * dec_s[idx][0:1] + ns + c_s[idx]
            o0 = o0_s[idx]
            r0 = pl.multiple_of(ch["c"] * CHUNK, CHUNK)
            acc_s[pl.ds(r0, CHUNK), 0:GDN_HD] += r[2 * GDN_HD:2 * GDN_HD + CHUNK, :GDN_HD] + o0[:CHUNK]
            acc_s[pl.ds(r0, CHUNK), GDN_HD:] += r[2 * GDN_HD + CHUNK:, GDN_HD:] + o0[CHUNK:]
        return carry

    lax.fori_loop(0, N_CHUNKS, phase_b, 0)

    def finish(t, carry):
        r0 = pl.multiple_of(t * ROW_TILE, ROW_TILE)
        for h in range(2):
            sl = slice(h * GDN_HD, (h + 1) * GDN_HD)
            x = acc_s[pl.ds(r0, ROW_TILE), sl]
            o_ref[0, pl.ds(r0, ROW_TILE), sl] = (
                x * lax.rsqrt(jnp.mean(x * x, axis=-1, keepdims=True) + EPS) * gn_ref[...]).astype(o_ref.dtype)
        return carry

    lax.fori_loop(0, N_ROW_TILES, finish, 0)


def _gdn_mixer(p_main, p_ab, conv_w, a_log, dt_bias, g_norm):
    tri_row, mats, lvl = _gdn_consts()
    bsz = p_main.shape[0]
    ab = p_ab.reshape(bsz, N_CHUNKS, CHUNK, 2, 2, GDN_QK_HEADS, 2)
    abr = ab.transpose(0, 5, 1, 4, 3, 6, 2).reshape(bsz, GDN_QK_HEADS, N_CHUNKS, 4, PAIR)
    prm = jnp.stack([a_log, dt_bias], axis=0).astype(F32).reshape(2, 2, GDN_QK_HEADS, 2)
    rows = jnp.stack([prm[0, 0], prm[0, 0], prm[0, 1], prm[0, 1],
                      prm[1, 0], prm[1, 0], prm[1, 1], prm[1, 1]], axis=0)
    pr = jnp.repeat(rows.transpose(1, 0, 2), CHUNK, axis=2)

    grid = (bsz, GDN_QK_HEADS)
    kq = GDN_KEY_DIM // GDN_HD
    in_specs = [
        pl.BlockSpec((1, L_ALL, GDN_HD), lambda b, j: (b, 0, j)),
        pl.BlockSpec((1, L_ALL, GDN_HD), lambda b, j: (b, 0, kq + j)),
        pl.BlockSpec((1, L_ALL, 2 * GDN_HD), lambda b, j: (b, 0, kq + j)),
        pl.BlockSpec((4, GDN_HD), lambda b, j: (0, j)),
        pl.BlockSpec((4, GDN_HD), lambda b, j: (0, kq + j)),
        pl.BlockSpec((4, 2 * GDN_HD), lambda b, j: (0, kq + j)),
        pl.BlockSpec((1, 1, N_CHUNKS, 4, PAIR), lambda b, j: (b, j, 0, 0, 0)),
        pl.BlockSpec((1, 8, PAIR), lambda b, j: (j, 0, 0)),
        pl.BlockSpec(tri_row.shape, lambda b, j: (0, 0)),
        pl.BlockSpec(mats.shape, lambda b, j: (0, 0, 0)),
        pl.BlockSpec(lvl.shape, lambda b, j: (0, 0, 0)),
        pl.BlockSpec((1, GDN_HD), lambda b, j: (0, 0)),
    ]
    return pl.pallas_call(
        _gdn_body,
        grid=grid,
        in_specs=in_specs,
        out_specs=pl.BlockSpec((1, L_ALL, 2 * GDN_HD), lambda b, j: (b, 0, j)),
        out_shape=jax.ShapeDtypeStruct((bsz, L_ALL, GDN_VAL_DIM), ACT),
        scratch_shapes=[
            pltpu.VMEM((4, L_ALL + 3 * SUB, GDN_HD), F32),
            pltpu.VMEM((2 * N_CHUNKS, 3 * GDN_HD, GDN_HD), BF16),
            pltpu.VMEM((2 * N_CHUNKS, GDN_HD, 2 * GDN_HD), F32),
            pltpu.VMEM((2 * N_CHUNKS, PAIR, GDN_HD), F32),
            pltpu.VMEM((2 * N_CHUNKS, SUB, 2 * GDN_HD), F32),
            pltpu.VMEM((2, GDN_HD, 2 * GDN_HD), F32),
            pltpu.VMEM((L_ALL, 2 * GDN_HD), F32),
        ],
        compiler_params=_cparams(("parallel", "parallel")),
    )(p_main, p_main, p_main, conv_w, conv_w, conv_w, abr, pr, tri_row, mats, lvl,
      g_norm.reshape(1, GDN_HD))


def _lru_body(u_ref, cw_ref, cb_ref, wra_ref, wri_ref, bra_ref, bri_ref, lam_ref, o_ref, a_s, b_s, acc_s):
    full = slice(None)
    n_steps = L_ALL // LRU_ROWS

    def gates(c, carry):
        r0 = pl.multiple_of(c * LRU_ROWS, LRU_ROWS)
        left_ok = jnp.where(c >= 2, 1.0, 0.0)
        right_ok = jnp.where((c == 0) | (c == n_steps - 1), 0.0, 1.0)
        u = _conv_rows(u_ref, r0, LRU_ROWS, full, cw_ref[...], left_ok, right_ok) + cb_ref[...]
        ub = u.astype(BF16)
        for d in range(2):
            r = jax.nn.sigmoid(_dot(ub, wra_ref[d, 0]) + bra_ref[d:d + 1])
            i = jax.nn.sigmoid(_dot(ub, wri_ref[d, 0]) + bri_ref[d:d + 1])
            log_a = -LRU_C * r * _softplus(-lam_ref[d:d + 1])
            a = jnp.exp(log_a)
            one_m_a2 = -jnp.tanh(log_a) * (a * a + 1.0)
            a_s[d, pl.ds(r0, LRU_ROWS), :] = a
            b_s[d, pl.ds(r0, LRU_ROWS), :] = jnp.sqrt(one_m_a2) * (i * u)
        return carry

    lax.fori_loop(0, n_steps, gates, 0)

    acc_s[...] = jnp.zeros_like(acc_s)
    row = lax.broadcasted_iota(jnp.int32, (SUB, LRU_BLOCK), 0)

    def scan(g, carry):
        hf, hb = carry
        gb = jnp.where(g < CTX_GROUPS, CTX_GROUPS - 1 - g, N_GROUPS + CTX_GROUPS - 1 - g)
        r0 = pl.multiple_of(g * SUB, SUB)
        a, b = a_s[0, pl.ds(r0, SUB), :], b_s[0, pl.ds(r0, SUB), :]
        for s in (1, 2, 4):
            ok = row >= s
            b = jnp.where(ok, a * pltpu.roll(b, s, 0) + b, b)
            a = jnp.where(ok, a * pltpu.roll(a, s, 0), a)
        h = b + a * hf
        acc_s[pl.ds(r0, SUB), :] += h
        hf = h[SUB - 1:SUB]
        r0 = pl.multiple_of(gb * SUB, SUB)
        a, b = a_s[1, pl.ds(r0, SUB), :], b_s[1, pl.ds(r0, SUB), :]
        for s in (1, 2, 4):
            ok = row < SUB - s
            b = jnp.where(ok, a * pltpu.roll(b, SUB - s, 0) + b, b)
            a = jnp.where(ok, a * pltpu.roll(a, SUB - s, 0), a)
        h = b + a * hb
        acc_s[pl.ds(r0, SUB), :] += h
        hb = h[0:1]
        return hf, hb

    zero = jnp.zeros((1, LRU_BLOCK), F32)
    lax.fori_loop(0, N_GROUPS, scan, (zero, zero), unroll=4)

    def emit(t, carry):
        r0 = pl.multiple_of(t * LRU_ROWS, LRU_ROWS)
        o_ref[0, pl.ds(r0, LRU_ROWS), :] = acc_s[pl.ds(r0, LRU_ROWS), :].astype(o_ref.dtype)
        return carry

    lax.fori_loop(0, n_steps, emit, 0)


def _lru_mixer(p, conv_w, conv_b, w_ra, w_ri, b_ra, b_ri, lam):
    nb = LRU_WIDTH // LRU_BLOCK
    bsz = p.shape[0]
    vec = lambda rows: pl.BlockSpec((rows, LRU_BLOCK), lambda b, n: (0, n))
    wspec = pl.BlockSpec((2, 1, LRU_BLOCK, LRU_BLOCK), lambda b, n: (0, n, 0, 0))
    return pl.pallas_call(
        _lru_body,
        grid=(bsz, LRU_BLOCKS),
        in_specs=[
            pl.BlockSpec((1, L_ALL, LRU_BLOCK), lambda b, n: (b, 0, nb + n)),
            vec(4), vec(1), wspec, wspec, vec(2), vec(2), vec(2),
        ],
        out_specs=pl.BlockSpec((1, L_ALL, LRU_BLOCK), lambda b, n: (b, 0, n)),
        out_shape=jax.ShapeDtypeStruct((bsz, L_ALL, LRU_WIDTH), ACT),
        scratch_shapes=[pltpu.VMEM((2, L_ALL, LRU_BLOCK), F32), pltpu.VMEM((2, L_ALL, LRU_BLOCK), F32),
                        pltpu.VMEM((L_ALL, LRU_BLOCK), F32)],
        compiler_params=_cparams(("parallel", "parallel")),
    )(p, conv_w, conv_b.reshape(1, LRU_WIDTH), w_ra.astype(BF16), w_ri.astype(BF16), b_ra, b_ri, lam)


def _att_body(sink_ref, q_ref, kc_ref, vc_ref, k0_ref, k1_ref, k2_ref, v0_ref, v1_ref, v2_ref, o_ref):
    i = pl.program_id(1)
    n_ctx_blocks = CTX_LEN // ATT_BLOCK
    li = i - n_ctx_blocks
    qpos = li * ATT_BLOCK + lax.broadcasted_iota(jnp.int32, (ATT_BLOCK, 3 * ATT_BLOCK), 0)
    kpos = (li - 1) * ATT_BLOCK + lax.broadcasted_iota(jnp.int32, (ATT_BLOCK, 3 * ATT_BLOCK), 1)
    band = (jnp.abs(qpos - kpos) <= ATT_WINDOW) & (kpos >= 0) & (kpos < SEQ) & (li >= 0)
    bias = jnp.concatenate([jnp.zeros((ATT_BLOCK, CTX_LEN), F32), jnp.where(band, 0.0, NEG)], axis=1)
    bias4 = jnp.concatenate([bias] * ATT_GROUP, axis=0)
    lane = lax.broadcasted_iota(jnp.int32, (ATT_BLOCK, ATT_QW), 1)
    qmask = [(lane % 128) // 32 == g for g in range(ATT_GROUP)]
    omask = [lane // ATT_HD == g for g in range(ATT_GROUP)]
    rowg = lax.broadcasted_iota(jnp.int32, (ATT_GROUP * ATT_BLOCK, 1), 0) // ATT_BLOCK
    for h in range(ATT_KV_HEADS):
        sl = slice(h * ATT_QW, (h + 1) * ATT_QW)
        qh = q_ref[0, :, sl]
        qm = jnp.concatenate([jnp.where(qmask[g], qh, 0.0) for g in range(ATT_GROUP)], axis=0).astype(BF16)
        keys = jnp.concatenate([kc_ref[0, :, sl], k0_ref[0, :, sl], k1_ref[0, :, sl], k2_ref[0, :, sl]],
                               axis=0).astype(BF16)
        vals = jnp.concatenate([vc_ref[0, :, sl], v0_ref[0, :, sl], v1_ref[0, :, sl], v2_ref[0, :, sl]],
                               axis=0).astype(BF16)
        s = _dot_nt(qm, keys) + bias4
        sink = jnp.zeros((ATT_GROUP * ATT_BLOCK, 1), F32)
        for g in range(ATT_GROUP):
            sink = jnp.where(rowg == g, sink_ref[h * ATT_GROUP + g], sink)
        mx = jnp.maximum(jnp.max(s, axis=-1, keepdims=True), sink)
        e = jnp.exp(s - mx)
        den = jnp.sum(e, axis=-1, keepdims=True) + jnp.exp(sink - mx)
        r = _dot((e / den).astype(BF16), vals)
        out = jnp.zeros((ATT_BLOCK, ATT_QW), F32)
        for g in range(ATT_GROUP):
            out = jnp.where(omask[g], r[g * ATT_BLOCK:(g + 1) * ATT_BLOCK], out)
        o_ref[0, :, sl] = out.astype(o_ref.dtype)


def _att_mixer(p, sinks):
    n_ctx_blocks = CTX_LEN // ATT_BLOCK
    n_lat_blocks = SEQ // ATT_BLOCK
    bsz = p.shape[0]

    def win(off, col):
        def index(b, i, sink_ref):
            blk = jnp.clip(i - n_ctx_blocks + off, 0, n_lat_blocks - 1)
            return (b, n_ctx_blocks + blk, col)
        return pl.BlockSpec((1, ATT_BLOCK, D_MODEL), index)

    grid_spec = pltpu.PrefetchScalarGridSpec(
        num_scalar_prefetch=1,
        grid=(bsz, L_ALL // ATT_BLOCK),
        in_specs=[
            pl.BlockSpec((1, ATT_BLOCK, D_MODEL), lambda b, i, s: (b, i, 0)),
            pl.BlockSpec((1, CTX_LEN, D_MODEL), lambda b, i, s: (b, 0, 2)),
            pl.BlockSpec((1, CTX_LEN, D_MODEL), lambda b, i, s: (b, 0, 3)),
            win(-1, 2), win(0, 2), win(1, 2), win(-1, 3), win(0, 3), win(1, 3),
        ],
        out_specs=pl.BlockSpec((1, ATT_BLOCK, D_MODEL), lambda b, i, s: (b, i, 0)),
    )
    return pl.pallas_call(
        _att_body,
        grid_spec=grid_spec,
        out_shape=jax.ShapeDtypeStruct((bsz, L_ALL, D_MODEL), ACT),
        compiler_params=_cparams(("parallel", "arbitrary")),
    )(sinks.astype(F32), p, p, p, p, p, p, p, p, p)


def _att_weight_layout(w_in):
    half = ATT_HD // 2
    q_cols, k_cols, v_cols = [], [], []
    for h in range(ATT_KV_HEADS):
        for part in range(2):
            for g in range(ATT_GROUP):
                base = (h * ATT_GROUP + g) * ATT_HD + part * half
                q_cols += list(range(base, base + half))
                kb = D_MODEL + h * ATT_HD + part * half
                k_cols += list(range(kb, kb + half))
        vb = D_MODEL + ATT_KV_HEADS * ATT_HD + h * ATT_HD
        v_cols += list(range(vb, vb + ATT_HD)) * ATT_GROUP
    z0 = D_MODEL + 2 * ATT_KV_HEADS * ATT_HD
    z_cols = list(range(z0, z0 + D_MODEL))
    order = np.asarray(q_cols + z_cols + k_cols + v_cols, np.int32)
    scale = np.ones((ATT_COLS,), np.float32)
    scale[:D_MODEL] = ATT_HD ** -0.5
    return (w_in[:, order] * scale).astype(BF16)


def _rope_tables():
    rows = SEQ // GRID_W
    row = jnp.repeat(jnp.arange(rows), GRID_W)
    col = jnp.tile(jnp.arange(GRID_W), rows)
    n_freq = ATT_HD // 4
    inv = 10000.0 ** (-jnp.arange(n_freq, dtype=F32) / n_freq)
    ang = jnp.concatenate([row[:, None] * inv, col[:, None] * inv], axis=-1)
    cos = jnp.concatenate([jnp.ones((CTX_LEN, ATT_HD // 2), F32), jnp.cos(ang)], axis=0)
    sin = jnp.concatenate([jnp.zeros((CTX_LEN, ATT_HD // 2), F32), jnp.sin(ang)], axis=0)
    return jnp.tile(cos, (1, ATT_GROUP)), jnp.tile(sin, (1, ATT_GROUP))


def kernel(x, c, ctx, c_ctx, ada_w, ada_b, norm_pre, norm_post, gdn_w_in, gdn_conv_w, gdn_a_log, gdn_dt_bias, gdn_g_norm, gdn_w_out, lru_w_in, lru_conv_w, lru_conv_b, lru_w_ra, lru_b_ra, lru_w_ri, lru_b_ri, lru_lam, lru_w_out, att_w_in, att_sinks, att_w_out):
    cc = jnp.zeros((MOD_ROWS, D_MODEL), F32).at[:c.shape[0]].set(c).at[CTX_MOD_ROW].set(c_ctx)
    mod_all = _modulation(cc, ada_w, ada_b).reshape(DEPTH, MOD_ROWS, 1, 3 * D_MODEL)
    xs = jnp.concatenate([ctx, x], axis=1)
    cos, sin = _rope_tables()
    for i in range(DEPTH):
        kind, j = i % 3, i // 3
        mod = mod_all[i]
        last = i == DEPTH - 1
        if kind == 0:
            p_main, p_ab = _in_proj(xs, mod, norm_pre[i], gdn_w_in[j].astype(BF16),
                                    [(0, GDN_MAIN), (GDN_MAIN, GDN_AB)], [ACT, F32])
            o = _gdn_mixer(p_main, p_ab, gdn_conv_w[j], gdn_a_log[j], gdn_dt_bias[j], gdn_g_norm[j])
            xs = _out_proj(o, p_main, 2, xs, mod, norm_post[i], gdn_w_out[j].astype(BF16), latent_only=last)
        elif kind == 1:
            (p,) = _in_proj(xs, mod, norm_pre[i], lru_w_in[j].astype(BF16), [(0, 2 * LRU_WIDTH)], [ACT])
            o = _lru_mixer(p, lru_conv_w[j], lru_conv_b[j], lru_w_ra[j], lru_w_ri[j],
                           lru_b_ra[j], lru_b_ri[j], lru_lam[j])
            xs = _out_proj(o, p, 0, xs, mod, norm_post[i], lru_w_out[j].astype(BF16), latent_only=last)
        else:
            (p,) = _in_proj(xs, mod, norm_pre[i], _att_weight_layout(att_w_in[j]), [(0, ATT_COLS)], [ACT],
                            rope=(cos, sin, (0, 1, 4, 5)))
            o = _att_mixer(p, att_sinks[j])
            xs = _out_proj(o, p, 1, xs, mod, norm_post[i], att_w_out[j].astype(BF16), latent_only=last)
    return xs
```

```python
import functools

import numpy as np
import jax
import jax.numpy as jnp
from jax import lax
from jax.experimental import pallas as pl
from jax.experimental.pallas import tpu as pltpu

F32 = jnp.float32
BF16 = jnp.bfloat16
ACT = jnp.bfloat16

D_MODEL = 1024
BATCH = 8
SEQ = 2048
CTX_LEN = 256
L_ALL = CTX_LEN + SEQ
DEPTH = 4
GRID_W = 64
EPS = 1e-6

ROW_TILE = 256
N_ROW_TILES = L_ALL // ROW_TILE
MOD_ROWS = 16
CTX_MOD_ROW = BATCH
VMEM_LIMIT = 56 * 1024 * 1024

GDN_HD = 128
GDN_QK_HEADS = 8
GDN_V_HEADS = 16
GDN_KEY_DIM = 1024
GDN_VAL_DIM = 2048
GDN_MAIN = 2 * GDN_KEY_DIM + 2 * GDN_VAL_DIM
GDN_AB = 64
CHUNK = 64
N_CHUNKS = L_ALL // CHUNK
CTX_CHUNKS = CTX_LEN // CHUNK
PAIR = 2 * CHUNK
A_GROUP = 9
INV_LEAF = 8

LRU_WIDTH = 1024
LRU_BLOCK = 128
LRU_BLOCKS = 8
LRU_C = 8.0
LRU_ROWS = 256
SUB = 8
N_GROUPS = L_ALL // SUB
CTX_GROUPS = CTX_LEN // SUB

ATT_HD = 64
ATT_KV_HEADS = 4
ATT_GROUP = 4
ATT_BLOCK = 128
ATT_WINDOW = 128
ATT_QW = ATT_GROUP * ATT_HD
ATT_COLS = 4 * D_MODEL
NEG = -1e30


def _silu(x):
    return x * jax.nn.sigmoid(x)


def _softplus(x):
    return jnp.maximum(x, 0.0) + jnp.log1p(jnp.exp(-jnp.abs(x)))


def _cparams(sem):
    return pltpu.CompilerParams(dimension_semantics=sem, vmem_limit_bytes=VMEM_LIMIT)


def _dot(a, b):
    return jnp.dot(a, b, preferred_element_type=F32)


def _dot_nt(a, b):
    return lax.dot_general(a, b, (((1,), (1,)), ((), ())), preferred_element_type=F32)


def _dot_tn(a, b):
    return lax.dot_general(a, b, (((0,), (0,)), ((), ())), preferred_element_type=F32)


def _mod_body(c_ref, w_ref, b_ref, o_ref):
    h = _silu(c_ref[...]).astype(BF16)
    o_ref[0] = _dot(h, w_ref[0].astype(BF16)) + b_ref[0]


def _modulation(cc, ada_w, ada_b):
    n_col = 3
    return pl.pallas_call(
        _mod_body,
        grid=(DEPTH, n_col),
        in_specs=[
            pl.BlockSpec((MOD_ROWS, D_MODEL), lambda i, n: (0, 0)),
            pl.BlockSpec((1, D_MODEL, D_MODEL), lambda i, n: (i, 0, n)),
            pl.BlockSpec((1, 1, D_MODEL), lambda i, n: (i, 0, n)),
        ],
        out_specs=pl.BlockSpec((1, MOD_ROWS, D_MODEL), lambda i, n: (i, 0, n)),
        out_shape=jax.ShapeDtypeStruct((DEPTH, MOD_ROWS, 3 * D_MODEL), F32),
        compiler_params=_cparams(("parallel", "parallel")),
    )(cc, ada_w, ada_b.reshape(DEPTH, 1, 3 * D_MODEL))


def _mod_index(b, t):
    return (jnp.where(t == 0, CTX_MOD_ROW, b), 0, 0)


def _in_proj_body(x_ref, mod_ref, nw_ref, w_ref, *refs, segs, rope_groups):
    if rope_groups:
        cos_ref, sin_ref = refs[:2]
        refs = refs[2:]
    x = x_ref[0]
    y = x * lax.rsqrt(jnp.mean(x * x, axis=-1, keepdims=True) + EPS) * nw_ref[...]
    shift = mod_ref[0, :, 0:D_MODEL]
    scale = mod_ref[0, :, D_MODEL:2 * D_MODEL]
    h = (y * (1.0 + scale) + shift).astype(BF16)
    for ref, (c0, width) in zip(refs, segs):
        step = min(width, 512)
        for n0 in range(0, width, step):
            acc = _dot(h, w_ref[:, c0 + n0:c0 + n0 + step])
            if (c0 + n0) // 512 in rope_groups:
                cos = cos_ref[...]
                sin = sin_ref[...]
                parts = []
                for g0 in range(0, step, 256):
                    x1 = acc[:, g0:g0 + 128]
                    x2 = acc[:, g0 + 128:g0 + 256]
                    parts += [x1 * cos - x2 * sin, x2 * cos + x1 * sin]
                acc = jnp.concatenate(parts, axis=1)
            ref[0, :, n0:n0 + step] = acc.astype(ref.dtype)


def _in_proj(x, mod, nw, w, segs, out_dtypes, rope=None):
    n_total = w.shape[1]
    in_specs = [
        pl.BlockSpec((1, ROW_TILE, D_MODEL), lambda b, t: (b, t, 0)),
        pl.BlockSpec((1, 1, 3 * D_MODEL), _mod_index),
        pl.BlockSpec((1, D_MODEL), lambda b, t: (0, 0)),
        pl.BlockSpec((D_MODEL, n_total), lambda b, t: (0, 0), pipeline_mode=pl.Buffered(1)),
    ]
    args = [x, mod, nw.reshape(1, D_MODEL), w]
    rope_groups = ()
    if rope is not None:
        cos, sin, rope_groups = rope
        in_specs += [pl.BlockSpec((ROW_TILE, 128), lambda b, t: (t, 0))] * 2
        args += [cos, sin]
    out_specs = [pl.BlockSpec((1, ROW_TILE, width), lambda b, t: (b, t, 0)) for _, width in segs]
    bsz = x.shape[0]
    out_shape = [jax.ShapeDtypeStruct((bsz, L_ALL, width), dt) for (_, width), dt in zip(segs, out_dtypes)]
    return pl.pallas_call(
        functools.partial(_in_proj_body, segs=tuple(segs), rope_groups=tuple(rope_groups)),
        grid=(bsz, N_ROW_TILES),
        in_specs=in_specs,
        out_specs=out_specs,
        out_shape=out_shape,
        compiler_params=_cparams(("parallel", "parallel")),
    )(*args)


def _out_proj_body(o_ref, z_ref, x_ref, mod_ref, nw_ref, w_ref, out_ref):
    g = (o_ref[0].astype(F32) * _silu(z_ref[0].astype(F32))).astype(BF16)
    y = _dot(g, w_ref[...])
    yn = y * lax.rsqrt(jnp.mean(y * y, axis=-1, keepdims=True) + EPS) * nw_ref[...]
    gate = mod_ref[0, :, 2 * D_MODEL:3 * D_MODEL]
    out_ref[0] = x_ref[0] + gate * yn


def _out_proj(o, z_arr, z_block, x, mod, nw, w, latent_only=False):
    width = o.shape[-1]
    bsz = o.shape[0]
    skip = CTX_LEN // ROW_TILE if latent_only else 0
    return pl.pallas_call(
        _out_proj_body,
        grid=(bsz, N_ROW_TILES - skip),
        in_specs=[
            pl.BlockSpec((1, ROW_TILE, width), lambda b, t: (b, t + skip, 0)),
            pl.BlockSpec((1, ROW_TILE, width), lambda b, t: (b, t + skip, z_block)),
            pl.BlockSpec((1, ROW_TILE, D_MODEL), lambda b, t: (b, t + skip, 0)),
            pl.BlockSpec((1, 1, 3 * D_MODEL), lambda b, t: _mod_index(b, t + skip)),
            pl.BlockSpec((1, D_MODEL), lambda b, t: (0, 0)),
            pl.BlockSpec((width, D_MODEL), lambda b, t: (0, 0), pipeline_mode=pl.Buffered(1)),
        ],
        out_specs=pl.BlockSpec((1, ROW_TILE, D_MODEL), lambda b, t: (b, t, 0)),
        out_shape=jax.ShapeDtypeStruct((bsz, L_ALL - skip * ROW_TILE, D_MODEL), F32),
        compiler_params=_cparams(("parallel", "parallel")),
    )(o, z_arr, x, mod, nw.reshape(1, D_MODEL), w)


def _conv_rows(ref, r0, rows, width_sl, cw, left_ok, right_ok):
    total = ref.shape[1]
    halo = 2 * SUB
    main = ref[0, pl.ds(r0, rows), width_sl].astype(F32)
    prev = ref[0, pl.ds(pl.multiple_of(jnp.maximum(r0 - halo, 0), halo), halo), width_sl].astype(F32) * left_ok
    nxt = ref[0, pl.ds(pl.multiple_of(jnp.minimum(r0 + rows, total - halo), halo), halo),
              width_sl].astype(F32) * right_ok
    xw = jnp.concatenate([prev, main, nxt], axis=0)
    return (cw[0:1] * xw[halo - 2:halo - 2 + rows] + cw[1:2] * xw[halo - 1:halo - 1 + rows]
            + cw[2:3] * xw[halo:halo + rows] + cw[3:4] * xw[halo + 1:halo + 1 + rows])


def _gdn_consts():
    i = np.arange(PAIR)
    same = (i[:, None] // CHUNK) == (i[None, :] // CHUNK)
    t_i, t_m = i[:, None] % CHUNK, i[None, :] % CHUNK
    low = same & (t_m <= t_i)
    upp = same & (t_m >= t_i)
    both = lambda f, b: np.concatenate([f, b], axis=1)
    eye = np.eye(PAIR, dtype=np.float32)
    mats = np.stack([
        both(np.where(low, 0.0, NEG), np.where(upp, 0.0, NEG)),
        both(same & (t_m < t_i), same & (t_m > t_i)).astype(np.float32),
        both(eye, eye)]).astype(np.float32)
    levels = [(i[:, None] // INV_LEAF) == (i[None, :] // INV_LEAF)]
    size = INV_LEAF
    while size < CHUNK:
        levels.append(((i[:, None] // (2 * size)) == (i[None, :] // (2 * size)))
                      & ((i[:, None] // size) != (i[None, :] // size)))
        size *= 2
    lvl = np.stack([both(m, m) for m in levels]).astype(np.float32)
    return jnp.asarray(mats, F32), jnp.asarray(lvl, BF16)


def _block_diag(x):
    zero = jnp.zeros((PAIR, PAIR), x.dtype)
    return jnp.concatenate([jnp.concatenate([x[:, :PAIR], zero], axis=1),
                            jnp.concatenate([zero, x[:, PAIR:]], axis=1)], axis=0)


def _gdn_body(q_ref, k_ref, v_ref, cwq_ref, cwk_ref, cwv_ref, abr_ref, pr_ref,
              mats_ref, lvl_ref, gn_ref, o_ref,
              pad_s, lhs_s, c_s, o0_s, dec_s, st_s, acc_s, *stage_refs):
    pad_s[:, 0:SUB, :] = jnp.zeros((4, SUB, GDN_HD), F32)
    pad_s[:, SUB + CTX_LEN:2 * SUB + CTX_LEN, :] = jnp.zeros((4, SUB, GDN_HD), F32)
    pad_s[:, 2 * SUB + L_ALL:3 * SUB + L_ALL, :] = jnp.zeros((4, SUB, GDN_HD), F32)

    def fill(t, carry):
        src = pl.multiple_of(t * ROW_TILE, ROW_TILE)
        dst = pl.multiple_of(src + jnp.where(t == 0, SUB, 2 * SUB), SUB)
        pad_s[0, pl.ds(dst, ROW_TILE), :] = q_ref[0, pl.ds(src, ROW_TILE), :].astype(F32)
        pad_s[1, pl.ds(dst, ROW_TILE), :] = k_ref[0, pl.ds(src, ROW_TILE), :].astype(F32)
        pad_s[2, pl.ds(dst, ROW_TILE), :] = v_ref[0, pl.ds(src, ROW_TILE), 0:GDN_HD].astype(F32)
        pad_s[3, pl.ds(dst, ROW_TILE), :] = v_ref[0, pl.ds(src, ROW_TILE), GDN_HD:].astype(F32)
        return carry

    lax.fori_loop(0, N_ROW_TILES, fill, 0)
    stage_a, stage_b = stage_refs[:5], stage_refs[5:]
    t_lane = lax.broadcasted_iota(jnp.int32, (A_GROUP * SUB, PAIR), 1)
    t_in_chunk = t_lane % CHUNK

    def conv(which, base, cw):
        taps = [pad_s[which, pl.ds(base + (k - 2), CHUNK), :] for k in range(4)]
        return _silu(cw[0:1] * taps[0] + cw[1:2] * taps[1] + cw[2:3] * taps[2] + cw[3:4] * taps[3])

    def lanes2(f, b):
        return jnp.concatenate([jnp.broadcast_to(f, (PAIR, PAIR)), jnp.broadcast_to(b, (PAIR, PAIR))], axis=1)

    def group_gates(first):
        zero = jnp.zeros((SUB - 4, PAIR), F32)
        xr = jnp.concatenate([x for g in range(A_GROUP) for x in (abr_ref[0, 0, first + g], zero)], axis=0)
        g_r = -jnp.exp(jnp.tile(pr_ref[0, 0], (A_GROUP, 1))) * _softplus(xr + jnp.tile(pr_ref[0, 1], (A_GROUP, 1)))
        beta_r = jax.nn.sigmoid(xr)
        cum_f, cum_b = g_r, g_r
        step = 1
        while step < CHUNK:
            cum_f = cum_f + jnp.where(t_in_chunk >= step, pltpu.roll(cum_f, step, 1), 0.0)
            cum_b = cum_b + jnp.where(t_in_chunk < CHUNK - step, pltpu.roll(cum_b, PAIR - step, 1), 0.0)
            step *= 2
        tot = cum_f + cum_b - g_r
        other = pltpu.roll(tot, CHUNK, 1)
        tot_h0 = jnp.where(t_lane < CHUNK, tot, other)
        tot_h1 = jnp.where(t_lane < CHUNK, other, tot)
        pad = jnp.zeros((PAIR - A_GROUP * SUB, PAIR), F32)
        col = lambda x: jnp.concatenate([x, pad], axis=0).T
        return dict(cum_f=cum_f, cum_b=cum_b, cum_f_t=col(cum_f), cum_b_t=col(cum_b), tot_t=col(tot),
                    beta_t=col(beta_r), dec=jnp.exp(jnp.concatenate([tot_h0, tot_h1], axis=1)))

    def chunk_start(c, g, gt, stage):
        base = c * CHUNK + (SUB if c < CTX_CHUNKS else 2 * SUB)
        cwv = cwv_ref[...]
        qc = conv(0, base, cwq_ref[...])
        kc = conv(1, base, cwk_ref[...])
        v2 = jnp.concatenate([conv(2, base, cwv[:, :GDN_HD]), conv(3, base, cwv[:, GDN_HD:])], axis=0)
        qn = qc * (lax.rsqrt(jnp.sum(qc * qc, axis=-1, keepdims=True) + 1e-6) * (GDN_HD ** -0.5))
        kn = kc * lax.rsqrt(jnp.sum(kc * kc, axis=-1, keepdims=True) + 1e-6)
        q2 = jnp.concatenate([qn, qn], axis=0)
        k2 = jnp.concatenate([kn, kn], axis=0)
        gram = _dot_nt(jnp.concatenate([kn, qn], axis=0).astype(BF16), k2.astype(BF16))
        kk = jnp.concatenate([gram[:CHUNK], gram[:CHUNK]], axis=0)
        qk = jnp.concatenate([gram[CHUNK:], gram[CHUNK:]], axis=0)
        r = g * SUB
        gc_f, gc_b = gt["cum_f"][r:r + 1], gt["cum_b"][r + 2:r + 3]
        gcc_d = [gt["cum_f_t"][:, r:r + 1], gt["cum_b_t"][:, r + 2:r + 3]]
        totc_d = [gt["tot_t"][:, r:r + 1], gt["tot_t"][:, r + 2:r + 3]]
        bc_d = [gt["beta_t"][:, r + 1:r + 2], gt["beta_t"][:, r + 3:r + 4]]
        m = jnp.exp(lanes2(gcc_d[0], gcc_d[1]) - jnp.concatenate([gc_f, gc_b], axis=1) + mats_ref[0])
        a = (jnp.concatenate([kk, kk], axis=1) * m * mats_ref[1]
             * lanes2(bc_d[0], bc_d[1])).astype(BF16)
        qkm = (jnp.concatenate([qk, qk], axis=1) * m).astype(BF16)
        sa, srhs, sqd, skd, sqkm = stage
        sa[g] = a
        for d in range(2):
            gcc, totc, bc = gcc_d[d], totc_d[d], bc_d[d]
            egc = jnp.exp(gcc)
            dec_s[d * N_CHUNKS + c] = jnp.broadcast_to(gt["dec"][r + 2 * d:r + 2 * d + 1], (SUB, 2 * GDN_HD))
            srhs[2 * g + d] = jnp.concatenate([v2 * bc, k2 * (bc * egc)], axis=1).astype(BF16)
            sqd[2 * g + d] = q2 * egc
            skd[2 * g + d] = (k2 * jnp.exp(totc - gcc)).astype(BF16)
            sqkm[2 * g + d] = qkm[:, d * PAIR:(d + 1) * PAIR]

    def prepare(it, stage):
        gt = group_gates(it * A_GROUP)
        for g in range(A_GROUP):
            chunk_start(it * A_GROUP + g, g, gt, stage)

    def mm(p, q):
        return _dot(p.astype(BF16), _block_diag(q.astype(BF16)))

    def solve(it, stage):
        sa, srhs, sqd, skd, sqkm = stage
        chunks = [dict(a=sa[g], dirs=[dict(idx=d * N_CHUNKS + it * A_GROUP + g, rhs=srhs[2 * g + d],
                                           qd=sqd[2 * g + d], kd=skd[2 * g + d], qkm=sqkm[2 * g + d])
                                      for d in range(2)]) for g in range(A_GROUP)]
        n_levels = lvl_ref.shape[0]
        for ch in chunks:
            leaf = ch["a"] * lvl_ref[0]
            ch["x"] = mats_ref[2] - leaf.astype(F32)
            ch["p"] = mm(leaf, leaf)
        for ch in chunks:
            xp = mm(jnp.concatenate([ch["x"], ch["p"]], axis=0), ch["p"])
            ch["x"] = ch["x"] + xp[:PAIR]
            ch["p"] = xp[PAIR:]
        for ch in chunks:
            ch["x"] = ch["x"] + mm(ch["x"], ch["p"])
        for lv in range(1, n_levels):
            for ch in chunks:
                ch["y"] = mm(ch["a"] * lvl_ref[lv], ch["x"])
            for ch in chunks:
                ch["x"] = ch["x"] - mm(ch["x"], ch["y"])
        chains = []
        for ch in chunks:
            xb = ch["x"].astype(BF16)
            for d, cd in enumerate(ch["dirs"]):
                cd["uw"] = _dot(xb[:, d * PAIR:(d + 1) * PAIR], cd["rhs"]).astype(BF16)
                chains.append(cd)
        for cd in chains:
            wu = jnp.concatenate([cd["uw"][:, GDN_HD:], cd["uw"][:, :GDN_HD]], axis=1)
            cd["nc"] = [_dot_tn(cd["kd"][h * CHUNK:(h + 1) * CHUNK], wu[h * CHUNK:(h + 1) * CHUNK])
                        for h in range(2)]
            cd["qo"] = _dot(cd["qkm"], wu)
        for cd in chains:
            nc, qo, idx = cd["nc"], cd["qo"], cd["idx"]
            lhs_s[idx] = jnp.concatenate([-nc[0][:, :GDN_HD], -nc[1][:, :GDN_HD],
                                          cd["qd"] - qo[:, :GDN_HD]], axis=0).astype(BF16)
            c_s[idx] = jnp.concatenate([nc[0][:, GDN_HD:], nc[1][:, GDN_HD:]], axis=1)
            o0_s[idx] = qo[:, GDN_HD:].astype(o0_s.dtype)

    stages = (stage_a, stage_b)
    n_groups = N_CHUNKS // A_GROUP
    prepare(0, stages[0])
    for it in range(n_groups):
        if it + 1 < n_groups:
            prepare(it + 1, stages[(it + 1) % 2])
        solve(it, stages[it % 2])

    st_s[...] = jnp.zeros_like(st_s)
    acc_s[...] = jnp.zeros_like(acc_s)

    def phase_b(s, carry):
        chunk_b = jnp.where(s < CTX_CHUNKS, CTX_CHUNKS - 1 - s, N_CHUNKS + CTX_CHUNKS - 1 - s)
        chains = [dict(d=0, c=s, idx=s), dict(d=1, c=chunk_b, idx=N_CHUNKS + chunk_b)]
        for ch in chains:
            ch["st"] = st_s[ch["d"]]
            ch["r"] = _dot(lhs_s[ch["idx"]], ch["st"].astype(BF16))
        for ch in chains:
            r, idx = ch["r"], ch["idx"]
            ns = jnp.concatenate([r[0:GDN_HD, :GDN_HD], r[GDN_HD:2 * GDN_HD, GDN_HD:]], axis=1)
            st_s[ch["d"]] = ch["st"] * dec_s[idx][0:1] + ns + c_s[idx]
            o0 = o0_s[idx]
            r0 = pl.multiple_of(ch["c"] * CHUNK, CHUNK)
            acc_s[pl.ds(r0, CHUNK), 0:GDN_HD] += r[2 * GDN_HD:2 * GDN_HD + CHUNK, :GDN_HD] + o0[:CHUNK]
            acc_s[pl.ds(r0, CHUNK), GDN_HD:] += r[2 * GDN_HD + CHUNK:, GDN_HD:] + o0[CHUNK:]
        return carry

    lax.fori_loop(0, N_CHUNKS, phase_b, 0)

    def finish(t, carry):
        r0 = pl.multiple_of(t * ROW_TILE, ROW_TILE)
        for h in range(2):
            sl = slice(h * GDN_HD, (h + 1) * GDN_HD)
            x = acc_s[pl.ds(r0, ROW_TILE), sl]
            o_ref[0, pl.ds(r0, ROW_TILE), sl] = (
                x * lax.rsqrt(jnp.mean(x * x, axis=-1, keepdims=True) + EPS) * gn_ref[...]).astype(o_ref.dtype)
        return carry

    lax.fori_loop(0, N_ROW_TILES, finish, 0)


def _gdn_mixer(p_main, p_ab, conv_w, a_log, dt_bias, g_norm):
    mats, lvl = _gdn_consts()
    bsz = p_main.shape[0]
    ab = p_ab.reshape(bsz, N_CHUNKS, CHUNK, 2, 2, GDN_QK_HEADS, 2)
    abr = ab.transpose(0, 5, 1, 4, 3, 6, 2).reshape(bsz, GDN_QK_HEADS, N_CHUNKS, 4, PAIR)
    prm = jnp.stack([a_log, dt_bias], axis=0).astype(F32).reshape(2, 2, GDN_QK_HEADS, 2)
    zero = jnp.zeros_like(prm[:, 0])
    rows = jnp.stack([prm[:, 0], prm[:, 0], prm[:, 1], prm[:, 1]] + [zero] * (SUB - 4), axis=1)
    pr = jnp.repeat(rows.transpose(2, 0, 1, 3), CHUNK, axis=3)

    grid = (bsz, GDN_QK_HEADS)
    kq = GDN_KEY_DIM // GDN_HD
    in_specs = [
        pl.BlockSpec((1, L_ALL, GDN_HD), lambda b, j: (b, 0, j)),
        pl.BlockSpec((1, L_ALL, GDN_HD), lambda b, j: (b, 0, kq + j)),
        pl.BlockSpec((1, L_ALL, 2 * GDN_HD), lambda b, j: (b, 0, kq + j)),
        pl.BlockSpec((4, GDN_HD), lambda b, j: (0, j)),
        pl.BlockSpec((4, GDN_HD), lambda b, j: (0, kq + j)),
        pl.BlockSpec((4, 2 * GDN_HD), lambda b, j: (0, kq + j)),
        pl.BlockSpec((1, 1, N_CHUNKS, 4, PAIR), lambda b, j: (b, j, 0, 0, 0)),
        pl.BlockSpec((1, 2, SUB, PAIR), lambda b, j: (j, 0, 0, 0)),
        pl.BlockSpec(mats.shape, lambda b, j: (0, 0, 0)),
        pl.BlockSpec(lvl.shape, lambda b, j: (0, 0, 0)),
        pl.BlockSpec((1, GDN_HD), lambda b, j: (0, 0)),
    ]
    return pl.pallas_call(
        _gdn_body,
        grid=grid,
        in_specs=in_specs,
        out_specs=pl.BlockSpec((1, L_ALL, 2 * GDN_HD), lambda b, j: (b, 0, j)),
        out_shape=jax.ShapeDtypeStruct((bsz, L_ALL, GDN_VAL_DIM), ACT),
        scratch_shapes=[
            pltpu.VMEM((4, L_ALL + 3 * SUB, GDN_HD), F32),
            pltpu.VMEM((2 * N_CHUNKS, 3 * GDN_HD, GDN_HD), BF16),
            pltpu.VMEM((2 * N_CHUNKS, GDN_HD, 2 * GDN_HD), F32),
            pltpu.VMEM((2 * N_CHUNKS, PAIR, GDN_HD), BF16),
            pltpu.VMEM((2 * N_CHUNKS, SUB, 2 * GDN_HD), F32),
            pltpu.VMEM((2, GDN_HD, 2 * GDN_HD), F32),
            pltpu.VMEM((L_ALL, 2 * GDN_HD), F32),
        ] + 2 * [
            pltpu.VMEM((A_GROUP, PAIR, 2 * PAIR), BF16),
            pltpu.VMEM((2 * A_GROUP, PAIR, 2 * GDN_HD), BF16),
            pltpu.VMEM((2 * A_GROUP, PAIR, GDN_HD), F32),
            pltpu.VMEM((2 * A_GROUP, PAIR, GDN_HD), BF16),
            pltpu.VMEM((2 * A_GROUP, PAIR, PAIR), BF16),
        ],
        compiler_params=_cparams(("parallel", "parallel")),
    )(p_main, p_main, p_main, conv_w, conv_w, conv_w, abr, pr, mats, lvl,
      g_norm.reshape(1, GDN_HD))


def _lru_body(u_ref, cw_ref, cb_ref, wra_ref, wri_ref, bra_ref, bri_ref, lam_ref, o_ref, a_s, b_s, acc_s):
    full = slice(None)
    n_steps = L_ALL // LRU_ROWS

    def gates(c, carry):
        r0 = pl.multiple_of(c * LRU_ROWS, LRU_ROWS)
        left_ok = jnp.where(c >= 2, 1.0, 0.0)
        right_ok = jnp.where((c == 0) | (c == n_steps - 1), 0.0, 1.0)
        u = _conv_rows(u_ref, r0, LRU_ROWS, full, cw_ref[...], left_ok, right_ok) + cb_ref[...]
        ub = u.astype(BF16)
        for d in range(2):
            r = jax.nn.sigmoid(_dot(ub, wra_ref[d, 0]) + bra_ref[d:d + 1])
            i = jax.nn.sigmoid(_dot(ub, wri_ref[d, 0]) + bri_ref[d:d + 1])
            log_a = -LRU_C * r * _softplus(-lam_ref[d:d + 1])
            a = jnp.exp(log_a)
            one_m_a2 = -jnp.tanh(log_a) * (a * a + 1.0)
            a_s[d, pl.ds(r0, LRU_ROWS), :] = a
            b_s[d, pl.ds(r0, LRU_ROWS), :] = jnp.sqrt(one_m_a2) * (i * u)
        return carry

    lax.fori_loop(0, n_steps, gates, 0)

    acc_s[...] = jnp.zeros_like(acc_s)
    row = lax.broadcasted_iota(jnp.int32, (SUB, LRU_BLOCK), 0)

    def scan(g, carry):
        hf, hb = carry
        gb = jnp.where(g < CTX_GROUPS, CTX_GROUPS - 1 - g, N_GROUPS + CTX_GROUPS - 1 - g)
        r0 = pl.multiple_of(g * SUB, SUB)
        a, b = a_s[0, pl.ds(r0, SUB), :], b_s[0, pl.ds(r0, SUB), :]
        for s in (1, 2, 4):
            ok = row >= s
            b = jnp.where(ok, a * pltpu.roll(b, s, 0) + b, b)
            a = jnp.where(ok, a * pltpu.roll(a, s, 0), a)
        h = b + a * hf
        acc_s[pl.ds(r0, SUB), :] += h
        hf = h[SUB - 1:SUB]
        r0 = pl.multiple_of(gb * SUB, SUB)
        a, b = a_s[1, pl.ds(r0, SUB), :], b_s[1, pl.ds(r0, SUB), :]
        for s in (1, 2, 4):
            ok = row < SUB - s
            b = jnp.where(ok, a * pltpu.roll(b, SUB - s, 0) + b, b)
            a = jnp.where(ok, a * pltpu.roll(a, SUB - s, 0), a)
        h = b + a * hb
        acc_s[pl.ds(r0, SUB), :] += h
        hb = h[0:1]
        return hf, hb

    zero = jnp.zeros((1, LRU_BLOCK), F32)
    lax.fori_loop(0, N_GROUPS, scan, (zero, zero), unroll=4)

    def emit(t, carry):
        r0 = pl.multiple_of(t * LRU_ROWS, LRU_ROWS)
        o_ref[0, pl.ds(r0, LRU_ROWS), :] = acc_s[pl.ds(r0, LRU_ROWS), :].astype(o_ref.dtype)
        return carry

    lax.fori_loop(0, n_steps, emit, 0)


def _lru_mixer(p, conv_w, conv_b, w_ra, w_ri, b_ra, b_ri, lam):
    nb = LRU_WIDTH // LRU_BLOCK
    bsz = p.shape[0]
    vec = lambda rows: pl.BlockSpec((rows, LRU_BLOCK), lambda b, n: (0, n))
    wspec = pl.BlockSpec((2, 1, LRU_BLOCK, LRU_BLOCK), lambda b, n: (0, n, 0, 0))
    return pl.pallas_call(
        _lru_body,
        grid=(bsz, LRU_BLOCKS),
        in_specs=[
            pl.BlockSpec((1, L_ALL, LRU_BLOCK), lambda b, n: (b, 0, nb + n)),
            vec(4), vec(1), wspec, wspec, vec(2), vec(2), vec(2),
        ],
        out_specs=pl.BlockSpec((1, L_ALL, LRU_BLOCK), lambda b, n: (b, 0, n)),
        out_shape=jax.ShapeDtypeStruct((bsz, L_ALL, LRU_WIDTH), ACT),
        scratch_shapes=[pltpu.VMEM((2, L_ALL, LRU_BLOCK), F32), pltpu.VMEM((2, L_ALL, LRU_BLOCK), F32),
                        pltpu.VMEM((L_ALL, LRU_BLOCK), F32)],
        compiler_params=_cparams(("parallel", "parallel")),
    )(p, conv_w, conv_b.reshape(1, LRU_WIDTH), w_ra.astype(BF16), w_ri.astype(BF16), b_ra, b_ri, lam)


def _att_body(sink_ref, q_ref, kc_ref, vc_ref, k0_ref, k1_ref, k2_ref, v0_ref, v1_ref, v2_ref, o_ref):
    i = pl.program_id(1)
    n_ctx_blocks = CTX_LEN // ATT_BLOCK
    li = i - n_ctx_blocks
    qpos = li * ATT_BLOCK + lax.broadcasted_iota(jnp.int32, (ATT_BLOCK, 3 * ATT_BLOCK), 0)
    kpos = (li - 1) * ATT_BLOCK + lax.broadcasted_iota(jnp.int32, (ATT_BLOCK, 3 * ATT_BLOCK), 1)
    band = (jnp.abs(qpos - kpos) <= ATT_WINDOW) & (kpos >= 0) & (kpos < SEQ) & (li >= 0)
    bias = jnp.concatenate([jnp.zeros((ATT_BLOCK, CTX_LEN), F32), jnp.where(band, 0.0, NEG)], axis=1)
    bias4 = jnp.concatenate([bias] * ATT_GROUP, axis=0)
    lane = lax.broadcasted_iota(jnp.int32, (ATT_BLOCK, ATT_QW), 1)
    qmask = [(lane % 128) // 32 == g for g in range(ATT_GROUP)]
    omask = [lane // ATT_HD == g for g in range(ATT_GROUP)]
    rowg = lax.broadcasted_iota(jnp.int32, (ATT_GROUP * ATT_BLOCK, 1), 0) // ATT_BLOCK
    for h in range(ATT_KV_HEADS):
        sl = slice(h * ATT_QW, (h + 1) * ATT_QW)
        qh = q_ref[0, :, sl]
        qm = jnp.concatenate([jnp.where(qmask[g], qh, 0.0) for g in range(ATT_GROUP)], axis=0).astype(BF16)
        keys = jnp.concatenate([kc_ref[0, :, sl], k0_ref[0, :, sl], k1_ref[0, :, sl], k2_ref[0, :, sl]],
                               axis=0).astype(BF16)
        vals = jnp.concatenate([vc_ref[0, :, sl], v0_ref[0, :, sl], v1_ref[0, :, sl], v2_ref[0, :, sl]],
                               axis=0).astype(BF16)
        s = _dot_nt(qm, keys) + bias4
        sink = jnp.zeros((ATT_GROUP * ATT_BLOCK, 1), F32)
        for g in range(ATT_GROUP):
            sink = jnp.where(rowg == g, sink_ref[h * ATT_GROUP + g], sink)
        mx = jnp.maximum(jnp.max(s, axis=-1, keepdims=True), sink)
        e = jnp.exp(s - mx)
        den = jnp.sum(e, axis=-1, keepdims=True) + jnp.exp(sink - mx)
        r = _dot((e / den).astype(BF16), vals)
        out = jnp.zeros((ATT_BLOCK, ATT_QW), F32)
        for g in range(ATT_GROUP):
            out = jnp.where(omask[g], r[g * ATT_BLOCK:(g + 1) * ATT_BLOCK], out)
        o_ref[0, :, sl] = out.astype(o_ref.dtype)


def _att_mixer(p, sinks):
    n_ctx_blocks = CTX_LEN // ATT_BLOCK
    n_lat_blocks = SEQ // ATT_BLOCK
    bsz = p.shape[0]

    def win(off, col):
        def index(b, i, sink_ref):
            blk = jnp.clip(i - n_ctx_blocks + off, 0, n_lat_blocks - 1)
            return (b, n_ctx_blocks + blk, col)
        return pl.BlockSpec((1, ATT_BLOCK, D_MODEL), index)

    grid_spec = pltpu.PrefetchScalarGridSpec(
        num_scalar_prefetch=1,
        grid=(bsz, L_ALL // ATT_BLOCK),
        in_specs=[
            pl.BlockSpec((1, ATT_BLOCK, D_MODEL), lambda b, i, s: (b, i, 0)),
            pl.BlockSpec((1, CTX_LEN, D_MODEL), lambda b, i, s: (b, 0, 2)),
            pl.BlockSpec((1, CTX_LEN, D_MODEL), lambda b, i, s: (b, 0, 3)),
            win(-1, 2), win(0, 2), win(1, 2), win(-1, 3), win(0, 3), win(1, 3),
        ],
        out_specs=pl.BlockSpec((1, ATT_BLOCK, D_MODEL), lambda b, i, s: (b, i, 0)),
    )
    return pl.pallas_call(
        _att_body,
        grid_spec=grid_spec,
        out_shape=jax.ShapeDtypeStruct((bsz, L_ALL, D_MODEL), ACT),
        compiler_params=_cparams(("parallel", "arbitrary")),
    )(sinks.astype(F32), p, p, p, p, p, p, p, p, p)


def _att_weight_layout(w_in):
    half = ATT_HD // 2
    q_cols, k_cols, v_cols = [], [], []
    for h in range(ATT_KV_HEADS):
        for part in range(2):
            for g in range(ATT_GROUP):
                base = (h * ATT_GROUP + g) * ATT_HD + part * half
                q_cols += list(range(base, base + half))
                kb = D_MODEL + h * ATT_HD + part * half
                k_cols += list(range(kb, kb + half))
        vb = D_MODEL + ATT_KV_HEADS * ATT_HD + h * ATT_HD
        v_cols += list(range(vb, vb + ATT_HD)) * ATT_GROUP
    z0 = D_MODEL + 2 * ATT_KV_HEADS * ATT_HD
    z_cols = list(range(z0, z0 + D_MODEL))
    order = np.asarray(q_cols + z_cols + k_cols + v_cols, np.int32)
    scale = np.ones((ATT_COLS,), np.float32)
    scale[:D_MODEL] = ATT_HD ** -0.5
    return (w_in[:, order] * scale).astype(BF16)


def _rope_tables():
    rows = SEQ // GRID_W
    row = jnp.repeat(jnp.arange(rows), GRID_W)
    col = jnp.tile(jnp.arange(GRID_W), rows)
    n_freq = ATT_HD // 4
    inv = 10000.0 ** (-jnp.arange(n_freq, dtype=F32) / n_freq)
    ang = jnp.concatenate([row[:, None] * inv, col[:, None] * inv], axis=-1)
    cos = jnp.concatenate([jnp.ones((CTX_LEN, ATT_HD // 2), F32), jnp.cos(ang)], axis=0)
    sin = jnp.concatenate([jnp.zeros((CTX_LEN, ATT_HD // 2), F32), jnp.sin(ang)], axis=0)
    return jnp.tile(cos, (1, ATT_GROUP)), jnp.tile(sin, (1, ATT_GROUP))


def kernel(x, c, ctx, c_ctx, ada_w, ada_b, norm_pre, norm_post, gdn_w_in, gdn_conv_w, gdn_a_log, gdn_dt_bias, gdn_g_norm, gdn_w_out, lru_w_in, lru_conv_w, lru_conv_b, lru_w_ra, lru_b_ra, lru_w_ri, lru_b_ri, lru_lam, lru_w_out, att_w_in, att_sinks, att_w_out):
    cc = jnp.zeros((MOD_ROWS, D_MODEL), F32).at[:c.shape[0]].set(c).at[CTX_MOD_ROW].set(c_ctx)
    mod_all = _modulation(cc, ada_w, ada_b).reshape(DEPTH, MOD_ROWS, 1, 3 * D_MODEL)
    xs = jnp.concatenate([ctx, x], axis=1)
    cos, sin = _rope_tables()
    for i in range(DEPTH):
        kind, j = i % 3, i // 3
        mod = mod_all[i]
        last = i == DEPTH - 1
        if kind == 0:
            p_main, p_ab = _in_proj(xs, mod, norm_pre[i], gdn_w_in[j].astype(BF16),
                                    [(0, GDN_MAIN), (GDN_MAIN, GDN_AB)], [ACT, F32])
            o = _gdn_mixer(p_main, p_ab, gdn_conv_w[j], gdn_a_log[j], gdn_dt_bias[j], gdn_g_norm[j])
            xs = _out_proj(o, p_main, 2, xs, mod, norm_post[i], gdn_w_out[j].astype(BF16), latent_only=last)
        elif kind == 1:
            (p,) = _in_proj(xs, mod, norm_pre[i], lru_w_in[j].astype(BF16), [(0, 2 * LRU_WIDTH)], [ACT])
            o = _lru_mixer(p, lru_conv_w[j], lru_conv_b[j], lru_w_ra[j], lru_w_ri[j],
                           lru_b_ra[j], lru_b_ri[j], lru_lam[j])
            xs = _out_proj(o, p, 0, xs, mod, norm_post[i], lru_w_out[j].astype(BF16), latent_only=last)
        else:
            (p,) = _in_proj(xs, mod, norm_pre[i], _att_weight_layout(att_w_in[j]), [(0, ATT_COLS)], [ACT],
                            rope=(cos, sin, (0, 1, 4, 5)))
            o = _att_mixer(p, att_sinks[j])
            xs = _out_proj(o, p, 1, xs, mod, norm_post[i], att_w_out[j].astype(BF16), latent_only=last)
    return xs
```

```python
import functools

import numpy as np
import jax
import jax.numpy as jnp
from jax import lax
from jax.experimental import pallas as pl
from jax.experimental.pallas import tpu as pltpu

F32 = jnp.float32
BF16 = jnp.bfloat16
ACT = jnp.bfloat16

D_MODEL = 1024
BATCH = 8
SEQ = 2048
CTX_LEN = 256
L_ALL = CTX_LEN + SEQ
DEPTH = 4
GRID_W = 64
EPS = 1e-6

ROW_TILE = 256
N_ROW_TILES = L_ALL // ROW_TILE
MOD_ROWS = 16
CTX_MOD_ROW = BATCH
VMEM_LIMIT = 56 * 1024 * 1024

GDN_HD = 128
GDN_QK_HEADS = 8
GDN_V_HEADS = 16
GDN_KEY_DIM = 1024
GDN_VAL_DIM = 2048
GDN_MAIN = 2 * GDN_KEY_DIM + 2 * GDN_VAL_DIM
GDN_AB = 64
CHUNK = 64
N_CHUNKS = L_ALL // CHUNK
CTX_CHUNKS = CTX_LEN // CHUNK
PAIR = 2 * CHUNK
A_GROUP = 9
INV_LEAF = 8

LRU_WIDTH = 1024
LRU_BLOCK = 128
LRU_BLOCKS = 8
LRU_C = 8.0
LRU_ROWS = 256
SUB = 8
N_GROUPS = L_ALL // SUB
CTX_GROUPS = CTX_LEN // SUB

ATT_HD = 64
ATT_KV_HEADS = 4
ATT_GROUP = 4
ATT_BLOCK = 128
ATT_WINDOW = 128
ATT_QW = ATT_GROUP * ATT_HD
ATT_COLS = 4 * D_MODEL
NEG = -1e30


def _silu(x):
    return x * jax.nn.sigmoid(x)


def _softplus(x):
    return jnp.maximum(x, 0.0) + jnp.log1p(jnp.exp(-jnp.abs(x)))


def _cparams(sem):
    return pltpu.CompilerParams(dimension_semantics=sem, vmem_limit_bytes=VMEM_LIMIT)


def _dot(a, b):
    return jnp.dot(a, b, preferred_element_type=F32)


def _dot_nt(a, b):
    return lax.dot_general(a, b, (((1,), (1,)), ((), ())), preferred_element_type=F32)


def _dot_tn(a, b):
    return lax.dot_general(a, b, (((0,), (0,)), ((), ())), preferred_element_type=F32)


def _mod_body(c_ref, w_ref, b_ref, o_ref):
    h = _silu(c_ref[...]).astype(BF16)
    o_ref[0] = _dot(h, w_ref[0].astype(BF16)) + b_ref[0]


def _modulation(cc, ada_w, ada_b):
    n_col = 3
    return pl.pallas_call(
        _mod_body,
        grid=(DEPTH, n_col),
        in_specs=[
            pl.BlockSpec((MOD_ROWS, D_MODEL), lambda i, n: (0, 0)),
            pl.BlockSpec((1, D_MODEL, D_MODEL), lambda i, n: (i, 0, n)),
            pl.BlockSpec((1, 1, D_MODEL), lambda i, n: (i, 0, n)),
        ],
        out_specs=pl.BlockSpec((1, MOD_ROWS, D_MODEL), lambda i, n: (i, 0, n)),
        out_shape=jax.ShapeDtypeStruct((DEPTH, MOD_ROWS, 3 * D_MODEL), F32),
        compiler_params=_cparams(("parallel", "parallel")),
    )(cc, ada_w, ada_b.reshape(DEPTH, 1, 3 * D_MODEL))


def _mod_index(b, t):
    return (jnp.where(t == 0, CTX_MOD_ROW, b), 0, 0)


def _in_proj_body(x_ref, mod_ref, nw_ref, w_ref, *refs, segs, rope_groups):
    if rope_groups:
        cos_ref, sin_ref = refs[:2]
        refs = refs[2:]
    x = x_ref[0]
    y = x * lax.rsqrt(jnp.mean(x * x, axis=-1, keepdims=True) + EPS) * nw_ref[...]
    shift = mod_ref[0, :, 0:D_MODEL]
    scale = mod_ref[0, :, D_MODEL:2 * D_MODEL]
    h = (y * (1.0 + scale) + shift).astype(BF16)
    for ref, (c0, width) in zip(refs, segs):
        step = min(width, 512)
        for n0 in range(0, width, step):
            acc = _dot(h, w_ref[:, c0 + n0:c0 + n0 + step])
            if (c0 + n0) // 512 in rope_groups:
                cos = cos_ref[...]
                sin = sin_ref[...]
                parts = []
                for g0 in range(0, step, 256):
                    x1 = acc[:, g0:g0 + 128]
                    x2 = acc[:, g0 + 128:g0 + 256]
                    parts += [x1 * cos - x2 * sin, x2 * cos + x1 * sin]
                acc = jnp.concatenate(parts, axis=1)
            ref[0, :, n0:n0 + step] = acc.astype(ref.dtype)


def _in_proj(x, mod, nw, w, segs, out_dtypes, rope=None):
    n_total = w.shape[1]
    in_specs = [
        pl.BlockSpec((1, ROW_TILE, D_MODEL), lambda b, t: (b, t, 0)),
        pl.BlockSpec((1, 1, 3 * D_MODEL), _mod_index),
        pl.BlockSpec((1, D_MODEL), lambda b, t: (0, 0)),
        pl.BlockSpec((D_MODEL, n_total), lambda b, t: (0, 0), pipeline_mode=pl.Buffered(1)),
    ]
    args = [x, mod, nw.reshape(1, D_MODEL), w]
    rope_groups = ()
    if rope is not None:
        cos, sin, rope_groups = rope
        in_specs += [pl.BlockSpec((ROW_TILE, 128), lambda b, t: (t, 0))] * 2
        args += [cos, sin]
    out_specs = [pl.BlockSpec((1, ROW_TILE, width), lambda b, t: (b, t, 0)) for _, width in segs]
    bsz = x.shape[0]
    out_shape = [jax.ShapeDtypeStruct((bsz, L_ALL, width), dt) for (_, width), dt in zip(segs, out_dtypes)]
    return pl.pallas_call(
        functools.partial(_in_proj_body, segs=tuple(segs), rope_groups=tuple(rope_groups)),
        grid=(bsz, N_ROW_TILES),
        in_specs=in_specs,
        out_specs=out_specs,
        out_shape=out_shape,
        compiler_params=_cparams(("parallel", "parallel")),
    )(*args)


def _out_proj_body(o_ref, z_ref, x_ref, mod_ref, nw_ref, w_ref, out_ref):
    g = (o_ref[0].astype(F32) * _silu(z_ref[0].astype(F32))).astype(BF16)
    y = _dot(g, w_ref[...])
    yn = y * lax.rsqrt(jnp.mean(y * y, axis=-1, keepdims=True) + EPS) * nw_ref[...]
    gate = mod_ref[0, :, 2 * D_MODEL:3 * D_MODEL]
    out_ref[0] = x_ref[0] + gate * yn


def _out_proj(o, z_arr, z_block, x, mod, nw, w, latent_only=False):
    width = o.shape[-1]
    bsz = o.shape[0]
    skip = CTX_LEN // ROW_TILE if latent_only else 0
    return pl.pallas_call(
        _out_proj_body,
        grid=(bsz, N_ROW_TILES - skip),
        in_specs=[
            pl.BlockSpec((1, ROW_TILE, width), lambda b, t: (b, t + skip, 0)),
            pl.BlockSpec((1, ROW_TILE, width), lambda b, t: (b, t + skip, z_block)),
            pl.BlockSpec((1, ROW_TILE, D_MODEL), lambda b, t: (b, t + skip, 0)),
            pl.BlockSpec((1, 1, 3 * D_MODEL), lambda b, t: _mod_index(b, t + skip)),
            pl.BlockSpec((1, D_MODEL), lambda b, t: (0, 0)),
            pl.BlockSpec((width, D_MODEL), lambda b, t: (0, 0), pipeline_mode=pl.Buffered(1)),
        ],
        out_specs=pl.BlockSpec((1, ROW_TILE, D_MODEL), lambda b, t: (b, t, 0)),
        out_shape=jax.ShapeDtypeStruct((bsz, L_ALL - skip * ROW_TILE, D_MODEL), F32),
        compiler_params=_cparams(("parallel", "parallel")),
    )(o, z_arr, x, mod, nw.reshape(1, D_MODEL), w)


def _conv_rows(ref, r0, rows, width_sl, cw, left_ok, right_ok):
    total = ref.shape[1]
    halo = 2 * SUB
    main = ref[0, pl.ds(r0, rows), width_sl].astype(F32)
    prev = ref[0, pl.ds(pl.multiple_of(jnp.maximum(r0 - halo, 0), halo), halo), width_sl].astype(F32) * left_ok
    nxt = ref[0, pl.ds(pl.multiple_of(jnp.minimum(r0 + rows, total - halo), halo), halo),
              width_sl].astype(F32) * right_ok
    xw = jnp.concatenate([prev, main, nxt], axis=0)
    return (cw[0:1] * xw[halo - 2:halo - 2 + rows] + cw[1:2] * xw[halo - 1:halo - 1 + rows]
            + cw[2:3] * xw[halo:halo + rows] + cw[3:4] * xw[halo + 1:halo + 1 + rows])


def _gdn_consts():
    i = np.arange(PAIR)
    same = (i[:, None] // CHUNK) == (i[None, :] // CHUNK)
    t_i, t_m = i[:, None] % CHUNK, i[None, :] % CHUNK
    low = same & (t_m <= t_i)
    upp = same & (t_m >= t_i)
    both = lambda f, b: np.concatenate([f, b], axis=1)
    eye = np.eye(PAIR, dtype=np.float32)
    mats = np.stack([
        both(np.where(low, 0.0, NEG), np.where(upp, 0.0, NEG)),
        both(same & (t_m < t_i), same & (t_m > t_i)).astype(np.float32),
        both(eye, eye)]).astype(np.float32)
    levels = [(i[:, None] // INV_LEAF) == (i[None, :] // INV_LEAF)]
    size = INV_LEAF
    while size < CHUNK:
        levels.append(((i[:, None] // (2 * size)) == (i[None, :] // (2 * size)))
                      & ((i[:, None] // size) != (i[None, :] // size)))
        size *= 2
    lvl = np.stack([both(m, m) for m in levels]).astype(np.float32)
    return jnp.asarray(mats, F32), jnp.asarray(lvl, BF16)


def _block_diag(x):
    zero = jnp.zeros((PAIR, PAIR), x.dtype)
    return jnp.concatenate([jnp.concatenate([x[:, :PAIR], zero], axis=1),
                            jnp.concatenate([zero, x[:, PAIR:]], axis=1)], axis=0)


def _take_groups(x, keep):
    return jnp.concatenate([x[g * SUB:(g + 1) * SUB] for g, k in enumerate(keep) if k], axis=0)


def _put_groups(y, keep, minuend=None):
    out, j = [], 0
    for g, k in enumerate(keep):
        base = None if minuend is None else minuend[g * SUB:(g + 1) * SUB]
        if k:
            piece = y[j * SUB:(j + 1) * SUB]
            out.append(piece if base is None else base - piece)
            j += 1
        else:
            out.append(jnp.zeros((SUB, y.shape[1]), y.dtype) if base is None else base)
    return jnp.concatenate(out, axis=0)


def _gdn_body(q_ref, k_ref, v_ref, cwq_ref, cwk_ref, cwv_ref, abr_ref, pr_ref,
              mats_ref, lvl_ref, gn_ref, o_ref,
              pad_s, lhs_s, c_s, o0_s, dec_s, st_s, acc_s, *stage_refs):
    pad_s[:, 0:SUB, :] = jnp.zeros((4, SUB, GDN_HD), F32)
    pad_s[:, SUB + CTX_LEN:2 * SUB + CTX_LEN, :] = jnp.zeros((4, SUB, GDN_HD), F32)
    pad_s[:, 2 * SUB + L_ALL:3 * SUB + L_ALL, :] = jnp.zeros((4, SUB, GDN_HD), F32)

    def fill(t, carry):
        src = pl.multiple_of(t * ROW_TILE, ROW_TILE)
        dst = pl.multiple_of(src + jnp.where(t == 0, SUB, 2 * SUB), SUB)
        pad_s[0, pl.ds(dst, ROW_TILE), :] = q_ref[0, pl.ds(src, ROW_TILE), :].astype(F32)
        pad_s[1, pl.ds(dst, ROW_TILE), :] = k_ref[0, pl.ds(src, ROW_TILE), :].astype(F32)
        pad_s[2, pl.ds(dst, ROW_TILE), :] = v_ref[0, pl.ds(src, ROW_TILE), 0:GDN_HD].astype(F32)
        pad_s[3, pl.ds(dst, ROW_TILE), :] = v_ref[0, pl.ds(src, ROW_TILE), GDN_HD:].astype(F32)
        return carry

    lax.fori_loop(0, N_ROW_TILES, fill, 0)
    stage_a, stage_b = stage_refs[:5], stage_refs[5:]
    t_lane = lax.broadcasted_iota(jnp.int32, (A_GROUP * SUB, PAIR), 1)
    t_in_chunk = t_lane % CHUNK

    def conv(which, base, cw):
        taps = [pad_s[which, pl.ds(base + (k - 2), CHUNK), :] for k in range(4)]
        return _silu(cw[0:1] * taps[0] + cw[1:2] * taps[1] + cw[2:3] * taps[2] + cw[3:4] * taps[3])

    def lanes2(f, b):
        return jnp.concatenate([jnp.broadcast_to(f, (PAIR, PAIR)), jnp.broadcast_to(b, (PAIR, PAIR))], axis=1)

    def group_gates(first):
        zero = jnp.zeros((SUB - 4, PAIR), F32)
        xr = jnp.concatenate([x for g in range(A_GROUP) for x in (abr_ref[0, 0, first + g], zero)], axis=0)
        g_r = -jnp.exp(jnp.tile(pr_ref[0, 0], (A_GROUP, 1))) * _softplus(xr + jnp.tile(pr_ref[0, 1], (A_GROUP, 1)))
        beta_r = jax.nn.sigmoid(xr)
        cum_f, cum_b = g_r, g_r
        step = 1
        while step < CHUNK:
            cum_f = cum_f + jnp.where(t_in_chunk >= step, pltpu.roll(cum_f, step, 1), 0.0)
            cum_b = cum_b + jnp.where(t_in_chunk < CHUNK - step, pltpu.roll(cum_b, PAIR - step, 1), 0.0)
            step *= 2
        tot = cum_f + cum_b - g_r
        other = pltpu.roll(tot, CHUNK, 1)
        tot_h0 = jnp.where(t_lane < CHUNK, tot, other)
        tot_h1 = jnp.where(t_lane < CHUNK, other, tot)
        pad = jnp.zeros((PAIR - A_GROUP * SUB, PAIR), F32)
        col = lambda x: jnp.concatenate([x, pad], axis=0).T
        return dict(cum_f=cum_f, cum_b=cum_b, cum_f_t=col(cum_f), cum_b_t=col(cum_b), tot_t=col(tot),
                    beta_t=col(beta_r), dec=jnp.exp(jnp.concatenate([tot_h0, tot_h1], axis=1)))

    def chunk_start(c, g, gt, stage):
        base = c * CHUNK + (SUB if c < CTX_CHUNKS else 2 * SUB)
        cwv = cwv_ref[...]
        qc = conv(0, base, cwq_ref[...])
        kc = conv(1, base, cwk_ref[...])
        v2 = jnp.concatenate([conv(2, base, cwv[:, :GDN_HD]), conv(3, base, cwv[:, GDN_HD:])], axis=0)
        qn = qc * (lax.rsqrt(jnp.sum(qc * qc, axis=-1, keepdims=True) + 1e-6) * (GDN_HD ** -0.5))
        kn = kc * lax.rsqrt(jnp.sum(kc * kc, axis=-1, keepdims=True) + 1e-6)
        q2 = jnp.concatenate([qn, qn], axis=0)
        k2 = jnp.concatenate([kn, kn], axis=0)
        gram = _dot_nt(jnp.concatenate([kn, qn], axis=0).astype(BF16), k2.astype(BF16))
        kk = jnp.concatenate([gram[:CHUNK], gram[:CHUNK]], axis=0)
        qk = jnp.concatenate([gram[CHUNK:], gram[CHUNK:]], axis=0)
        r = g * SUB
        gc_f, gc_b = gt["cum_f"][r:r + 1], gt["cum_b"][r + 2:r + 3]
        gcc_d = [gt["cum_f_t"][:, r:r + 1], gt["cum_b_t"][:, r + 2:r + 3]]
        totc_d = [gt["tot_t"][:, r:r + 1], gt["tot_t"][:, r + 2:r + 3]]
        bc_d = [gt["beta_t"][:, r + 1:r + 2], gt["beta_t"][:, r + 3:r + 4]]
        m = jnp.exp(lanes2(gcc_d[0], gcc_d[1]) - jnp.concatenate([gc_f, gc_b], axis=1) + mats_ref[0])
        a = (jnp.concatenate([kk, kk], axis=1) * m * mats_ref[1]
             * lanes2(bc_d[0], bc_d[1])).astype(BF16)
        qkm = (jnp.concatenate([qk, qk], axis=1) * m).astype(BF16)
        sa, srhs, sqd, skd, sqkm = stage
        sa[g] = a
        for d in range(2):
            gcc, totc, bc = gcc_d[d], totc_d[d], bc_d[d]
            egc = jnp.exp(gcc)
            dec_s[d * N_CHUNKS + c] = jnp.broadcast_to(gt["dec"][r + 2 * d:r + 2 * d + 1], (SUB, 2 * GDN_HD))
            srhs[2 * g + d] = jnp.concatenate([v2 * bc, k2 * (bc * egc)], axis=1).astype(BF16)
            sqd[2 * g + d] = q2 * egc
            skd[2 * g + d] = (k2 * jnp.exp(totc - gcc)).astype(BF16)
            sqkm[2 * g + d] = qkm[:, d * PAIR:(d + 1) * PAIR]

    def prepare(it, stage):
        gt = group_gates(it * A_GROUP)
        for g in range(A_GROUP):
            chunk_start(it * A_GROUP + g, g, gt, stage)

    def mm(p, q):
        return _dot(p.astype(BF16), _block_diag(q.astype(BF16)))

    def solve(it, stage, steps):
        sa, srhs, sqd, skd, sqkm = stage
        steps = list(steps)

        def tick():
            if steps:
                recur(steps.pop(0))

        chunks = [dict(a=sa[g], dirs=[dict(idx=d * N_CHUNKS + it * A_GROUP + g, rhs=srhs[2 * g + d],
                                           qd=sqd[2 * g + d], kd=skd[2 * g + d], qkm=sqkm[2 * g + d])
                                      for d in range(2)]) for g in range(A_GROUP)]
        n_levels = lvl_ref.shape[0]
        for ch in chunks:
            leaf = ch["a"] * lvl_ref[0]
            ch["x"] = mats_ref[2] - leaf.astype(F32)
            ch["p"] = mm(leaf, leaf)
        tick()
        for ch in chunks:
            xp = mm(jnp.concatenate([ch["x"], ch["p"]], axis=0), ch["p"])
            ch["x"] = ch["x"] + xp[:PAIR]
            ch["p"] = xp[PAIR:]
        tick()
        for ch in chunks:
            ch["x"] = ch["x"] + mm(ch["x"], ch["p"])
        tick()
        for lv in range(1, n_levels):
            size = INV_LEAF << (lv - 1)
            later = [(gi * SUB // size) % 2 == 1 for gi in range(PAIR // SUB)]
            keep = (later, [not k for k in later])

            def take(x):
                return jnp.concatenate([_take_groups(x[:, d * PAIR:(d + 1) * PAIR], keep[d]) for d in range(2)],
                                       axis=1)

            def put(y, minuend=None):
                return jnp.concatenate(
                    [_put_groups(y[:, d * PAIR:(d + 1) * PAIR], keep[d],
                                 None if minuend is None else minuend[:, d * PAIR:(d + 1) * PAIR])
                     for d in range(2)], axis=1)

            for ch in chunks:
                ch["y"] = mm(take((ch["a"] * lvl_ref[lv]).astype(F32)), ch["x"])
            tick()
            for ch in chunks:
                ch["x"] = put(mm(take(ch["x"]), put(ch["y"])), minuend=ch["x"])
            tick()
        chains = []
        for ch in chunks:
            xb = ch["x"].astype(BF16)
            for d, cd in enumerate(ch["dirs"]):
                cd["uw"] = _dot(xb[:, d * PAIR:(d + 1) * PAIR], cd["rhs"]).astype(BF16)
                chains.append(cd)
        tick()
        for cd in chains:
            wu = jnp.concatenate([cd["uw"][:, GDN_HD:], cd["uw"][:, :GDN_HD]], axis=1)
            cd["nc"] = [_dot_tn(cd["kd"][h * CHUNK:(h + 1) * CHUNK], wu[h * CHUNK:(h + 1) * CHUNK])
                        for h in range(2)]
            cd["qo"] = _dot(cd["qkm"], wu)
        for cd in chains:
            nc, qo, idx = cd["nc"], cd["qo"], cd["idx"]
            lhs_s[idx] = jnp.concatenate([-nc[0][:, :GDN_HD], -nc[1][:, :GDN_HD],
                                          cd["qd"] - qo[:, :GDN_HD]], axis=0).astype(BF16)
            c_s[idx] = jnp.concatenate([nc[0][:, GDN_HD:], nc[1][:, GDN_HD:]], axis=1)
            o0_s[idx] = qo[:, GDN_HD:].astype(o0_s.dtype)
        while steps:
            tick()

    def recur(step):
        chains = [dict(d=d, c=c, idx=d * N_CHUNKS + c) for d, c in step]
        for ch in chains:
            ch["st"] = st_s[ch["d"]]
            ch["r"] = _dot(lhs_s[ch["idx"]], ch["st"].astype(BF16))
        for ch in chains:
            r, idx = ch["r"], ch["idx"]
            ns = jnp.concatenate([r[0:GDN_HD, :GDN_HD], r[GDN_HD:2 * GDN_HD, GDN_HD:]], axis=1)
            st_s[ch["d"]] = ch["st"] * dec_s[idx][0:1] + ns + c_s[idx]
            o0 = o0_s[idx]
            r0 = ch["c"] * CHUNK
            acc_s[r0:r0 + CHUNK, 0:GDN_HD] += r[2 * GDN_HD:2 * GDN_HD + CHUNK, :GDN_HD] + o0[:CHUNK]
            acc_s[r0:r0 + CHUNK, GDN_HD:] += r[2 * GDN_HD + CHUNK:, GDN_HD:] + o0[CHUNK:]

    st_s[...] = jnp.zeros_like(st_s)
    acc_s[...] = jnp.zeros_like(acc_s)

    order_f = list(range(N_CHUNKS))
    order_b = list(range(CTX_CHUNKS - 1, -1, -1)) + list(range(N_CHUNKS - 1, CTX_CHUNKS - 1, -1))
    n_groups = N_CHUNKS // A_GROUP
    group_order, lo, hi = [], 0, n_groups - 1
    while lo <= hi:
        group_order.append(lo)
        lo += 1
        if lo <= hi:
            group_order.append(hi)
            hi -= 1
    stages = (stage_a, stage_b)
    solved, pos = set(), [0, 0]

    def ready_steps():
        out = []
        while True:
            step = [(d, order[pos[d]]) for d, order in enumerate((order_f, order_b))
                    if pos[d] < N_CHUNKS and order[pos[d]] in solved]
            if not step:
                return out
            for d, _ in step:
                pos[d] += 1
            out.append(step)

    prepare(group_order[0], stages[0])
    for n, grp in enumerate(group_order):
        if n + 1 < n_groups:
            prepare(group_order[n + 1], stages[(n + 1) % 2])
        solve(grp, stages[n % 2], ready_steps())
        solved.update(range(grp * A_GROUP, (grp + 1) * A_GROUP))
    for step in ready_steps():
        recur(step)

    def finish(t, carry):
        r0 = pl.multiple_of(t * ROW_TILE, ROW_TILE)
        for h in range(2):
            sl = slice(h * GDN_HD, (h + 1) * GDN_HD)
            x = acc_s[pl.ds(r0, ROW_TILE), sl]
            o_ref[0, pl.ds(r0, ROW_TILE), sl] = (
                x * lax.rsqrt(jnp.mean(x * x, axis=-1, keepdims=True) + EPS) * gn_ref[...]).astype(o_ref.dtype)
        return carry

    lax.fori_loop(0, N_ROW_TILES, finish, 0)


def _gdn_mixer(p_main, p_ab, conv_w, a_log, dt_bias, g_norm):
    mats, lvl = _gdn_consts()
    bsz = p_main.shape[0]
    ab = p_ab.reshape(bsz, N_CHUNKS, CHUNK, 2, 2, GDN_QK_HEADS, 2)
    abr = ab.transpose(0, 5, 1, 4, 3, 6, 2).reshape(bsz, GDN_QK_HEADS, N_CHUNKS, 4, PAIR)
    prm = jnp.stack([a_log, dt_bias], axis=0).astype(F32).reshape(2, 2, GDN_QK_HEADS, 2)
    zero = jnp.zeros_like(prm[:, 0])
    rows = jnp.stack([prm[:, 0], prm[:, 0], prm[:, 1], prm[:, 1]] + [zero] * (SUB - 4), axis=1)
    pr = jnp.repeat(rows.transpose(2, 0, 1, 3), CHUNK, axis=3)

    grid = (bsz, GDN_QK_HEADS)
    kq = GDN_KEY_DIM // GDN_HD
    in_specs = [
        pl.BlockSpec((1, L_ALL, GDN_HD), lambda b, j: (b, 0, j)),
        pl.BlockSpec((1, L_ALL, GDN_HD), lambda b, j: (b, 0, kq + j)),
        pl.BlockSpec((1, L_ALL, 2 * GDN_HD), lambda b, j: (b, 0, kq + j)),
        pl.BlockSpec((4, GDN_HD), lambda b, j: (0, j)),
        pl.BlockSpec((4, GDN_HD), lambda b, j: (0, kq + j)),
        pl.BlockSpec((4, 2 * GDN_HD), lambda b, j: (0, kq + j)),
        pl.BlockSpec((1, 1, N_CHUNKS, 4, PAIR), lambda b, j: (b, j, 0, 0, 0)),
        pl.BlockSpec((1, 2, SUB, PAIR), lambda b, j: (j, 0, 0, 0)),
        pl.BlockSpec(mats.shape, lambda b, j: (0, 0, 0)),
        pl.BlockSpec(lvl.shape, lambda b, j: (0, 0, 0)),
        pl.BlockSpec((1, GDN_HD), lambda b, j: (0, 0)),
    ]
    return pl.pallas_call(
        _gdn_body,
        grid=grid,
        in_specs=in_specs,
        out_specs=pl.BlockSpec((1, L_ALL, 2 * GDN_HD), lambda b, j: (b, 0, j)),
        out_shape=jax.ShapeDtypeStruct((bsz, L_ALL, GDN_VAL_DIM), ACT),
        scratch_shapes=[
            pltpu.VMEM((4, L_ALL + 3 * SUB, GDN_HD), F32),
            pltpu.VMEM((2 * N_CHUNKS, 3 * GDN_HD, GDN_HD), BF16),
            pltpu.VMEM((2 * N_CHUNKS, GDN_HD, 2 * GDN_HD), F32),
            pltpu.VMEM((2 * N_CHUNKS, PAIR, GDN_HD), BF16),
            pltpu.VMEM((2 * N_CHUNKS, SUB, 2 * GDN_HD), F32),
            pltpu.VMEM((2, GDN_HD, 2 * GDN_HD), F32),
            pltpu.VMEM((L_ALL, 2 * GDN_HD), F32),
        ] + 2 * [
            pltpu.VMEM((A_GROUP, PAIR, 2 * PAIR), BF16),
            pltpu.VMEM((2 * A_GROUP, PAIR, 2 * GDN_HD), BF16),
            pltpu.VMEM((2 * A_GROUP, PAIR, GDN_HD), F32),
            pltpu.VMEM((2 * A_GROUP, PAIR, GDN_HD), BF16),
            pltpu.VMEM((2 * A_GROUP, PAIR, PAIR), BF16),
        ],
        compiler_params=_cparams(("parallel", "parallel")),
    )(p_main, p_main, p_main, conv_w, conv_w, conv_w, abr, pr, mats, lvl,
      g_norm.reshape(1, GDN_HD))


def _lru_body(u_ref, cw_ref, cb_ref, wra_ref, wri_ref, bra_ref, bri_ref, lam_ref, o_ref, a_s, b_s, acc_s):
    full = slice(None)
    n_steps = L_ALL // LRU_ROWS

    def gates(c, carry):
        r0 = pl.multiple_of(c * LRU_ROWS, LRU_ROWS)
        left_ok = jnp.where(c >= 2, 1.0, 0.0)
        right_ok = jnp.where((c == 0) | (c == n_steps - 1), 0.0, 1.0)
        u = _conv_rows(u_ref, r0, LRU_ROWS, full, cw_ref[...], left_ok, right_ok) + cb_ref[...]
        ub = u.astype(BF16)
        for d in range(2):
            r = jax.nn.sigmoid(_dot(ub, wra_ref[d, 0]) + bra_ref[d:d + 1])
            i = jax.nn.sigmoid(_dot(ub, wri_ref[d, 0]) + bri_ref[d:d + 1])
            log_a = -LRU_C * r * _softplus(-lam_ref[d:d + 1])
            a = jnp.exp(log_a)
            one_m_a2 = -jnp.tanh(log_a) * (a * a + 1.0)
            a_s[d, pl.ds(r0, LRU_ROWS), :] = a
            b_s[d, pl.ds(r0, LRU_ROWS), :] = jnp.sqrt(one_m_a2) * (i * u)
        return carry

    lax.fori_loop(0, n_steps, gates, 0)

    acc_s[...] = jnp.zeros_like(acc_s)
    row = lax.broadcasted_iota(jnp.int32, (SUB, LRU_BLOCK), 0)

    def scan(g, carry):
        hf, hb = carry
        gb = jnp.where(g < CTX_GROUPS, CTX_GROUPS - 1 - g, N_GROUPS + CTX_GROUPS - 1 - g)
        r0 = pl.multiple_of(g * SUB, SUB)
        a, b = a_s[0, pl.ds(r0, SUB), :], b_s[0, pl.ds(r0, SUB), :]
        for s in (1, 2, 4):
            ok = row >= s
            b = jnp.where(ok, a * pltpu.roll(b, s, 0) + b, b)
            a = jnp.where(ok, a * pltpu.roll(a, s, 0), a)
        h = b + a * hf
        acc_s[pl.ds(r0, SUB), :] += h
        hf = h[SUB - 1:SUB]
        r0 = pl.multiple_of(gb * SUB, SUB)
        a, b = a_s[1, pl.ds(r0, SUB), :], b_s[1, pl.ds(r0, SUB), :]
        for s in (1, 2, 4):
            ok = row < SUB - s
            b = jnp.where(ok, a * pltpu.roll(b, SUB - s, 0) + b, b)
            a = jnp.where(ok, a * pltpu.roll(a, SUB - s, 0), a)
        h = b + a * hb
        acc_s[pl.ds(r0, SUB), :] += h
        hb = h[0:1]
        return hf, hb

    zero = jnp.zeros((1, LRU_BLOCK), F32)
    lax.fori_loop(0, N_GROUPS, scan, (zero, zero), unroll=4)

    def emit(t, carry):
        r0 = pl.multiple_of(t * LRU_ROWS, LRU_ROWS)
        o_ref[0, pl.ds(r0, LRU_ROWS), :] = acc_s[pl.ds(r0, LRU_ROWS), :].astype(o_ref.dtype)
        return carry

    lax.fori_loop(0, n_steps, emit, 0)


def _lru_mixer(p, conv_w, conv_b, w_ra, w_ri, b_ra, b_ri, lam):
    nb = LRU_WIDTH // LRU_BLOCK
    bsz = p.shape[0]
    vec = lambda rows: pl.BlockSpec((rows, LRU_BLOCK), lambda b, n: (0, n))
    wspec = pl.BlockSpec((2, 1, LRU_BLOCK, LRU_BLOCK), lambda b, n: (0, n, 0, 0))
    return pl.pallas_call(
        _lru_body,
        grid=(bsz, LRU_BLOCKS),
        in_specs=[
            pl.BlockSpec((1, L_ALL, LRU_BLOCK), lambda b, n: (b, 0, nb + n)),
            vec(4), vec(1), wspec, wspec, vec(2), vec(2), vec(2),
        ],
        out_specs=pl.BlockSpec((1, L_ALL, LRU_BLOCK), lambda b, n: (b, 0, n)),
        out_shape=jax.ShapeDtypeStruct((bsz, L_ALL, LRU_WIDTH), ACT),
        scratch_shapes=[pltpu.VMEM((2, L_ALL, LRU_BLOCK), F32), pltpu.VMEM((2, L_ALL, LRU_BLOCK), F32),
                        pltpu.VMEM((L_ALL, LRU_BLOCK), F32)],
        compiler_params=_cparams(("parallel", "parallel")),
    )(p, conv_w, conv_b.reshape(1, LRU_WIDTH), w_ra.astype(BF16), w_ri.astype(BF16), b_ra, b_ri, lam)


def _att_body(sink_ref, q_ref, kc_ref, vc_ref, k0_ref, k1_ref, k2_ref, v0_ref, v1_ref, v2_ref, o_ref):
    i = pl.program_id(1)
    n_ctx_blocks = CTX_LEN // ATT_BLOCK
    li = i - n_ctx_blocks
    qpos = li * ATT_BLOCK + lax.broadcasted_iota(jnp.int32, (ATT_BLOCK, 3 * ATT_BLOCK), 0)
    kpos = (li - 1) * ATT_BLOCK + lax.broadcasted_iota(jnp.int32, (ATT_BLOCK, 3 * ATT_BLOCK), 1)
    band = (jnp.abs(qpos - kpos) <= ATT_WINDOW) & (kpos >= 0) & (kpos < SEQ) & (li >= 0)
    bias = jnp.concatenate([jnp.zeros((ATT_BLOCK, CTX_LEN), F32), jnp.where(band, 0.0, NEG)], axis=1)
    bias4 = jnp.concatenate([bias] * ATT_GROUP, axis=0)
    lane = lax.broadcasted_iota(jnp.int32, (ATT_BLOCK, ATT_QW), 1)
    qmask = [(lane % 128) // 32 == g for g in range(ATT_GROUP)]
    omask = [lane // ATT_HD == g for g in range(ATT_GROUP)]
    rowg = lax.broadcasted_iota(jnp.int32, (ATT_GROUP * ATT_BLOCK, 1), 0) // ATT_BLOCK
    for h in range(ATT_KV_HEADS):
        sl = slice(h * ATT_QW, (h + 1) * ATT_QW)
        qh = q_ref[0, :, sl]
        qm = jnp.concatenate([jnp.where(qmask[g], qh, 0.0) for g in range(ATT_GROUP)], axis=0).astype(BF16)
        keys = jnp.concatenate([kc_ref[0, :, sl], k0_ref[0, :, sl], k1_ref[0, :, sl], k2_ref[0, :, sl]],
                               axis=0).astype(BF16)
        vals = jnp.concatenate([vc_ref[0, :, sl], v0_ref[0, :, sl], v1_ref[0, :, sl], v2_ref[0, :, sl]],
                               axis=0).astype(BF16)
        s = _dot_nt(qm, keys) + bias4
        sink = jnp.zeros((ATT_GROUP * ATT_BLOCK, 1), F32)
        for g in range(ATT_GROUP):
            sink = jnp.where(rowg == g, sink_ref[h * ATT_GROUP + g], sink)
        mx = jnp.maximum(jnp.max(s, axis=-1, keepdims=True), sink)
        e = jnp.exp(s - mx)
        den = jnp.sum(e, axis=-1, keepdims=True) + jnp.exp(sink - mx)
        r = _dot((e / den).astype(BF16), vals)
        out = jnp.zeros((ATT_BLOCK, ATT_QW), F32)
        for g in range(ATT_GROUP):
            out = jnp.where(omask[g], r[g * ATT_BLOCK:(g + 1) * ATT_BLOCK], out)
        o_ref[0, :, sl] = out.astype(o_ref.dtype)


def _att_mixer(p, sinks):
    n_ctx_blocks = CTX_LEN // ATT_BLOCK
    n_lat_blocks = SEQ // ATT_BLOCK
    bsz = p.shape[0]

    def win(off, col):
        def index(b, i, sink_ref):
            blk = jnp.clip(i - n_ctx_blocks + off, 0, n_lat_blocks - 1)
            return (b, n_ctx_blocks + blk, col)
        return pl.BlockSpec((1, ATT_BLOCK, D_MODEL), index)

    grid_spec = pltpu.PrefetchScalarGridSpec(
        num_scalar_prefetch=1,
        grid=(bsz, L_ALL // ATT_BLOCK),
        in_specs=[
            pl.BlockSpec((1, ATT_BLOCK, D_MODEL), lambda b, i, s: (b, i, 0)),
            pl.BlockSpec((1, CTX_LEN, D_MODEL), lambda b, i, s: (b, 0, 2)),
            pl.BlockSpec((1, CTX_LEN, D_MODEL), lambda b, i, s: (b, 0, 3)),
            win(-1, 2), win(0, 2), win(1, 2), win(-1, 3), win(0, 3), win(1, 3),
        ],
        out_specs=pl.BlockSpec((1, ATT_BLOCK, D_MODEL), lambda b, i, s: (b, i, 0)),
    )
    return pl.pallas_call(
        _att_body,
        grid_spec=grid_spec,
        out_shape=jax.ShapeDtypeStruct((bsz, L_ALL, D_MODEL), ACT),
        compiler_params=_cparams(("parallel", "arbitrary")),
    )(sinks.astype(F32), p, p, p, p, p, p, p, p, p)


def _att_weight_layout(w_in):
    half = ATT_HD // 2
    q_cols, k_cols, v_cols = [], [], []
    for h in range(ATT_KV_HEADS):
        for part in range(2):
            for g in range(ATT_GROUP):
                base = (h * ATT_GROUP + g) * ATT_HD + part * half
                q_cols += list(range(base, base + half))
                kb = D_MODEL + h * ATT_HD + part * half
                k_cols += list(range(kb, kb + half))
        vb = D_MODEL + ATT_KV_HEADS * ATT_HD + h * ATT_HD
        v_cols += list(range(vb, vb + ATT_HD)) * ATT_GROUP
    z0 = D_MODEL + 2 * ATT_KV_HEADS * ATT_HD
    z_cols = list(range(z0, z0 + D_MODEL))
    order = np.asarray(q_cols + z_cols + k_cols + v_cols, np.int32)
    scale = np.ones((ATT_COLS,), np.float32)
    scale[:D_MODEL] = ATT_HD ** -0.5
    return (w_in[:, order] * scale).astype(BF16)


def _rope_tables():
    rows = SEQ // GRID_W
    row = jnp.repeat(jnp.arange(rows), GRID_W)
    col = jnp.tile(jnp.arange(GRID_W), rows)
    n_freq = ATT_HD // 4
    inv = 10000.0 ** (-jnp.arange(n_freq, dtype=F32) / n_freq)
    ang = jnp.concatenate([row[:, None] * inv, col[:, None] * inv], axis=-1)
    cos = jnp.concatenate([jnp.ones((CTX_LEN, ATT_HD // 2), F32), jnp.cos(ang)], axis=0)
    sin = jnp.concatenate([jnp.zeros((CTX_LEN, ATT_HD // 2), F32), jnp.sin(ang)], axis=0)
    return jnp.tile(cos, (1, ATT_GROUP)), jnp.tile(sin, (1, ATT_GROUP))


def kernel(x, c, ctx, c_ctx, ada_w, ada_b, norm_pre, norm_post, gdn_w_in, gdn_conv_w, gdn_a_log, gdn_dt_bias, gdn_g_norm, gdn_w_out, lru_w_in, lru_conv_w, lru_conv_b, lru_w_ra, lru_b_ra, lru_w_ri, lru_b_ri, lru_lam, lru_w_out, att_w_in, att_sinks, att_w_out):
    cc = jnp.zeros((MOD_ROWS, D_MODEL), F32).at[:c.shape[0]].set(c).at[CTX_MOD_ROW].set(c_ctx)
    mod_all = _modulation(cc, ada_w, ada_b).reshape(DEPTH, MOD_ROWS, 1, 3 * D_MODEL)
    xs = jnp.concatenate([ctx, x], axis=1)
    cos, sin = _rope_tables()
    for i in range(DEPTH):
        kind, j = i % 3, i // 3
        mod = mod_all[i]
        last = i == DEPTH - 1
        if kind == 0:
            p_main, p_ab = _in_proj(xs, mod, norm_pre[i], gdn_w_in[j].astype(BF16),
                                    [(0, GDN_MAIN), (GDN_MAIN, GDN_AB)], [ACT, F32])
            o = _gdn_mixer(p_main, p_ab, gdn_conv_w[j], gdn_a_log[j], gdn_dt_bias[j], gdn_g_norm[j])
            xs = _out_proj(o, p_main, 2, xs, mod, norm_post[i], gdn_w_out[j].astype(BF16), latent_only=last)
        elif kind == 1:
            (p,) = _in_proj(xs, mod, norm_pre[i], lru_w_in[j].astype(BF16), [(0, 2 * LRU_WIDTH)], [ACT])
            o = _lru_mixer(p, lru_conv_w[j], lru_conv_b[j], lru_w_ra[j], lru_w_ri[j],
                           lru_b_ra[j], lru_b_ri[j], lru_lam[j])
            xs = _out_proj(o, p, 0, xs, mod, norm_post[i], lru_w_out[j].astype(BF16), latent_only=last)
        else:
            (p,) = _in_proj(xs, mod, norm_pre[i], _att_weight_layout(att_w_in[j]), [(0, ATT_COLS)], [ACT],
                            rope=(cos, sin, (0, 1, 4, 5)))
            o = _att_mixer(p, att_sinks[j])
            xs = _out_proj(o, p, 1, xs, mod, norm_post[i], att_w_out[j].astype(BF16), latent_only=last)
    return xs
```

```python
import functools

import numpy as np
import jax
import jax.numpy as jnp
from jax import lax
from jax.experimental import pallas as pl
from jax.experimental.pallas import tpu as pltpu

F32 = jnp.float32
BF16 = jnp.bfloat16
ACT = jnp.bfloat16

D_MODEL = 1024
BATCH = 8
SEQ = 2048
CTX_LEN = 256
L_ALL = CTX_LEN + SEQ
DEPTH = 4
GRID_W = 64
EPS = 1e-6

ROW_TILE = 256
N_ROW_TILES = L_ALL // ROW_TILE
MOD_ROWS = 16
CTX_MOD_ROW = BATCH
VMEM_LIMIT = 56 * 1024 * 1024

GDN_HD = 128
GDN_QK_HEADS = 8
GDN_V_HEADS = 16
GDN_KEY_DIM = 1024
GDN_VAL_DIM = 2048
GDN_MAIN = 2 * GDN_KEY_DIM + 2 * GDN_VAL_DIM
GDN_AB = 64
CHUNK = 64
N_CHUNKS = L_ALL // CHUNK
CTX_CHUNKS = CTX_LEN // CHUNK
PAIR = 2 * CHUNK
GDN_GROUPS = ((0, 4), (28, 8), (4, 8), (22, 6), (12, 6), (18, 4))
A_GROUP = max(n for _, n in GDN_GROUPS)
assert sorted(c for f, n in GDN_GROUPS for c in range(f, f + n)) == list(range(N_CHUNKS))
INV_LEAF = 8

LRU_WIDTH = 1024
LRU_BLOCK = 128
LRU_LANES = 256
LRU_C = 8.0
LRU_ROWS = 256
SUB = 8
N_GROUPS = L_ALL // SUB
CTX_GROUPS = CTX_LEN // SUB

ATT_HD = 64
ATT_KV_HEADS = 4
ATT_GROUP = 4
ATT_BLOCK = 128
ATT_WINDOW = 128
ATT_QW = ATT_GROUP * ATT_HD
ATT_COLS = 4 * D_MODEL
NEG = -1e30


def _silu(x):
    return x * jax.nn.sigmoid(x)


def _softplus(x):
    return jnp.maximum(x, 0.0) + jnp.log1p(jnp.exp(-jnp.abs(x)))


def _cparams(sem):
    return pltpu.CompilerParams(dimension_semantics=sem, vmem_limit_bytes=VMEM_LIMIT)


def _dot(a, b):
    return jnp.dot(a, b, preferred_element_type=F32)


def _dot_nt(a, b):
    return lax.dot_general(a, b, (((1,), (1,)), ((), ())), preferred_element_type=F32)


def _dot_tn(a, b):
    return lax.dot_general(a, b, (((0,), (0,)), ((), ())), preferred_element_type=F32)


def _mod_body(c_ref, w_ref, b_ref, o_ref):
    h = _silu(c_ref[...]).astype(BF16)
    o_ref[0] = _dot(h, w_ref[0].astype(BF16)) + b_ref[0]


def _modulation(cc, ada_w, ada_b):
    n_col = 3
    return pl.pallas_call(
        _mod_body,
        grid=(DEPTH, n_col),
        in_specs=[
            pl.BlockSpec((MOD_ROWS, D_MODEL), lambda i, n: (0, 0)),
            pl.BlockSpec((1, D_MODEL, D_MODEL), lambda i, n: (i, 0, n)),
            pl.BlockSpec((1, 1, D_MODEL), lambda i, n: (i, 0, n)),
        ],
        out_specs=pl.BlockSpec((1, MOD_ROWS, D_MODEL), lambda i, n: (i, 0, n)),
        out_shape=jax.ShapeDtypeStruct((DEPTH, MOD_ROWS, 3 * D_MODEL), F32),
        compiler_params=_cparams(("parallel", "parallel")),
    )(cc, ada_w, ada_b.reshape(DEPTH, 1, 3 * D_MODEL))


def _mod_index(b, t):
    return (jnp.where(t == 0, CTX_MOD_ROW, b), 0, 0)


def _in_proj_body(x_ref, mod_ref, nw_ref, w_ref, *refs, segs, rope_groups):
    if rope_groups:
        cos_ref, sin_ref = refs[:2]
        refs = refs[2:]
    x = x_ref[0]
    y = x * lax.rsqrt(jnp.mean(x * x, axis=-1, keepdims=True) + EPS) * nw_ref[...]
    shift = mod_ref[0, :, 0:D_MODEL]
    scale = mod_ref[0, :, D_MODEL:2 * D_MODEL]
    h = (y * (1.0 + scale) + shift).astype(BF16)
    for ref, (c0, width) in zip(refs, segs):
        step = min(width, 512)
        for n0 in range(0, width, step):
            acc = _dot(h, w_ref[:, c0 + n0:c0 + n0 + step])
            if (c0 + n0) // 512 in rope_groups:
                cos = cos_ref[...]
                sin = sin_ref[...]
                parts = []
                for g0 in range(0, step, 256):
                    x1 = acc[:, g0:g0 + 128]
                    x2 = acc[:, g0 + 128:g0 + 256]
                    parts += [x1 * cos - x2 * sin, x2 * cos + x1 * sin]
                acc = jnp.concatenate(parts, axis=1)
            ref[0, :, n0:n0 + step] = acc.astype(ref.dtype)


def _in_proj(x, mod, nw, w, segs, out_dtypes, rope=None):
    n_total = w.shape[1]
    in_specs = [
        pl.BlockSpec((1, ROW_TILE, D_MODEL), lambda b, t: (b, t, 0)),
        pl.BlockSpec((1, 1, 3 * D_MODEL), _mod_index),
        pl.BlockSpec((1, D_MODEL), lambda b, t: (0, 0)),
        pl.BlockSpec((D_MODEL, n_total), lambda b, t: (0, 0), pipeline_mode=pl.Buffered(1)),
    ]
    args = [x, mod, nw.reshape(1, D_MODEL), w]
    rope_groups = ()
    if rope is not None:
        cos, sin, rope_groups = rope
        in_specs += [pl.BlockSpec((ROW_TILE, 128), lambda b, t: (t, 0))] * 2
        args += [cos, sin]
    out_specs = [pl.BlockSpec((1, ROW_TILE, width), lambda b, t: (b, t, 0)) for _, width in segs]
    bsz = x.shape[0]
    out_shape = [jax.ShapeDtypeStruct((bsz, L_ALL, width), dt) for (_, width), dt in zip(segs, out_dtypes)]
    return pl.pallas_call(
        functools.partial(_in_proj_body, segs=tuple(segs), rope_groups=tuple(rope_groups)),
        grid=(bsz, N_ROW_TILES),
        in_specs=in_specs,
        out_specs=out_specs,
        out_shape=out_shape,
        compiler_params=_cparams(("parallel", "parallel")),
    )(*args)


def _out_proj_body(o_ref, z_ref, x_ref, mod_ref, nw_ref, w_ref, out_ref):
    g = (o_ref[0].astype(F32) * _silu(z_ref[0].astype(F32))).astype(BF16)
    y = _dot(g, w_ref[...])
    yn = y * lax.rsqrt(jnp.mean(y * y, axis=-1, keepdims=True) + EPS) * nw_ref[...]
    gate = mod_ref[0, :, 2 * D_MODEL:3 * D_MODEL]
    out_ref[0] = x_ref[0] + gate * yn


def _out_proj(o, z_arr, z_block, x, mod, nw, w, latent_only=False):
    width = o.shape[-1]
    bsz = o.shape[0]
    skip = CTX_LEN // ROW_TILE if latent_only else 0
    return pl.pallas_call(
        _out_proj_body,
        grid=(bsz, N_ROW_TILES - skip),
        in_specs=[
            pl.BlockSpec((1, ROW_TILE, width), lambda b, t: (b, t + skip, 0)),
            pl.BlockSpec((1, ROW_TILE, width), lambda b, t: (b, t + skip, z_block)),
            pl.BlockSpec((1, ROW_TILE, D_MODEL), lambda b, t: (b, t + skip, 0)),
            pl.BlockSpec((1, 1, 3 * D_MODEL), lambda b, t: _mod_index(b, t + skip)),
            pl.BlockSpec((1, D_MODEL), lambda b, t: (0, 0)),
            pl.BlockSpec((width, D_MODEL), lambda b, t: (0, 0), pipeline_mode=pl.Buffered(1)),
        ],
        out_specs=pl.BlockSpec((1, ROW_TILE, D_MODEL), lambda b, t: (b, t, 0)),
        out_shape=jax.ShapeDtypeStruct((bsz, L_ALL - skip * ROW_TILE, D_MODEL), F32),
        compiler_params=_cparams(("parallel", "parallel")),
    )(o, z_arr, x, mod, nw.reshape(1, D_MODEL), w)


def _conv_rows(ref, r0, rows, width_sl, cw, left_ok, right_ok):
    total = ref.shape[1]
    halo = 2 * SUB
    main = ref[0, pl.ds(r0, rows), width_sl].astype(F32)
    prev = ref[0, pl.ds(pl.multiple_of(jnp.maximum(r0 - halo, 0), halo), halo), width_sl].astype(F32) * left_ok
    nxt = ref[0, pl.ds(pl.multiple_of(jnp.minimum(r0 + rows, total - halo), halo), halo),
              width_sl].astype(F32) * right_ok
    xw = jnp.concatenate([prev, main, nxt], axis=0)
    return (cw[0:1] * xw[halo - 2:halo - 2 + rows] + cw[1:2] * xw[halo - 1:halo - 1 + rows]
            + cw[2:3] * xw[halo:halo + rows] + cw[3:4] * xw[halo + 1:halo + 1 + rows])


def _gdn_consts():
    i = np.arange(PAIR)
    same = (i[:, None] // CHUNK) == (i[None, :] // CHUNK)
    t_i, t_m = i[:, None] % CHUNK, i[None, :] % CHUNK
    low = same & (t_m <= t_i)
    upp = same & (t_m >= t_i)
    both = lambda f, b: np.concatenate([f, b], axis=1)
    eye = np.eye(PAIR, dtype=np.float32)
    mats = np.stack([
        both(np.where(low, 0.0, NEG), np.where(upp, 0.0, NEG)),
        both(same & (t_m < t_i), same & (t_m > t_i)).astype(np.float32),
        both(eye, eye)]).astype(np.float32)
    levels = [(i[:, None] // INV_LEAF) == (i[None, :] // INV_LEAF)]
    size = INV_LEAF
    while size < CHUNK:
        levels.append(((i[:, None] // (2 * size)) == (i[None, :] // (2 * size)))
                      & ((i[:, None] // size) != (i[None, :] // size)))
        size *= 2
    lvl = np.stack([both(m, m) for m in levels]).astype(np.float32)
    return jnp.asarray(mats, F32), jnp.asarray(lvl, BF16)


def _block_diag(x):
    zero = jnp.zeros((PAIR, PAIR), x.dtype)
    return jnp.concatenate([jnp.concatenate([x[:, :PAIR], zero], axis=1),
                            jnp.concatenate([zero, x[:, PAIR:]], axis=1)], axis=0)


def _take_groups(x, keep):
    return jnp.concatenate([x[g * SUB:(g + 1) * SUB] for g, k in enumerate(keep) if k], axis=0)


def _put_groups(y, keep, minuend=None):
    out, j = [], 0
    for g, k in enumerate(keep):
        base = None if minuend is None else minuend[g * SUB:(g + 1) * SUB]
        if k:
            piece = y[j * SUB:(j + 1) * SUB]
            out.append(piece if base is None else base - piece)
            j += 1
        else:
            out.append(jnp.zeros((SUB, y.shape[1]), y.dtype) if base is None else base)
    return jnp.concatenate(out, axis=0)


def _gdn_body(q_ref, k_ref, v_ref, cwq_ref, cwk_ref, cwv_ref, abr_ref, pr_ref,
              mats_ref, lvl_ref, gn_ref, o_ref,
              pad_s, lhs_s, c_s, o0_s, dec_s, st_s, acc_s, *stage_refs):
    pad_s[:, 0:SUB, :] = jnp.zeros((4, SUB, GDN_HD), F32)
    pad_s[:, SUB + CTX_LEN:2 * SUB + CTX_LEN, :] = jnp.zeros((4, SUB, GDN_HD), F32)
    pad_s[:, 2 * SUB + L_ALL:3 * SUB + L_ALL, :] = jnp.zeros((4, SUB, GDN_HD), F32)

    def fill(t, carry):
        src = pl.multiple_of(t * ROW_TILE, ROW_TILE)
        dst = pl.multiple_of(src + jnp.where(t == 0, SUB, 2 * SUB), SUB)
        pad_s[0, pl.ds(dst, ROW_TILE), :] = q_ref[0, pl.ds(src, ROW_TILE), :].astype(F32)
        pad_s[1, pl.ds(dst, ROW_TILE), :] = k_ref[0, pl.ds(src, ROW_TILE), :].astype(F32)
        pad_s[2, pl.ds(dst, ROW_TILE), :] = v_ref[0, pl.ds(src, ROW_TILE), 0:GDN_HD].astype(F32)
        pad_s[3, pl.ds(dst, ROW_TILE), :] = v_ref[0, pl.ds(src, ROW_TILE), GDN_HD:].astype(F32)
        return carry

    lax.fori_loop(0, N_ROW_TILES, fill, 0)
    stage_a, stage_b = stage_refs[:5], stage_refs[5:]
    t_lane = lax.broadcasted_iota(jnp.int32, (A_GROUP * SUB, PAIR), 1)
    t_in_chunk = t_lane % CHUNK

    def conv(which, base, cw):
        taps = [pad_s[which, pl.ds(base + (k - 2), CHUNK), :] for k in range(4)]
        return _silu(cw[0:1] * taps[0] + cw[1:2] * taps[1] + cw[2:3] * taps[2] + cw[3:4] * taps[3])

    def lanes2(f, b):
        return jnp.concatenate([jnp.broadcast_to(f, (PAIR, PAIR)), jnp.broadcast_to(b, (PAIR, PAIR))], axis=1)

    def group_gates(first, n):
        zero = jnp.zeros((SUB - 4, PAIR), F32)
        rows = [x for g in range(n) for x in (abr_ref[0, 0, first + g], zero)]
        rows += [jnp.zeros((SUB, PAIR), F32)] * (A_GROUP - n)
        xr = jnp.concatenate(rows, axis=0)
        g_r = -jnp.exp(jnp.tile(pr_ref[0, 0], (A_GROUP, 1))) * _softplus(xr + jnp.tile(pr_ref[0, 1], (A_GROUP, 1)))
        beta_r = jax.nn.sigmoid(xr)
        lane, t_chunk = t_lane, t_in_chunk
        cum_f, cum_b = g_r, g_r
        step = 1
        while step < CHUNK:
            cum_f = cum_f + jnp.where(t_chunk >= step, pltpu.roll(cum_f, step, 1), 0.0)
            cum_b = cum_b + jnp.where(t_chunk < CHUNK - step, pltpu.roll(cum_b, PAIR - step, 1), 0.0)
            step *= 2
        tot = cum_f + cum_b - g_r
        other = pltpu.roll(tot, CHUNK, 1)
        tot_h0 = jnp.where(lane < CHUNK, tot, other)
        tot_h1 = jnp.where(lane < CHUNK, other, tot)
        pad = jnp.zeros((PAIR - A_GROUP * SUB, PAIR), F32)
        col = lambda x: jnp.concatenate([x, pad], axis=0).T
        return dict(cum_f=cum_f, cum_b=cum_b, cum_f_t=col(cum_f), cum_b_t=col(cum_b), tot_t=col(tot),
                    beta_t=col(beta_r), dec=jnp.exp(jnp.concatenate([tot_h0, tot_h1], axis=1)))

    def chunk_start(c, g, gt, stage):
        base = c * CHUNK + (SUB if c < CTX_CHUNKS else 2 * SUB)
        cwv = cwv_ref[...]
        qc = conv(0, base, cwq_ref[...])
        kc = conv(1, base, cwk_ref[...])
        v2 = jnp.concatenate([conv(2, base, cwv[:, :GDN_HD]), conv(3, base, cwv[:, GDN_HD:])], axis=0)
        qn = qc * (lax.rsqrt(jnp.sum(qc * qc, axis=-1, keepdims=True) + 1e-6) * (GDN_HD ** -0.5))
        kn = kc * lax.rsqrt(jnp.sum(kc * kc, axis=-1, keepdims=True) + 1e-6)
        q2 = jnp.concatenate([qn, qn], axis=0)
        k2 = jnp.concatenate([kn, kn], axis=0)
        gram = _dot_nt(jnp.concatenate([kn, qn], axis=0).astype(BF16), k2.astype(BF16))
        kk = jnp.concatenate([gram[:CHUNK], gram[:CHUNK]], axis=0)
        qk = jnp.concatenate([gram[CHUNK:], gram[CHUNK:]], axis=0)
        r = g * SUB
        gc_f, gc_b = gt["cum_f"][r:r + 1], gt["cum_b"][r + 2:r + 3]
        gcc_d = [gt["cum_f_t"][:, r:r + 1], gt["cum_b_t"][:, r + 2:r + 3]]
        totc_d = [gt["tot_t"][:, r:r + 1], gt["tot_t"][:, r + 2:r + 3]]
        bc_d = [gt["beta_t"][:, r + 1:r + 2], gt["beta_t"][:, r + 3:r + 4]]
        m = jnp.exp(lanes2(gcc_d[0], gcc_d[1]) - jnp.concatenate([gc_f, gc_b], axis=1) + mats_ref[0])
        a = (jnp.concatenate([kk, kk], axis=1) * m * mats_ref[1]
             * lanes2(bc_d[0], bc_d[1])).astype(BF16)
        qkm = (jnp.concatenate([qk, qk], axis=1) * m).astype(BF16)
        sa, srhs, sqd, skd, sqkm = stage
        sa[g] = a
        for d in range(2):
            gcc, totc, bc = gcc_d[d], totc_d[d], bc_d[d]
            egc = jnp.exp(gcc)
            dec_s[d * N_CHUNKS + c] = jnp.broadcast_to(gt["dec"][r + 2 * d:r + 2 * d + 1], (SUB, 2 * GDN_HD))
            srhs[2 * g + d] = jnp.concatenate([v2 * bc, k2 * (bc * egc)], axis=1).astype(BF16)
            sqd[2 * g + d] = q2 * egc
            skd[2 * g + d] = (k2 * jnp.exp(totc - gcc)).astype(BF16)
            sqkm[2 * g + d] = qkm[:, d * PAIR:(d + 1) * PAIR]

    def prepare(group, stage):
        first, n = group
        gt = group_gates(first, n)
        for g in range(n):
            chunk_start(first + g, g, gt, stage)

    def mm(p, q):
        return _dot(p.astype(BF16), _block_diag(q.astype(BF16)))

    def solve(group, stage, steps):
        sa, srhs, sqd, skd, sqkm = stage
        steps = list(steps)
        first, n_chunks = group

        def tick():
            if steps:
                recur(steps.pop(0))

        chunks = [dict(a=sa[g], dirs=[dict(idx=d * N_CHUNKS + first + g, rhs=srhs[2 * g + d],
                                           qd=sqd[2 * g + d], kd=skd[2 * g + d], qkm=sqkm[2 * g + d])
                                      for d in range(2)]) for g in range(n_chunks)]
        n_levels = lvl_ref.shape[0]
        for ch in chunks:
            leaf = ch["a"] * lvl_ref[0]
            ch["x"] = mats_ref[2] - leaf.astype(F32)
            ch["p"] = mm(leaf, leaf)
        tick()
        for ch in chunks:
            xp = mm(jnp.concatenate([ch["x"], ch["p"]], axis=0), ch["p"])
            ch["x"] = ch["x"] + xp[:PAIR]
            ch["p"] = xp[PAIR:]
        tick()
        for ch in chunks:
            ch["x"] = ch["x"] + mm(ch["x"], ch["p"])
        tick()
        for lv in range(1, n_levels):
            size = INV_LEAF << (lv - 1)
            later = [(gi * SUB // size) % 2 == 1 for gi in range(PAIR // SUB)]
            keep = (later, [not k for k in later])

            def take(x):
                return jnp.concatenate([_take_groups(x[:, d * PAIR:(d + 1) * PAIR], keep[d]) for d in range(2)],
                                       axis=1)

            def put(y, minuend=None):
                return jnp.concatenate(
                    [_put_groups(y[:, d * PAIR:(d + 1) * PAIR], keep[d],
                                 None if minuend is None else minuend[:, d * PAIR:(d + 1) * PAIR])
                     for d in range(2)], axis=1)

            for ch in chunks:
                ch["y"] = mm(take((ch["a"] * lvl_ref[lv]).astype(F32)), ch["x"])
            tick()
            for ch in chunks:
                ch["x"] = put(mm(take(ch["x"]), put(ch["y"])), minuend=ch["x"])
            tick()
        chains = []
        for ch in chunks:
            xb = ch["x"].astype(BF16)
            for d, cd in enumerate(ch["dirs"]):
                cd["uw"] = _dot(xb[:, d * PAIR:(d + 1) * PAIR], cd["rhs"]).astype(BF16)
                chains.append(cd)
        tick()
        for cd in chains:
            wu = jnp.concatenate([cd["uw"][:, GDN_HD:], cd["uw"][:, :GDN_HD]], axis=1)
            cd["nc"] = [_dot_tn(cd["kd"][h * CHUNK:(h + 1) * CHUNK], wu[h * CHUNK:(h + 1) * CHUNK])
                        for h in range(2)]
            cd["qo"] = _dot(cd["qkm"], wu)
        for cd in chains:
            nc, qo, idx = cd["nc"], cd["qo"], cd["idx"]
            lhs_s[idx] = jnp.concatenate([-nc[0][:, :GDN_HD], -nc[1][:, :GDN_HD],
                                          cd["qd"] - qo[:, :GDN_HD]], axis=0).astype(BF16)
            c_s[idx] = jnp.concatenate([nc[0][:, GDN_HD:], nc[1][:, GDN_HD:]], axis=1)
            o0_s[idx] = qo[:, GDN_HD:].astype(o0_s.dtype)
        while steps:
            tick()

    def recur(step):
        chains = [dict(d=d, c=c, idx=d * N_CHUNKS + c) for d, c in step]
        for ch in chains:
            ch["st"] = st_s[ch["d"]]
            ch["r"] = _dot(lhs_s[ch["idx"]], ch["st"].astype(BF16))
        for ch in chains:
            r, idx = ch["r"], ch["idx"]
            ns = jnp.concatenate([r[0:GDN_HD, :GDN_HD], r[GDN_HD:2 * GDN_HD, GDN_HD:]], axis=1)
            st_s[ch["d"]] = ch["st"] * dec_s[idx][0:1] + ns + c_s[idx]
            o0 = o0_s[idx]
            r0 = ch["c"] * CHUNK
            acc_s[r0:r0 + CHUNK, 0:GDN_HD] += r[2 * GDN_HD:2 * GDN_HD + CHUNK, :GDN_HD] + o0[:CHUNK]
            acc_s[r0:r0 + CHUNK, GDN_HD:] += r[2 * GDN_HD + CHUNK:, GDN_HD:] + o0[CHUNK:]

    st_s[...] = jnp.zeros_like(st_s)
    acc_s[...] = jnp.zeros_like(acc_s)

    order_f = list(range(N_CHUNKS))
    order_b = list(range(CTX_CHUNKS - 1, -1, -1)) + list(range(N_CHUNKS - 1, CTX_CHUNKS - 1, -1))
    group_order, n_groups = GDN_GROUPS, len(GDN_GROUPS)
    stages = (stage_a, stage_b)
    solved, pos = set(), [0, 0]

    def ready_steps():
        out = []
        while True:
            step = [(d, order[pos[d]]) for d, order in enumerate((order_f, order_b))
                    if pos[d] < N_CHUNKS and order[pos[d]] in solved]
            if not step:
                return out
            for d, _ in step:
                pos[d] += 1
            out.append(step)

    prepare(group_order[0], stages[0])
    for n, grp in enumerate(group_order):
        if n + 1 < n_groups:
            prepare(group_order[n + 1], stages[(n + 1) % 2])
        solve(grp, stages[n % 2], ready_steps())
        solved.update(range(grp[0], grp[0] + grp[1]))
    for step in ready_steps():
        recur(step)

    def finish(t, carry):
        r0 = pl.multiple_of(t * ROW_TILE, ROW_TILE)
        for h in range(2):
            sl = slice(h * GDN_HD, (h + 1) * GDN_HD)
            x = acc_s[pl.ds(r0, ROW_TILE), sl]
            o_ref[0, pl.ds(r0, ROW_TILE), sl] = (
                x * lax.rsqrt(jnp.mean(x * x, axis=-1, keepdims=True) + EPS) * gn_ref[...]).astype(o_ref.dtype)
        return carry

    lax.fori_loop(0, N_ROW_TILES, finish, 0)


def _gdn_mixer(p_main, p_ab, conv_w, a_log, dt_bias, g_norm):
    mats, lvl = _gdn_consts()
    bsz = p_main.shape[0]
    ab = p_ab.reshape(bsz, N_CHUNKS, CHUNK, 2, 2, GDN_QK_HEADS, 2)
    abr = ab.transpose(0, 5, 1, 4, 3, 6, 2).reshape(bsz, GDN_QK_HEADS, N_CHUNKS, 4, PAIR)
    prm = jnp.stack([a_log, dt_bias], axis=0).astype(F32).reshape(2, 2, GDN_QK_HEADS, 2)
    zero = jnp.zeros_like(prm[:, 0])
    rows = jnp.stack([prm[:, 0], prm[:, 0], prm[:, 1], prm[:, 1]] + [zero] * (SUB - 4), axis=1)
    pr = jnp.repeat(rows.transpose(2, 0, 1, 3), CHUNK, axis=3)

    grid = (bsz, GDN_QK_HEADS)
    kq = GDN_KEY_DIM // GDN_HD
    in_specs = [
        pl.BlockSpec((1, L_ALL, GDN_HD), lambda b, j: (b, 0, j)),
        pl.BlockSpec((1, L_ALL, GDN_HD), lambda b, j: (b, 0, kq + j)),
        pl.BlockSpec((1, L_ALL, 2 * GDN_HD), lambda b, j: (b, 0, kq + j)),
        pl.BlockSpec((4, GDN_HD), lambda b, j: (0, j)),
        pl.BlockSpec((4, GDN_HD), lambda b, j: (0, kq + j)),
        pl.BlockSpec((4, 2 * GDN_HD), lambda b, j: (0, kq + j)),
        pl.BlockSpec((1, 1, N_CHUNKS, 4, PAIR), lambda b, j: (b, j, 0, 0, 0)),
        pl.BlockSpec((1, 2, SUB, PAIR), lambda b, j: (j, 0, 0, 0)),
        pl.BlockSpec(mats.shape, lambda b, j: (0, 0, 0)),
        pl.BlockSpec(lvl.shape, lambda b, j: (0, 0, 0)),
        pl.BlockSpec((1, GDN_HD), lambda b, j: (0, 0)),
    ]
    return pl.pallas_call(
        _gdn_body,
        grid=grid,
        in_specs=in_specs,
        out_specs=pl.BlockSpec((1, L_ALL, 2 * GDN_HD), lambda b, j: (b, 0, j)),
        out_shape=jax.ShapeDtypeStruct((bsz, L_ALL, GDN_VAL_DIM), ACT),
        scratch_shapes=[
            pltpu.VMEM((4, L_ALL + 3 * SUB, GDN_HD), F32),
            pltpu.VMEM((2 * N_CHUNKS, 3 * GDN_HD, GDN_HD), BF16),
            pltpu.VMEM((2 * N_CHUNKS, GDN_HD, 2 * GDN_HD), F32),
            pltpu.VMEM((2 * N_CHUNKS, PAIR, GDN_HD), BF16),
            pltpu.VMEM((2 * N_CHUNKS, SUB, 2 * GDN_HD), F32),
            pltpu.VMEM((2, GDN_HD, 2 * GDN_HD), F32),
            pltpu.VMEM((L_ALL, 2 * GDN_HD), F32),
        ] + 2 * [
            pltpu.VMEM((A_GROUP, PAIR, 2 * PAIR), BF16),
            pltpu.VMEM((2 * A_GROUP, PAIR, 2 * GDN_HD), BF16),
            pltpu.VMEM((2 * A_GROUP, PAIR, GDN_HD), F32),
            pltpu.VMEM((2 * A_GROUP, PAIR, GDN_HD), BF16),
            pltpu.VMEM((2 * A_GROUP, PAIR, PAIR), BF16),
        ],
        compiler_params=_cparams(("parallel", "parallel")),
    )(p_main, p_main, p_main, conv_w, conv_w, conv_w, abr, pr, mats, lvl,
      g_norm.reshape(1, GDN_HD))


def _lru_body(u_ref, cw_ref, cb_ref, wra_ref, wri_ref, bra_ref, bri_ref, lam_ref, o_ref, a_s, b_s, acc_s):
    full = slice(None)
    n_steps = L_ALL // LRU_ROWS

    def gates(c, carry):
        r0 = pl.multiple_of(c * LRU_ROWS, LRU_ROWS)
        left_ok = jnp.where(c >= 2, 1.0, 0.0)
        right_ok = jnp.where((c == 0) | (c == n_steps - 1), 0.0, 1.0)
        u = _conv_rows(u_ref, r0, LRU_ROWS, full, cw_ref[...], left_ok, right_ok) + cb_ref[...]
        ub = u.astype(BF16)
        blocks = [slice(n * LRU_BLOCK, (n + 1) * LRU_BLOCK) for n in range(LRU_LANES // LRU_BLOCK)]
        for d in range(2):
            r = jax.nn.sigmoid(jnp.concatenate([_dot(ub[:, sl], wra_ref[d, n]) for n, sl in enumerate(blocks)],
                                               axis=1) + bra_ref[d:d + 1])
            i = jax.nn.sigmoid(jnp.concatenate([_dot(ub[:, sl], wri_ref[d, n]) for n, sl in enumerate(blocks)],
                                               axis=1) + bri_ref[d:d + 1])
            log_a = -LRU_C * r * _softplus(-lam_ref[d:d + 1])
            a = jnp.exp(log_a)
            one_m_a2 = -jnp.tanh(log_a) * (a * a + 1.0)
            a_s[d, pl.ds(r0, LRU_ROWS), :] = a
            b_s[d, pl.ds(r0, LRU_ROWS), :] = jnp.sqrt(one_m_a2) * (i * u)
        return carry

    lax.fori_loop(0, n_steps, gates, 0)

    acc_s[...] = jnp.zeros_like(acc_s)
    row = lax.broadcasted_iota(jnp.int32, (SUB, LRU_LANES), 0)

    def scan(g, carry):
        hf, hb = carry
        gb = jnp.where(g < CTX_GROUPS, CTX_GROUPS - 1 - g, N_GROUPS + CTX_GROUPS - 1 - g)
        r0 = pl.multiple_of(g * SUB, SUB)
        a, b = a_s[0, pl.ds(r0, SUB), :], b_s[0, pl.ds(r0, SUB), :]
        for s in (1, 2, 4):
            ok = row >= s
            b = jnp.where(ok, a * pltpu.roll(b, s, 0) + b, b)
            a = jnp.where(ok, a * pltpu.roll(a, s, 0), a)
        h = b + a * hf
        acc_s[pl.ds(r0, SUB), :] += h
        hf = h[SUB - 1:SUB]
        r0 = pl.multiple_of(gb * SUB, SUB)
        a, b = a_s[1, pl.ds(r0, SUB), :], b_s[1, pl.ds(r0, SUB), :]
        for s in (1, 2, 4):
            ok = row < SUB - s
            b = jnp.where(ok, a * pltpu.roll(b, SUB - s, 0) + b, b)
            a = jnp.where(ok, a * pltpu.roll(a, SUB - s, 0), a)
        h = b + a * hb
        acc_s[pl.ds(r0, SUB), :] += h
        hb = h[0:1]
        return hf, hb

    zero = jnp.zeros((1, LRU_LANES), F32)
    lax.fori_loop(0, N_GROUPS, scan, (zero, zero), unroll=4)

    def emit(t, carry):
        r0 = pl.multiple_of(t * LRU_ROWS, LRU_ROWS)
        o_ref[0, pl.ds(r0, LRU_ROWS), :] = acc_s[pl.ds(r0, LRU_ROWS), :].astype(o_ref.dtype)
        return carry

    lax.fori_loop(0, n_steps, emit, 0)


def _lru_mixer(p, conv_w, conv_b, w_ra, w_ri, b_ra, b_ri, lam):
    nb = LRU_WIDTH // LRU_LANES
    per = LRU_LANES // LRU_BLOCK
    bsz = p.shape[0]
    vec = lambda rows: pl.BlockSpec((rows, LRU_LANES), lambda b, n: (0, n))
    wspec = pl.BlockSpec((2, per, LRU_BLOCK, LRU_BLOCK), lambda b, n: (0, n, 0, 0))
    return pl.pallas_call(
        _lru_body,
        grid=(bsz, nb),
        in_specs=[
            pl.BlockSpec((1, L_ALL, LRU_LANES), lambda b, n: (b, 0, nb + n)),
            vec(4), vec(1), wspec, wspec, vec(2), vec(2), vec(2),
        ],
        out_specs=pl.BlockSpec((1, L_ALL, LRU_LANES), lambda b, n: (b, 0, n)),
        out_shape=jax.ShapeDtypeStruct((bsz, L_ALL, LRU_WIDTH), ACT),
        scratch_shapes=[pltpu.VMEM((2, L_ALL, LRU_LANES), F32), pltpu.VMEM((2, L_ALL, LRU_LANES), F32),
                        pltpu.VMEM((L_ALL, LRU_LANES), F32)],
        compiler_params=_cparams(("parallel", "parallel")),
    )(p, conv_w, conv_b.reshape(1, LRU_WIDTH), w_ra.astype(BF16), w_ri.astype(BF16), b_ra, b_ri, lam)


def _att_body(sink_ref, q_ref, kc_ref, vc_ref, k0_ref, k1_ref, k2_ref, v0_ref, v1_ref, v2_ref, o_ref):
    i = pl.program_id(1)
    n_ctx_blocks = CTX_LEN // ATT_BLOCK
    li = i - n_ctx_blocks
    qpos = li * ATT_BLOCK + lax.broadcasted_iota(jnp.int32, (ATT_BLOCK, 3 * ATT_BLOCK), 0)
    kpos = (li - 1) * ATT_BLOCK + lax.broadcasted_iota(jnp.int32, (ATT_BLOCK, 3 * ATT_BLOCK), 1)
    band = (jnp.abs(qpos - kpos) <= ATT_WINDOW) & (kpos >= 0) & (kpos < SEQ) & (li >= 0)
    bias = jnp.concatenate([jnp.zeros((ATT_BLOCK, CTX_LEN), F32), jnp.where(band, 0.0, NEG)], axis=1)
    bias4 = jnp.concatenate([bias] * ATT_GROUP, axis=0)
    lane = lax.broadcasted_iota(jnp.int32, (ATT_BLOCK, ATT_QW), 1)
    qmask = [(lane % 128) // 32 == g for g in range(ATT_GROUP)]
    omask = [lane // ATT_HD == g for g in range(ATT_GROUP)]
    rowg = lax.broadcasted_iota(jnp.int32, (ATT_GROUP * ATT_BLOCK, 1), 0) // ATT_BLOCK
    def scores(h):
        sl = slice(h * ATT_QW, (h + 1) * ATT_QW)
        qh = q_ref[0, :, sl]
        qm = jnp.concatenate([jnp.where(qmask[g], qh, 0.0) for g in range(ATT_GROUP)], axis=0).astype(BF16)
        keys = jnp.concatenate([kc_ref[0, :, sl], k0_ref[0, :, sl], k1_ref[0, :, sl], k2_ref[0, :, sl]],
                               axis=0).astype(BF16)
        return _dot_nt(qm, keys)

    s_next = scores(0)
    for h in range(ATT_KV_HEADS):
        sl = slice(h * ATT_QW, (h + 1) * ATT_QW)
        s = s_next + bias4
        if h + 1 < ATT_KV_HEADS:
            s_next = scores(h + 1)
        vals = jnp.concatenate([vc_ref[0, :, sl], v0_ref[0, :, sl], v1_ref[0, :, sl], v2_ref[0, :, sl]],
                               axis=0).astype(BF16)
        sink = jnp.zeros((ATT_GROUP * ATT_BLOCK, 1), F32)
        for g in range(ATT_GROUP):
            sink = jnp.where(rowg == g, sink_ref[h * ATT_GROUP + g], sink)
        mx = jnp.maximum(jnp.max(s, axis=-1, keepdims=True), sink)
        e = jnp.exp(s - mx)
        den = jnp.sum(e, axis=-1, keepdims=True) + jnp.exp(sink - mx)
        r = _dot(e.astype(BF16), vals) / den
        out = jnp.zeros((ATT_BLOCK, ATT_QW), F32)
        for g in range(ATT_GROUP):
            out = jnp.where(omask[g], r[g * ATT_BLOCK:(g + 1) * ATT_BLOCK], out)
        o_ref[0, :, sl] = out.astype(o_ref.dtype)


def _att_mixer(p, sinks):
    n_ctx_blocks = CTX_LEN // ATT_BLOCK
    n_lat_blocks = SEQ // ATT_BLOCK
    bsz = p.shape[0]

    def win(off, col):
        def index(b, i, sink_ref):
            blk = jnp.clip(i - n_ctx_blocks + off, 0, n_lat_blocks - 1)
            return (b, n_ctx_blocks + blk, col)
        return pl.BlockSpec((1, ATT_BLOCK, D_MODEL), index)

    grid_spec = pltpu.PrefetchScalarGridSpec(
        num_scalar_prefetch=1,
        grid=(bsz, L_ALL // ATT_BLOCK),
        in_specs=[
            pl.BlockSpec((1, ATT_BLOCK, D_MODEL), lambda b, i, s: (b, i, 0)),
            pl.BlockSpec((1, CTX_LEN, D_MODEL), lambda b, i, s: (b, 0, 2)),
            pl.BlockSpec((1, CTX_LEN, D_MODEL), lambda b, i, s: (b, 0, 3)),
            win(-1, 2), win(0, 2), win(1, 2), win(-1, 3), win(0, 3), win(1, 3),
        ],
        out_specs=pl.BlockSpec((1, ATT_BLOCK, D_MODEL), lambda b, i, s: (b, i, 0)),
    )
    return pl.pallas_call(
        _att_body,
        grid_spec=grid_spec,
        out_shape=jax.ShapeDtypeStruct((bsz, L_ALL, D_MODEL), ACT),
        compiler_params=_cparams(("parallel", "arbitrary")),
    )(sinks.astype(F32), p, p, p, p, p, p, p, p, p)


def _att_weight_layout(w_in):
    half = ATT_HD // 2
    q_cols, k_cols, v_cols = [], [], []
    for h in range(ATT_KV_HEADS):
        for part in range(2):
            for g in range(ATT_GROUP):
                base = (h * ATT_GROUP + g) * ATT_HD + part * half
                q_cols += list(range(base, base + half))
                kb = D_MODEL + h * ATT_HD + part * half
                k_cols += list(range(kb, kb + half))
        vb = D_MODEL + ATT_KV_HEADS * ATT_HD + h * ATT_HD
        v_cols += list(range(vb, vb + ATT_HD)) * ATT_GROUP
    z0 = D_MODEL + 2 * ATT_KV_HEADS * ATT_HD
    z_cols = list(range(z0, z0 + D_MODEL))
    order = np.asarray(q_cols + z_cols + k_cols + v_cols, np.int32)
    scale = np.ones((ATT_COLS,), np.float32)
    scale[:D_MODEL] = ATT_HD ** -0.5
    return (w_in[:, order] * scale).astype(BF16)


def _rope_tables():
    rows = SEQ // GRID_W
    row = jnp.repeat(jnp.arange(rows), GRID_W)
    col = jnp.tile(jnp.arange(GRID_W), rows)
    n_freq = ATT_HD // 4
    inv = 10000.0 ** (-jnp.arange(n_freq, dtype=F32) / n_freq)
    ang = jnp.concatenate([row[:, None] * inv, col[:, None] * inv], axis=-1)
    cos = jnp.concatenate([jnp.ones((CTX_LEN, ATT_HD // 2), F32), jnp.cos(ang)], axis=0)
    sin = jnp.concatenate([jnp.zeros((CTX_LEN, ATT_HD // 2), F32), jnp.sin(ang)], axis=0)
    return jnp.tile(cos, (1, ATT_GROUP)), jnp.tile(sin, (1, ATT_GROUP))


def kernel(x, c, ctx, c_ctx, ada_w, ada_b, norm_pre, norm_post, gdn_w_in, gdn_conv_w, gdn_a_log, gdn_dt_bias, gdn_g_norm, gdn_w_out, lru_w_in, lru_conv_w, lru_conv_b, lru_w_ra, lru_b_ra, lru_w_ri, lru_b_ri, lru_lam, lru_w_out, att_w_in, att_sinks, att_w_out):
    cc = jnp.zeros((MOD_ROWS, D_MODEL), F32).at[:c.shape[0]].set(c).at[CTX_MOD_ROW].set(c_ctx)
    mod_all = _modulation(cc, ada_w, ada_b).reshape(DEPTH, MOD_ROWS, 1, 3 * D_MODEL)
    xs = jnp.concatenate([ctx, x], axis=1)
    cos, sin = _rope_tables()
    for i in range(DEPTH):
        kind, j = i % 3, i // 3
        mod = mod_all[i]
        last = i == DEPTH - 1
        if kind == 0:
            p_main, p_ab = _in_proj(xs, mod, norm_pre[i], gdn_w_in[j].astype(BF16),
                                    [(0, GDN_MAIN), (GDN_MAIN, GDN_AB)], [ACT, F32])
            o = _gdn_mixer(p_main, p_ab, gdn_conv_w[j], gdn_a_log[j], gdn_dt_bias[j], gdn_g_norm[j])
            xs = _out_proj(o, p_main, 2, xs, mod, norm_post[i], gdn_w_out[j].astype(BF16), latent_only=last)
        elif kind == 1:
            (p,) = _in_proj(xs, mod, norm_pre[i], lru_w_in[j].astype(BF16), [(0, 2 * LRU_WIDTH)], [ACT])
            o = _lru_mixer(p, lru_conv_w[j], lru_conv_b[j], lru_w_ra[j], lru_w_ri[j],
                           lru_b_ra[j], lru_b_ri[j], lru_lam[j])
            xs = _out_proj(o, p, 0, xs, mod, norm_post[i], lru_w_out[j].astype(BF16), latent_only=last)
        else:
            (p,) = _in_proj(xs, mod, norm_pre[i], _att_weight_layout(att_w_in[j]), [(0, ATT_COLS)], [ACT],
                            rope=(cos, sin, (0, 1, 4, 5)))
            o = _att_mixer(p, att_sinks[j])
            xs = _out_proj(o, p, 1, xs, mod, norm_post[i], att_w_out[j].astype(BF16), latent_only=last)
    return xs
```

```python
import functools

import numpy as np
import jax
import jax.numpy as jnp
from jax import lax
from jax.experimental import pallas as pl
from jax.experimental.pallas import tpu as pltpu

F32 = jnp.float32
BF16 = jnp.bfloat16
ACT = jnp.bfloat16

D_MODEL = 1024
BATCH = 8
SEQ = 2048
CTX_LEN = 256
L_ALL = CTX_LEN + SEQ
DEPTH = 4
GRID_W = 64
EPS = 1e-6

ROW_TILE = 256
N_ROW_TILES = L_ALL // ROW_TILE
MOD_ROWS = 16
CTX_MOD_ROW = BATCH
VMEM_LIMIT = 56 * 1024 * 1024

GDN_HD = 128
GDN_QK_HEADS = 8
GDN_V_HEADS = 16
GDN_KEY_DIM = 1024
GDN_VAL_DIM = 2048
GDN_MAIN = 2 * GDN_KEY_DIM + 2 * GDN_VAL_DIM
GDN_AB = 64
CHUNK = 64
N_CHUNKS = L_ALL // CHUNK
CTX_CHUNKS = CTX_LEN // CHUNK
PAIR = 2 * CHUNK
GDN_GROUPS = ((0, 4), (28, 8), (4, 8), (22, 6), (12, 6), (18, 4))
A_GROUP = max(n for _, n in GDN_GROUPS)
assert sorted(c for f, n in GDN_GROUPS for c in range(f, f + n)) == list(range(N_CHUNKS))
INV_LEAF = 8

LRU_WIDTH = 1024
LRU_BLOCK = 128
LRU_LANES = 256
LRU_C = 8.0
LRU_ROWS = 256
SUB = 8
N_GROUPS = L_ALL // SUB
CTX_GROUPS = CTX_LEN // SUB

ATT_HD = 64
ATT_KV_HEADS = 4
ATT_GROUP = 4
ATT_BLOCK = 128
ATT_WINDOW = 128
ATT_QW = ATT_GROUP * ATT_HD
ATT_COLS = 4 * D_MODEL
NEG = -1e30


def _silu(x):
    return x * _sigmoid(x)


def _sigmoid(x):
    return 0.5 * jnp.tanh(0.5 * x) + 0.5


def _softplus(x):
    return jnp.maximum(x, 0.0) + jnp.log1p(jnp.exp(-jnp.abs(x)))


def _cparams(sem):
    return pltpu.CompilerParams(dimension_semantics=sem, vmem_limit_bytes=VMEM_LIMIT)


def _dot(a, b):
    return jnp.dot(a, b, preferred_element_type=F32)


def _dot_nt(a, b):
    return lax.dot_general(a, b, (((1,), (1,)), ((), ())), preferred_element_type=F32)


def _dot_tn(a, b):
    return lax.dot_general(a, b, (((0,), (0,)), ((), ())), preferred_element_type=F32)


def _mod_body(c_ref, w_ref, b_ref, o_ref):
    h = _silu(c_ref[...]).astype(BF16)
    o_ref[0] = _dot(h, w_ref[0].astype(BF16)) + b_ref[0]


def _modulation(cc, ada_w, ada_b):
    n_col = 3
    return pl.pallas_call(
        _mod_body,
        grid=(DEPTH, n_col),
        in_specs=[
            pl.BlockSpec((MOD_ROWS, D_MODEL), lambda i, n: (0, 0)),
            pl.BlockSpec((1, D_MODEL, D_MODEL), lambda i, n: (i, 0, n)),
            pl.BlockSpec((1, 1, D_MODEL), lambda i, n: (i, 0, n)),
        ],
        out_specs=pl.BlockSpec((1, MOD_ROWS, D_MODEL), lambda i, n: (i, 0, n)),
        out_shape=jax.ShapeDtypeStruct((DEPTH, MOD_ROWS, 3 * D_MODEL), F32),
        compiler_params=_cparams(("parallel", "parallel")),
    )(cc, ada_w, ada_b.reshape(DEPTH, 1, 3 * D_MODEL))


def _mod_index(b, t):
    return (jnp.where(t == 0, CTX_MOD_ROW, b), 0, 0)


def _in_proj_body(x_ref, mod_ref, nw_ref, w_ref, *refs, segs, rope_groups):
    if rope_groups:
        cos_ref, sin_ref = refs[:2]
        refs = refs[2:]
    x = x_ref[0]
    y = x * lax.rsqrt(jnp.mean(x * x, axis=-1, keepdims=True) + EPS) * nw_ref[...]
    shift = mod_ref[0, :, 0:D_MODEL]
    scale = mod_ref[0, :, D_MODEL:2 * D_MODEL]
    h = (y * (1.0 + scale) + shift).astype(BF16)
    for ref, (c0, width) in zip(refs, segs):
        step = min(width, 512)
        for n0 in range(0, width, step):
            acc = _dot(h, w_ref[:, c0 + n0:c0 + n0 + step])
            if (c0 + n0) // 512 in rope_groups:
                cos = cos_ref[...]
                sin = sin_ref[...]
                parts = []
                for g0 in range(0, step, 256):
                    x1 = acc[:, g0:g0 + 128]
                    x2 = acc[:, g0 + 128:g0 + 256]
                    parts += [x1 * cos - x2 * sin, x2 * cos + x1 * sin]
                acc = jnp.concatenate(parts, axis=1)
            ref[0, :, n0:n0 + step] = acc.astype(ref.dtype)


def _in_proj(x, mod, nw, w, segs, out_dtypes, rope=None):
    n_total = w.shape[1]
    in_specs = [
        pl.BlockSpec((1, ROW_TILE, D_MODEL), lambda b, t: (b, t, 0)),
        pl.BlockSpec((1, 1, 3 * D_MODEL), _mod_index),
        pl.BlockSpec((1, D_MODEL), lambda b, t: (0, 0)),
        pl.BlockSpec((D_MODEL, n_total), lambda b, t: (0, 0), pipeline_mode=pl.Buffered(1)),
    ]
    args = [x, mod, nw.reshape(1, D_MODEL), w]
    rope_groups = ()
    if rope is not None:
        cos, sin, rope_groups = rope
        in_specs += [pl.BlockSpec((ROW_TILE, 128), lambda b, t: (t, 0))] * 2
        args += [cos, sin]
    out_specs = [pl.BlockSpec((1, ROW_TILE, width), lambda b, t: (b, t, 0)) for _, width in segs]
    bsz = x.shape[0]
    out_shape = [jax.ShapeDtypeStruct((bsz, L_ALL, width), dt) for (_, width), dt in zip(segs, out_dtypes)]
    return pl.pallas_call(
        functools.partial(_in_proj_body, segs=tuple(segs), rope_groups=tuple(rope_groups)),
        grid=(bsz, N_ROW_TILES),
        in_specs=in_specs,
        out_specs=out_specs,
        out_shape=out_shape,
        compiler_params=_cparams(("parallel", "parallel")),
    )(*args)


def _out_proj_body(o_ref, z_ref, x_ref, mod_ref, nw_ref, w_ref, out_ref):
    g = (o_ref[0].astype(F32) * _silu(z_ref[0].astype(F32))).astype(BF16)
    y = _dot(g, w_ref[...])
    yn = y * lax.rsqrt(jnp.mean(y * y, axis=-1, keepdims=True) + EPS) * nw_ref[...]
    gate = mod_ref[0, :, 2 * D_MODEL:3 * D_MODEL]
    out_ref[0] = x_ref[0] + gate * yn


def _out_proj(o, z_arr, z_block, x, mod, nw, w, latent_only=False):
    width = o.shape[-1]
    bsz = o.shape[0]
    skip = CTX_LEN // ROW_TILE if latent_only else 0
    return pl.pallas_call(
        _out_proj_body,
        grid=(bsz, N_ROW_TILES - skip),
        in_specs=[
            pl.BlockSpec((1, ROW_TILE, width), lambda b, t: (b, t + skip, 0)),
            pl.BlockSpec((1, ROW_TILE, width), lambda b, t: (b, t + skip, z_block)),
            pl.BlockSpec((1, ROW_TILE, D_MODEL), lambda b, t: (b, t + skip, 0)),
            pl.BlockSpec((1, 1, 3 * D_MODEL), lambda b, t: _mod_index(b, t + skip)),
            pl.BlockSpec((1, D_MODEL), lambda b, t: (0, 0)),
            pl.BlockSpec((width, D_MODEL), lambda b, t: (0, 0), pipeline_mode=pl.Buffered(1)),
        ],
        out_specs=pl.BlockSpec((1, ROW_TILE, D_MODEL), lambda b, t: (b, t, 0)),
        out_shape=jax.ShapeDtypeStruct((bsz, L_ALL - skip * ROW_TILE, D_MODEL), F32),
        compiler_params=_cparams(("parallel", "parallel")),
    )(o, z_arr, x, mod, nw.reshape(1, D_MODEL), w)


def _conv_rows(ref, r0, rows, width_sl, cw, left_ok, right_ok):
    total = ref.shape[1]
    halo = 2 * SUB
    main = ref[0, pl.ds(r0, rows), width_sl].astype(F32)
    prev = ref[0, pl.ds(pl.multiple_of(jnp.maximum(r0 - halo, 0), halo), halo), width_sl].astype(F32) * left_ok
    nxt = ref[0, pl.ds(pl.multiple_of(jnp.minimum(r0 + rows, total - halo), halo), halo),
              width_sl].astype(F32) * right_ok
    xw = jnp.concatenate([prev, main, nxt], axis=0)
    return (cw[0:1] * xw[halo - 2:halo - 2 + rows] + cw[1:2] * xw[halo - 1:halo - 1 + rows]
            + cw[2:3] * xw[halo:halo + rows] + cw[3:4] * xw[halo + 1:halo + 1 + rows])


def _gdn_consts():
    i = np.arange(PAIR)
    same = (i[:, None] // CHUNK) == (i[None, :] // CHUNK)
    t_i, t_m = i[:, None] % CHUNK, i[None, :] % CHUNK
    low = same & (t_m <= t_i)
    upp = same & (t_m >= t_i)
    both = lambda f, b: np.concatenate([f, b], axis=1)
    eye = np.eye(PAIR, dtype=np.float32)
    mats = np.stack([
        both(np.where(low, 0.0, NEG), np.where(upp, 0.0, NEG)),
        both(same & (t_m < t_i), same & (t_m > t_i)).astype(np.float32),
        both(eye, eye)]).astype(np.float32)
    levels = [(i[:, None] // INV_LEAF) == (i[None, :] // INV_LEAF)]
    size = INV_LEAF
    while size < CHUNK:
        levels.append(((i[:, None] // (2 * size)) == (i[None, :] // (2 * size)))
                      & ((i[:, None] // size) != (i[None, :] // size)))
        size *= 2
    lvl = np.stack([both(m, m) for m in levels]).astype(np.float32)
    return jnp.asarray(mats, F32), jnp.asarray(lvl, BF16)


def _block_diag(x):
    zero = jnp.zeros((PAIR, PAIR), x.dtype)
    return jnp.concatenate([jnp.concatenate([x[:, :PAIR], zero], axis=1),
                            jnp.concatenate([zero, x[:, PAIR:]], axis=1)], axis=0)


def _take_groups(x, keep):
    return jnp.concatenate([x[g * SUB:(g + 1) * SUB] for g, k in enumerate(keep) if k], axis=0)


def _put_groups(y, keep, minuend=None):
    out, j = [], 0
    for g, k in enumerate(keep):
        base = None if minuend is None else minuend[g * SUB:(g + 1) * SUB]
        if k:
            piece = y[j * SUB:(j + 1) * SUB]
            out.append(piece if base is None else base - piece)
            j += 1
        else:
            out.append(jnp.zeros((SUB, y.shape[1]), y.dtype) if base is None else base)
    return jnp.concatenate(out, axis=0)


def _gdn_body(q_ref, k_ref, v_ref, cwq_ref, cwk_ref, cwv_ref, abr_ref, pr_ref,
              mats_ref, lvl_ref, gn_ref, o_ref,
              pad_s, lhs_s, c_s, o0_s, dec_s, st_s, acc_s, *stage_refs):
    pad_s[:, 0:SUB, :] = jnp.zeros((4, SUB, GDN_HD), F32)
    pad_s[:, SUB + CTX_LEN:2 * SUB + CTX_LEN, :] = jnp.zeros((4, SUB, GDN_HD), F32)
    pad_s[:, 2 * SUB + L_ALL:3 * SUB + L_ALL, :] = jnp.zeros((4, SUB, GDN_HD), F32)

    def fill(t, carry):
        src = pl.multiple_of(t * ROW_TILE, ROW_TILE)
        dst = pl.multiple_of(src + jnp.where(t == 0, SUB, 2 * SUB), SUB)
        pad_s[0, pl.ds(dst, ROW_TILE), :] = q_ref[0, pl.ds(src, ROW_TILE), :].astype(F32)
        pad_s[1, pl.ds(dst, ROW_TILE), :] = k_ref[0, pl.ds(src, ROW_TILE), :].astype(F32)
        pad_s[2, pl.ds(dst, ROW_TILE), :] = v_ref[0, pl.ds(src, ROW_TILE), 0:GDN_HD].astype(F32)
        pad_s[3, pl.ds(dst, ROW_TILE), :] = v_ref[0, pl.ds(src, ROW_TILE), GDN_HD:].astype(F32)
        return carry

    lax.fori_loop(0, N_ROW_TILES, fill, 0)
    stage_a, stage_b = stage_refs[:5], stage_refs[5:]
    t_lane = lax.broadcasted_iota(jnp.int32, (A_GROUP * SUB, PAIR), 1)
    t_in_chunk = t_lane % CHUNK

    def conv(which, base, cw):
        taps = [pad_s[which, pl.ds(base + (k - 2), CHUNK), :] for k in range(4)]
        return _silu(cw[0:1] * taps[0] + cw[1:2] * taps[1] + cw[2:3] * taps[2] + cw[3:4] * taps[3])

    def lanes2(f, b):
        return jnp.concatenate([jnp.broadcast_to(f, (PAIR, PAIR)), jnp.broadcast_to(b, (PAIR, PAIR))], axis=1)

    def group_gates(first, n):
        zero = jnp.zeros((SUB - 4, PAIR), F32)
        rows = [x for g in range(n) for x in (abr_ref[0, 0, first + g], zero)]
        rows += [jnp.zeros((SUB, PAIR), F32)] * (A_GROUP - n)
        xr = jnp.concatenate(rows, axis=0)
        g_r = -jnp.exp(jnp.tile(pr_ref[0, 0], (A_GROUP, 1))) * _softplus(xr + jnp.tile(pr_ref[0, 1], (A_GROUP, 1)))
        beta_r = _sigmoid(xr)
        lane, t_chunk = t_lane, t_in_chunk
        cum_f, cum_b = g_r, g_r
        step = 1
        while step < CHUNK:
            cum_f = cum_f + jnp.where(t_chunk >= step, pltpu.roll(cum_f, step, 1), 0.0)
            cum_b = cum_b + jnp.where(t_chunk < CHUNK - step, pltpu.roll(cum_b, PAIR - step, 1), 0.0)
            step *= 2
        tot = cum_f + cum_b - g_r
        other = pltpu.roll(tot, CHUNK, 1)
        tot_h0 = jnp.where(lane < CHUNK, tot, other)
        tot_h1 = jnp.where(lane < CHUNK, other, tot)
        pad = jnp.zeros((PAIR - A_GROUP * SUB, PAIR), F32)
        col = lambda x: jnp.concatenate([x, pad], axis=0).T
        return dict(cum_f=cum_f, cum_b=cum_b, cum_f_t=col(cum_f), cum_b_t=col(cum_b), tot_t=col(tot),
                    beta_t=col(beta_r), dec=jnp.exp(jnp.concatenate([tot_h0, tot_h1], axis=1)))

    def chunk_start(c, g, gt, stage):
        base = c * CHUNK + (SUB if c < CTX_CHUNKS else 2 * SUB)
        cwv = cwv_ref[...]
        qc = conv(0, base, cwq_ref[...])
        kc = conv(1, base, cwk_ref[...])
        v2 = jnp.concatenate([conv(2, base, cwv[:, :GDN_HD]), conv(3, base, cwv[:, GDN_HD:])], axis=0)
        qn = qc * (lax.rsqrt(jnp.sum(qc * qc, axis=-1, keepdims=True) + 1e-6) * (GDN_HD ** -0.5))
        kn = kc * lax.rsqrt(jnp.sum(kc * kc, axis=-1, keepdims=True) + 1e-6)
        q2 = jnp.concatenate([qn, qn], axis=0)
        k2 = jnp.concatenate([kn, kn], axis=0)
        gram = _dot_nt(jnp.concatenate([kn, qn], axis=0).astype(BF16), k2.astype(BF16))
        kk = jnp.concatenate([gram[:CHUNK], gram[:CHUNK]], axis=0)
        qk = jnp.concatenate([gram[CHUNK:], gram[CHUNK:]], axis=0)
        r = g * SUB
        gc_f, gc_b = gt["cum_f"][r:r + 1], gt["cum_b"][r + 2:r + 3]
        gcc_d = [gt["cum_f_t"][:, r:r + 1], gt["cum_b_t"][:, r + 2:r + 3]]
        totc_d = [gt["tot_t"][:, r:r + 1], gt["tot_t"][:, r + 2:r + 3]]
        bc_d = [gt["beta_t"][:, r + 1:r + 2], gt["beta_t"][:, r + 3:r + 4]]
        m = jnp.exp(lanes2(gcc_d[0], gcc_d[1]) - jnp.concatenate([gc_f, gc_b], axis=1) + mats_ref[0])
        a = (jnp.concatenate([kk, kk], axis=1) * m * mats_ref[1]
             * lanes2(bc_d[0], bc_d[1])).astype(BF16)
        qkm = (jnp.concatenate([qk, qk], axis=1) * m).astype(BF16)
        sa, srhs, sqd, skd, sqkm = stage
        sa[g] = a
        for d in range(2):
            gcc, totc, bc = gcc_d[d], totc_d[d], bc_d[d]
            egc = jnp.exp(gcc)
            dec_s[d * N_CHUNKS + c] = jnp.broadcast_to(gt["dec"][r + 2 * d:r + 2 * d + 1], (SUB, 2 * GDN_HD))
            srhs[2 * g + d] = jnp.concatenate([v2 * bc, k2 * (bc * egc)], axis=1).astype(BF16)
            sqd[2 * g + d] = q2 * egc
            skd[2 * g + d] = (k2 * jnp.exp(totc - gcc)).astype(BF16)
            sqkm[2 * g + d] = qkm[:, d * PAIR:(d + 1) * PAIR]

    def prepare(group, stage):
        first, n = group
        gt = group_gates(first, n)
        for g in range(n):
            chunk_start(first + g, g, gt, stage)

    def mm(p, q):
        return _dot(p.astype(BF16), _block_diag(q.astype(BF16)))

    def solve(group, stage, steps):
        sa, srhs, sqd, skd, sqkm = stage
        steps = list(steps)
        first, n_chunks = group

        def tick():
            if steps:
                recur(steps.pop(0))

        chunks = [dict(a=sa[g], dirs=[dict(idx=d * N_CHUNKS + first + g, rhs=srhs[2 * g + d],
                                           qd=sqd[2 * g + d], kd=skd[2 * g + d], qkm=sqkm[2 * g + d])
                                      for d in range(2)]) for g in range(n_chunks)]
        n_levels = lvl_ref.shape[0]
        for ch in chunks:
            leaf = ch["a"] * lvl_ref[0]
            ch["x"] = mats_ref[2] - leaf.astype(F32)
            ch["p"] = mm(leaf, leaf)
        tick()
        for ch in chunks:
            xp = mm(jnp.concatenate([ch["x"], ch["p"]], axis=0), ch["p"])
            ch["x"] = ch["x"] + xp[:PAIR]
            ch["p"] = xp[PAIR:]
        tick()
        for ch in chunks:
            ch["x"] = ch["x"] + mm(ch["x"], ch["p"])
        tick()
        for lv in range(1, n_levels):
            size = INV_LEAF << (lv - 1)
            later = [(gi * SUB // size) % 2 == 1 for gi in range(PAIR // SUB)]
            keep = (later, [not k for k in later])

            def take(x):
                return jnp.concatenate([_take_groups(x[:, d * PAIR:(d + 1) * PAIR], keep[d]) for d in range(2)],
                                       axis=1)

            def put(y, minuend=None):
                return jnp.concatenate(
                    [_put_groups(y[:, d * PAIR:(d + 1) * PAIR], keep[d],
                                 None if minuend is None else minuend[:, d * PAIR:(d + 1) * PAIR])
                     for d in range(2)], axis=1)

            for ch in chunks:
                ch["y"] = mm(take((ch["a"] * lvl_ref[lv]).astype(F32)), ch["x"])
            tick()
            for ch in chunks:
                ch["x"] = put(mm(take(ch["x"]), put(ch["y"])), minuend=ch["x"])
            tick()
        chains = []
        for ch in chunks:
            xb = ch["x"].astype(BF16)
            for d, cd in enumerate(ch["dirs"]):
                cd["uw"] = _dot(xb[:, d * PAIR:(d + 1) * PAIR], cd["rhs"]).astype(BF16)
                chains.append(cd)
        tick()
        for cd in chains:
            wu = jnp.concatenate([cd["uw"][:, GDN_HD:], cd["uw"][:, :GDN_HD]], axis=1)
            cd["nc"] = [_dot_tn(cd["kd"][h * CHUNK:(h + 1) * CHUNK], wu[h * CHUNK:(h + 1) * CHUNK])
                        for h in range(2)]
            cd["qo"] = _dot(cd["qkm"], wu)
        for cd in chains:
            nc, qo, idx = cd["nc"], cd["qo"], cd["idx"]
            qeff = cd["qd"] - qo[:, :GDN_HD]
            lhs_s[idx] = jnp.concatenate([-nc[0][:, :GDN_HD], qeff[:CHUNK], -nc[1][:, :GDN_HD], qeff[CHUNK:]],
                                         axis=0).astype(BF16)
            c_s[idx] = jnp.concatenate([nc[0][:, GDN_HD:], nc[1][:, GDN_HD:]], axis=1)
            o0_s[idx] = qo[:, GDN_HD:].astype(o0_s.dtype)
        while steps:
            tick()

    def recur(step):
        chains = [dict(d=d, c=c, idx=d * N_CHUNKS + c) for d, c in step]
        rows_h = GDN_HD + CHUNK
        for ch in chains:
            ch["st"] = st_s[ch["d"]]
            stb = ch["st"].astype(BF16)
            ch["r"] = [_dot(lhs_s[ch["idx"], h * rows_h:(h + 1) * rows_h], stb[:, h * GDN_HD:(h + 1) * GDN_HD])
                       for h in range(2)]
        for ch in chains:
            r, idx = ch["r"], ch["idx"]
            ns = jnp.concatenate([r[0][:GDN_HD], r[1][:GDN_HD]], axis=1)
            st_s[ch["d"]] = ch["st"] * dec_s[idx][0:1] + ns + c_s[idx]
            o0 = o0_s[idx]
            r0 = ch["c"] * CHUNK
            acc_s[r0:r0 + CHUNK, 0:GDN_HD] += r[0][GDN_HD:] + o0[:CHUNK]
            acc_s[r0:r0 + CHUNK, GDN_HD:] += r[1][GDN_HD:] + o0[CHUNK:]

    st_s[...] = jnp.zeros_like(st_s)
    acc_s[...] = jnp.zeros_like(acc_s)

    order_f = list(range(N_CHUNKS))
    order_b = list(range(CTX_CHUNKS - 1, -1, -1)) + list(range(N_CHUNKS - 1, CTX_CHUNKS - 1, -1))
    group_order, n_groups = GDN_GROUPS, len(GDN_GROUPS)
    stages = (stage_a, stage_b)
    solved, pos = set(), [0, 0]

    def ready_steps():
        out = []
        while True:
            step = [(d, order[pos[d]]) for d, order in enumerate((order_f, order_b))
                    if pos[d] < N_CHUNKS and order[pos[d]] in solved]
            if not step:
                return out
            for d, _ in step:
                pos[d] += 1
            out.append(step)

    prepare(group_order[0], stages[0])
    for n, grp in enumerate(group_order):
        if n + 1 < n_groups:
            prepare(group_order[n + 1], stages[(n + 1) % 2])
        solve(grp, stages[n % 2], ready_steps())
        solved.update(range(grp[0], grp[0] + grp[1]))
    for step in ready_steps():
        recur(step)

    def finish(t, carry):
        r0 = pl.multiple_of(t * ROW_TILE, ROW_TILE)
        for h in range(2):
            sl = slice(h * GDN_HD, (h + 1) * GDN_HD)
            x = acc_s[pl.ds(r0, ROW_TILE), sl]
            o_ref[0, pl.ds(r0, ROW_TILE), sl] = (
                x * lax.rsqrt(jnp.mean(x * x, axis=-1, keepdims=True) + EPS) * gn_ref[...]).astype(o_ref.dtype)
        return carry

    lax.fori_loop(0, N_ROW_TILES, finish, 0, unroll=3)


def _gdn_mixer(p_main, p_ab, conv_w, a_log, dt_bias, g_norm):
    mats, lvl = _gdn_consts()
    bsz = p_main.shape[0]
    ab = p_ab.reshape(bsz, N_CHUNKS, CHUNK, 2, 2, GDN_QK_HEADS, 2)
    abr = ab.transpose(0, 5, 1, 4, 3, 6, 2).reshape(bsz, GDN_QK_HEADS, N_CHUNKS, 4, PAIR)
    prm = jnp.stack([a_log, dt_bias], axis=0).astype(F32).reshape(2, 2, GDN_QK_HEADS, 2)
    zero = jnp.zeros_like(prm[:, 0])
    rows = jnp.stack([prm[:, 0], prm[:, 0], prm[:, 1], prm[:, 1]] + [zero] * (SUB - 4), axis=1)
    pr = jnp.repeat(rows.transpose(2, 0, 1, 3), CHUNK, axis=3)

    grid = (bsz, GDN_QK_HEADS)
    kq = GDN_KEY_DIM // GDN_HD
    in_specs = [
        pl.BlockSpec((1, L_ALL, GDN_HD), lambda b, j: (b, 0, j)),
        pl.BlockSpec((1, L_ALL, GDN_HD), lambda b, j: (b, 0, kq + j)),
        pl.BlockSpec((1, L_ALL, 2 * GDN_HD), lambda b, j: (b, 0, kq + j)),
        pl.BlockSpec((4, GDN_HD), lambda b, j: (0, j)),
        pl.BlockSpec((4, GDN_HD), lambda b, j: (0, kq + j)),
        pl.BlockSpec((4, 2 * GDN_HD), lambda b, j: (0, kq + j)),
        pl.BlockSpec((1, 1, N_CHUNKS, 4, PAIR), lambda b, j: (b, j, 0, 0, 0)),
        pl.BlockSpec((1, 2, SUB, PAIR), lambda b, j: (j, 0, 0, 0)),
        pl.BlockSpec(mats.shape, lambda b, j: (0, 0, 0)),
        pl.BlockSpec(lvl.shape, lambda b, j: (0, 0, 0)),
        pl.BlockSpec((1, GDN_HD), lambda b, j: (0, 0)),
    ]
    return pl.pallas_call(
        _gdn_body,
        grid=grid,
        in_specs=in_specs,
        out_specs=pl.BlockSpec((1, L_ALL, 2 * GDN_HD), lambda b, j: (b, 0, j)),
        out_shape=jax.ShapeDtypeStruct((bsz, L_ALL, GDN_VAL_DIM), ACT),
        scratch_shapes=[
            pltpu.VMEM((4, L_ALL + 3 * SUB, GDN_HD), F32),
            pltpu.VMEM((2 * N_CHUNKS, 3 * GDN_HD, GDN_HD), BF16),
            pltpu.VMEM((2 * N_CHUNKS, GDN_HD, 2 * GDN_HD), F32),
            pltpu.VMEM((2 * N_CHUNKS, PAIR, GDN_HD), BF16),
            pltpu.VMEM((2 * N_CHUNKS, SUB, 2 * GDN_HD), F32),
            pltpu.VMEM((2, GDN_HD, 2 * GDN_HD), F32),
            pltpu.VMEM((L_ALL, 2 * GDN_HD), F32),
        ] + 2 * [
            pltpu.VMEM((A_GROUP, PAIR, 2 * PAIR), BF16),
            pltpu.VMEM((2 * A_GROUP, PAIR, 2 * GDN_HD), BF16),
            pltpu.VMEM((2 * A_GROUP, PAIR, GDN_HD), F32),
            pltpu.VMEM((2 * A_GROUP, PAIR, GDN_HD), BF16),
            pltpu.VMEM((2 * A_GROUP, PAIR, PAIR), BF16),
        ],
        compiler_params=_cparams(("parallel", "parallel")),
    )(p_main, p_main, p_main, conv_w, conv_w, conv_w, abr, pr, mats, lvl,
      g_norm.reshape(1, GDN_HD))


def _lru_body(u_ref, cw_ref, cb_ref, wra_ref, wri_ref, bra_ref, bri_ref, lam_ref, o_ref, a_s, b_s, acc_s):
    full = slice(None)
    n_steps = L_ALL // LRU_ROWS

    def gates(c, carry):
        r0 = pl.multiple_of(c * LRU_ROWS, LRU_ROWS)
        left_ok = jnp.where(c >= 2, 1.0, 0.0)
        right_ok = jnp.where((c == 0) | (c == n_steps - 1), 0.0, 1.0)
        u = _conv_rows(u_ref, r0, LRU_ROWS, full, cw_ref[...], left_ok, right_ok) + cb_ref[...]
        ub = u.astype(BF16)
        blocks = [slice(n * LRU_BLOCK, (n + 1) * LRU_BLOCK) for n in range(LRU_LANES // LRU_BLOCK)]
        for d in range(2):
            r = _sigmoid(jnp.concatenate([_dot(ub[:, sl], wra_ref[d, n]) for n, sl in enumerate(blocks)],
                                               axis=1) + bra_ref[d:d + 1])
            i = _sigmoid(jnp.concatenate([_dot(ub[:, sl], wri_ref[d, n]) for n, sl in enumerate(blocks)],
                                               axis=1) + bri_ref[d:d + 1])
            log_a = -LRU_C * r * _softplus(-lam_ref[d:d + 1])
            a = jnp.exp(log_a)
            one_m_a2 = -jnp.tanh(log_a) * (a * a + 1.0)
            a_s[d, pl.ds(r0, LRU_ROWS), :] = a
            b_s[d, pl.ds(r0, LRU_ROWS), :] = jnp.sqrt(one_m_a2) * (i * u)
        return carry

    lax.fori_loop(0, n_steps, gates, 0)

    acc_s[...] = jnp.zeros_like(acc_s)
    row = lax.broadcasted_iota(jnp.int32, (SUB, LRU_LANES), 0)

    def scan(g, carry):
        hf, hb = carry
        gb = jnp.where(g < CTX_GROUPS, CTX_GROUPS - 1 - g, N_GROUPS + CTX_GROUPS - 1 - g)
        r0 = pl.multiple_of(g * SUB, SUB)
        a, b = a_s[0, pl.ds(r0, SUB), :], b_s[0, pl.ds(r0, SUB), :]
        for s in (1, 2, 4):
            ok = row >= s
            b = jnp.where(ok, a * pltpu.roll(b, s, 0) + b, b)
            a = jnp.where(ok, a * pltpu.roll(a, s, 0), a)
        h = b + a * hf
        acc_s[pl.ds(r0, SUB), :] += h
        hf = h[SUB - 1:SUB]
        r0 = pl.multiple_of(gb * SUB, SUB)
        a, b = a_s[1, pl.ds(r0, SUB), :], b_s[1, pl.ds(r0, SUB), :]
        for s in (1, 2, 4):
            ok = row < SUB - s
            b = jnp.where(ok, a * pltpu.roll(b, SUB - s, 0) + b, b)
            a = jnp.where(ok, a * pltpu.roll(a, SUB - s, 0), a)
        h = b + a * hb
        acc_s[pl.ds(r0, SUB), :] += h
        hb = h[0:1]
        return hf, hb

    zero = jnp.zeros((1, LRU_LANES), F32)
    lax.fori_loop(0, N_GROUPS, scan, (zero, zero), unroll=4)

    def emit(t, carry):
        r0 = pl.multiple_of(t * LRU_ROWS, LRU_ROWS)
        o_ref[0, pl.ds(r0, LRU_ROWS), :] = acc_s[pl.ds(r0, LRU_ROWS), :].astype(o_ref.dtype)
        return carry

    lax.fori_loop(0, n_steps, emit, 0)


def _lru_mixer(p, conv_w, conv_b, w_ra, w_ri, b_ra, b_ri, lam):
    nb = LRU_WIDTH // LRU_LANES
    per = LRU_LANES // LRU_BLOCK
    bsz = p.shape[0]
    vec = lambda rows: pl.BlockSpec((rows, LRU_LANES), lambda b, n: (0, n))
    wspec = pl.BlockSpec((2, per, LRU_BLOCK, LRU_BLOCK), lambda b, n: (0, n, 0, 0))
    return pl.pallas_call(
        _lru_body,
        grid=(bsz, nb),
        in_specs=[
            pl.BlockSpec((1, L_ALL, LRU_LANES), lambda b, n: (b, 0, nb + n)),
            vec(4), vec(1), wspec, wspec, vec(2), vec(2), vec(2),
        ],
        out_specs=pl.BlockSpec((1, L_ALL, LRU_LANES), lambda b, n: (b, 0, n)),
        out_shape=jax.ShapeDtypeStruct((bsz, L_ALL, LRU_WIDTH), ACT),
        scratch_shapes=[pltpu.VMEM((2, L_ALL, LRU_LANES), F32), pltpu.VMEM((2, L_ALL, LRU_LANES), F32),
                        pltpu.VMEM((L_ALL, LRU_LANES), F32)],
        compiler_params=_cparams(("parallel", "parallel")),
    )(p, conv_w, conv_b.reshape(1, LRU_WIDTH), w_ra.astype(BF16), w_ri.astype(BF16), b_ra, b_ri, lam)


def _att_body(sink_ref, q_ref, kc_ref, vc_ref, k0_ref, k1_ref, k2_ref, v0_ref, v1_ref, v2_ref, o_ref):
    i = pl.program_id(1)
    n_ctx_blocks = CTX_LEN // ATT_BLOCK
    li = i - n_ctx_blocks
    qpos = li * ATT_BLOCK + lax.broadcasted_iota(jnp.int32, (ATT_BLOCK, 3 * ATT_BLOCK), 0)
    kpos = (li - 1) * ATT_BLOCK + lax.broadcasted_iota(jnp.int32, (ATT_BLOCK, 3 * ATT_BLOCK), 1)
    band = (jnp.abs(qpos - kpos) <= ATT_WINDOW) & (kpos >= 0) & (kpos < SEQ) & (li >= 0)
    bias = jnp.concatenate([jnp.zeros((ATT_BLOCK, CTX_LEN), F32), jnp.where(band, 0.0, NEG)], axis=1)
    bias4 = jnp.concatenate([bias] * ATT_GROUP, axis=0)
    lane = lax.broadcasted_iota(jnp.int32, (ATT_BLOCK, ATT_QW), 1)
    qmask = [(lane % 128) // 32 == g for g in range(ATT_GROUP)]
    omask = [lane // ATT_HD == g for g in range(ATT_GROUP)]
    rowg = lax.broadcasted_iota(jnp.int32, (ATT_GROUP * ATT_BLOCK, 1), 0) // ATT_BLOCK
    def scores(h):
        sl = slice(h * ATT_QW, (h + 1) * ATT_QW)
        qh = q_ref[0, :, sl]
        qm = jnp.concatenate([jnp.where(qmask[g], qh, 0.0) for g in range(ATT_GROUP)], axis=0).astype(BF16)
        keys = jnp.concatenate([kc_ref[0, :, sl], k0_ref[0, :, sl], k1_ref[0, :, sl], k2_ref[0, :, sl]],
                               axis=0).astype(BF16)
        return _dot_nt(qm, keys)

    s_next = scores(0)
    for h in range(ATT_KV_HEADS):
        sl = slice(h * ATT_QW, (h + 1) * ATT_QW)
        s = s_next + bias4
        if h + 1 < ATT_KV_HEADS:
            s_next = scores(h + 1)
        vals = jnp.concatenate([vc_ref[0, :, sl], v0_ref[0, :, sl], v1_ref[0, :, sl], v2_ref[0, :, sl]],
                               axis=0).astype(BF16)
        sink = jnp.zeros((ATT_GROUP * ATT_BLOCK, 1), F32)
        for g in range(ATT_GROUP):
            sink = jnp.where(rowg == g, sink_ref[h * ATT_GROUP + g], sink)
        mx = jnp.maximum(jnp.max(s, axis=-1, keepdims=True), sink)
        e = jnp.exp(s - mx)
        den = jnp.sum(e, axis=-1, keepdims=True) + jnp.exp(sink - mx)
        r = _dot(e.astype(BF16), vals) / den
        out = jnp.zeros((ATT_BLOCK, ATT_QW), F32)
        for g in range(ATT_GROUP):
            out = jnp.where(omask[g], r[g * ATT_BLOCK:(g + 1) * ATT_BLOCK], out)
        o_ref[0, :, sl] = out.astype(o_ref.dtype)


def _att_mixer(p, sinks):
    n_ctx_blocks = CTX_LEN // ATT_BLOCK
    n_lat_blocks = SEQ // ATT_BLOCK
    bsz = p.shape[0]

    def win(off, col):
        def index(b, i, sink_ref):
            blk = jnp.clip(i - n_ctx_blocks + off, 0, n_lat_blocks - 1)
            return (b, n_ctx_blocks + blk, col)
        return pl.BlockSpec((1, ATT_BLOCK, D_MODEL), index)

    grid_spec = pltpu.PrefetchScalarGridSpec(
        num_scalar_prefetch=1,
        grid=(bsz, L_ALL // ATT_BLOCK),
        in_specs=[
            pl.BlockSpec((1, ATT_BLOCK, D_MODEL), lambda b, i, s: (b, i, 0)),
            pl.BlockSpec((1, CTX_LEN, D_MODEL), lambda b, i, s: (b, 0, 2)),
            pl.BlockSpec((1, CTX_LEN, D_MODEL), lambda b, i, s: (b, 0, 3)),
            win(-1, 2), win(0, 2), win(1, 2), win(-1, 3), win(0, 3), win(1, 3),
        ],
        out_specs=pl.BlockSpec((1, ATT_BLOCK, D_MODEL), lambda b, i, s: (b, i, 0)),
    )
    return pl.pallas_call(
        _att_body,
        grid_spec=grid_spec,
        out_shape=jax.ShapeDtypeStruct((bsz, L_ALL, D_MODEL), ACT),
        compiler_params=_cparams(("parallel", "arbitrary")),
    )(sinks.astype(F32), p, p, p, p, p, p, p, p, p)


def _att_weight_layout(w_in):
    half = ATT_HD // 2
    q_cols, k_cols, v_cols = [], [], []
    for h in range(ATT_KV_HEADS):
        for part in range(2):
            for g in range(ATT_GROUP):
                base = (h * ATT_GROUP + g) * ATT_HD + part * half
                q_cols += list(range(base, base + half))
                kb = D_MODEL + h * ATT_HD + part * half
                k_cols += list(range(kb, kb + half))
        vb = D_MODEL + ATT_KV_HEADS * ATT_HD + h * ATT_HD
        v_cols += list(range(vb, vb + ATT_HD)) * ATT_GROUP
    z0 = D_MODEL + 2 * ATT_KV_HEADS * ATT_HD
    z_cols = list(range(z0, z0 + D_MODEL))
    order = np.asarray(q_cols + z_cols + k_cols + v_cols, np.int32)
    scale = np.ones((ATT_COLS,), np.float32)
    scale[:D_MODEL] = ATT_HD ** -0.5
    return (w_in[:, order] * scale).astype(BF16)


def _rope_tables():
    rows = SEQ // GRID_W
    row = jnp.repeat(jnp.arange(rows), GRID_W)
    col = jnp.tile(jnp.arange(GRID_W), rows)
    n_freq = ATT_HD // 4
    inv = 10000.0 ** (-jnp.arange(n_freq, dtype=F32) / n_freq)
    ang = jnp.concatenate([row[:, None] * inv, col[:, None] * inv], axis=-1)
    cos = jnp.concatenate([jnp.ones((CTX_LEN, ATT_HD // 2), F32), jnp.cos(ang)], axis=0)
    sin = jnp.concatenate([jnp.zeros((CTX_LEN, ATT_HD // 2), F32), jnp.sin(ang)], axis=0)
    return jnp.tile(cos, (1, ATT_GROUP)), jnp.tile(sin, (1, ATT_GROUP))


def kernel(x, c, ctx, c_ctx, ada_w, ada_b, norm_pre, norm_post, gdn_w_in, gdn_conv_w, gdn_a_log, gdn_dt_bias, gdn_g_norm, gdn_w_out, lru_w_in, lru_conv_w, lru_conv_b, lru_w_ra, lru_b_ra, lru_w_ri, lru_b_ri, lru_lam, lru_w_out, att_w_in, att_sinks, att_w_out):
    cc = jnp.zeros((MOD_ROWS, D_MODEL), F32).at[:c.shape[0]].set(c).at[CTX_MOD_ROW].set(c_ctx)
    mod_all = _modulation(cc, ada_w, ada_b).reshape(DEPTH, MOD_ROWS, 1, 3 * D_MODEL)
    xs = jnp.concatenate([ctx, x], axis=1)
    cos, sin = _rope_tables()
    for i in range(DEPTH):
        kind, j = i % 3, i // 3
        mod = mod_all[i]
        last = i == DEPTH - 1
        if kind == 0:
            p_main, p_ab = _in_proj(xs, mod, norm_pre[i], gdn_w_in[j].astype(BF16),
                                    [(0, GDN_MAIN), (GDN_MAIN, GDN_AB)], [ACT, F32])
            o = _gdn_mixer(p_main, p_ab, gdn_conv_w[j], gdn_a_log[j], gdn_dt_bias[j], gdn_g_norm[j])
            xs = _out_proj(o, p_main, 2, xs, mod, norm_post[i], gdn_w_out[j].astype(BF16), latent_only=last)
        elif kind == 1:
            (p,) = _in_proj(xs, mod, norm_pre[i], lru_w_in[j].astype(BF16), [(0, 2 * LRU_WIDTH)], [ACT])
            o = _lru_mixer(p, lru_conv_w[j], lru_conv_b[j], lru_w_ra[j], lru_w_ri[j],
                           lru_b_ra[j], lru_b_ri[j], lru_lam[j])
            xs = _out_proj(o, p, 0, xs, mod, norm_post[i], lru_w_out[j].astype(BF16), latent_only=last)
        else:
            (p,) = _in_proj(xs, mod, norm_pre[i], _att_weight_layout(att_w_in[j]), [(0, ATT_COLS)], [ACT],
                            rope=(cos, sin, (0, 1, 4, 5)))
            o = _att_mixer(p, att_sinks[j])
            xs = _out_proj(o, p, 1, xs, mod, norm_post[i], att_w_out[j].astype(BF16), latent_only=last)
    return xs
```

```python
import functools

import numpy as np
import jax
import jax.numpy as jnp
from jax import lax
from jax.experimental import pallas as pl
from jax.experimental.pallas import tpu as pltpu

F32 = jnp.float32
BF16 = jnp.bfloat16
ACT = jnp.bfloat16

D_MODEL = 1024
BATCH = 8
SEQ = 2048
CTX_LEN = 256
L_ALL = CTX_LEN + SEQ
DEPTH = 4
GRID_W = 64
EPS = 1e-6

ROW_TILE = 256
N_ROW_TILES = L_ALL // ROW_TILE
MOD_ROWS = 16
CTX_MOD_ROW = BATCH
VMEM_LIMIT = 56 * 1024 * 1024

GDN_HD = 128
GDN_QK_HEADS = 8
GDN_V_HEADS = 16
GDN_KEY_DIM = 1024
GDN_VAL_DIM = 2048
GDN_MAIN = 2 * GDN_KEY_DIM + 2 * GDN_VAL_DIM
GDN_AB = 64
CHUNK = 64
N_CHUNKS = L_ALL // CHUNK
CTX_CHUNKS = CTX_LEN // CHUNK
PAIR = 2 * CHUNK
GDN_GROUPS = ((0, 4), (28, 8), (4, 8), (22, 6), (12, 6), (18, 4))
A_GROUP = max(n for _, n in GDN_GROUPS)
assert sorted(c for f, n in GDN_GROUPS for c in range(f, f + n)) == list(range(N_CHUNKS))
INV_LEAF = 8

LRU_WIDTH = 1024
LRU_BLOCK = 128
LRU_LANES = 256
LRU_C = 8.0
LRU_ROWS = 256
SUB = 8
N_GROUPS = L_ALL // SUB
CTX_GROUPS = CTX_LEN // SUB

ATT_HD = 64
ATT_KV_HEADS = 4
ATT_GROUP = 4
ATT_BLOCK = 128
ATT_WINDOW = 128
ATT_QW = ATT_GROUP * ATT_HD
ATT_COLS = 4 * D_MODEL
NEG = -1e30


def _silu(x):
    return x * _sigmoid(x)


def _sigmoid(x):
    return 0.5 * jnp.tanh(0.5 * x) + 0.5


def _softplus(x):
    return jnp.maximum(x, 0.0) + jnp.log1p(jnp.exp(-jnp.abs(x)))


def _cparams(sem):
    return pltpu.CompilerParams(dimension_semantics=sem, vmem_limit_bytes=VMEM_LIMIT)


def _dot(a, b):
    return jnp.dot(a, b, preferred_element_type=F32)


def _dot_nt(a, b):
    return lax.dot_general(a, b, (((1,), (1,)), ((), ())), preferred_element_type=F32)


def _dot_tn(a, b):
    return lax.dot_general(a, b, (((0,), (0,)), ((), ())), preferred_element_type=F32)


def _mod_body(c_ref, w_ref, b_ref, o_ref):
    h = _silu(c_ref[...]).astype(BF16)
    o_ref[0] = _dot(h, w_ref[0].astype(BF16)) + b_ref[0]


def _modulation(cc, ada_w, ada_b):
    n_col = 3
    return pl.pallas_call(
        _mod_body,
        grid=(DEPTH, n_col),
        in_specs=[
            pl.BlockSpec((MOD_ROWS, D_MODEL), lambda i, n: (0, 0)),
            pl.BlockSpec((1, D_MODEL, D_MODEL), lambda i, n: (i, 0, n)),
            pl.BlockSpec((1, 1, D_MODEL), lambda i, n: (i, 0, n)),
        ],
        out_specs=pl.BlockSpec((1, MOD_ROWS, D_MODEL), lambda i, n: (i, 0, n)),
        out_shape=jax.ShapeDtypeStruct((DEPTH, MOD_ROWS, 3 * D_MODEL), F32),
        compiler_params=_cparams(("parallel", "parallel")),
    )(cc, ada_w, ada_b.reshape(DEPTH, 1, 3 * D_MODEL))


def _mod_index(b, t):
    return (jnp.where(t == 0, CTX_MOD_ROW, b), 0, 0)


def _in_proj_body(x_ref, mod_ref, nw_ref, w_ref, *refs, segs, rope_groups):
    _project(x_ref[0], mod_ref, nw_ref, w_ref, refs, segs, rope_groups)


def _project(x, mod_ref, nw_ref, w_ref, refs, segs, rope_groups):
    if rope_groups:
        cos_ref, sin_ref = refs[:2]
        refs = refs[2:]
    y = x * lax.rsqrt(jnp.mean(x * x, axis=-1, keepdims=True) + EPS) * nw_ref[...]
    shift = mod_ref[0, :, 0:D_MODEL]
    scale = mod_ref[0, :, D_MODEL:2 * D_MODEL]
    h = (y * (1.0 + scale) + shift).astype(BF16)
    for ref, (c0, width) in zip(refs, segs):
        step = min(width, 512)
        for n0 in range(0, width, step):
            acc = _dot(h, w_ref[:, c0 + n0:c0 + n0 + step])
            if (c0 + n0) // 512 in rope_groups:
                cos = cos_ref[...]
                sin = sin_ref[...]
                parts = []
                for g0 in range(0, step, 256):
                    x1 = acc[:, g0:g0 + 128]
                    x2 = acc[:, g0 + 128:g0 + 256]
                    parts += [x1 * cos - x2 * sin, x2 * cos + x1 * sin]
                acc = jnp.concatenate(parts, axis=1)
            ref[0, :, n0:n0 + step] = acc.astype(ref.dtype)


def _in_proj(x, mod, nw, w, segs, out_dtypes, rope=None):
    n_total = w.shape[1]
    in_specs = [
        pl.BlockSpec((1, ROW_TILE, D_MODEL), lambda b, t: (b, t, 0)),
        pl.BlockSpec((1, 1, 3 * D_MODEL), _mod_index),
        pl.BlockSpec((1, D_MODEL), lambda b, t: (0, 0)),
        pl.BlockSpec((D_MODEL, n_total), lambda b, t: (0, 0), pipeline_mode=pl.Buffered(1)),
    ]
    args = [x, mod, nw.reshape(1, D_MODEL), w]
    rope_groups = ()
    if rope is not None:
        cos, sin, rope_groups = rope
        in_specs += [pl.BlockSpec((ROW_TILE, 128), lambda b, t: (t, 0))] * 2
        args += [cos, sin]
    out_specs = [pl.BlockSpec((1, ROW_TILE, width), lambda b, t: (b, t, 0)) for _, width in segs]
    bsz = x.shape[0]
    out_shape = [jax.ShapeDtypeStruct((bsz, L_ALL, width), dt) for (_, width), dt in zip(segs, out_dtypes)]
    return pl.pallas_call(
        functools.partial(_in_proj_body, segs=tuple(segs), rope_groups=tuple(rope_groups)),
        grid=(bsz, N_ROW_TILES),
        in_specs=in_specs,
        out_specs=out_specs,
        out_shape=out_shape,
        compiler_params=_cparams(("parallel", "parallel")),
    )(*args)


def _out_proj_body(o_ref, z_ref, x_ref, mod_ref, nw_ref, w_ref, out_ref):
    g = (o_ref[0].astype(F32) * _silu(z_ref[0].astype(F32))).astype(BF16)
    y = _dot(g, w_ref[...])
    yn = y * lax.rsqrt(jnp.mean(y * y, axis=-1, keepdims=True) + EPS) * nw_ref[...]
    gate = mod_ref[0, :, 2 * D_MODEL:3 * D_MODEL]
    out_ref[0] = x_ref[0] + gate * yn


def _out_proj(o, z_arr, z_block, x, mod, nw, w, latent_only=False):
    width = o.shape[-1]
    bsz = o.shape[0]
    skip = CTX_LEN // ROW_TILE if latent_only else 0
    return pl.pallas_call(
        _out_proj_body,
        grid=(bsz, N_ROW_TILES - skip),
        in_specs=[
            pl.BlockSpec((1, ROW_TILE, width), lambda b, t: (b, t + skip, 0)),
            pl.BlockSpec((1, ROW_TILE, width), lambda b, t: (b, t + skip, z_block)),
            pl.BlockSpec((1, ROW_TILE, D_MODEL), lambda b, t: (b, t + skip, 0)),
            pl.BlockSpec((1, 1, 3 * D_MODEL), lambda b, t: _mod_index(b, t + skip)),
            pl.BlockSpec((1, D_MODEL), lambda b, t: (0, 0)),
            pl.BlockSpec((width, D_MODEL), lambda b, t: (0, 0), pipeline_mode=pl.Buffered(1)),
        ],
        out_specs=pl.BlockSpec((1, ROW_TILE, D_MODEL), lambda b, t: (b, t, 0)),
        out_shape=jax.ShapeDtypeStruct((bsz, L_ALL - skip * ROW_TILE, D_MODEL), F32),
        compiler_params=_cparams(("parallel", "parallel")),
    )(o, z_arr, x, mod, nw.reshape(1, D_MODEL), w)


def _out_in_body(o_ref, z_ref, x_ref, mod_ref, nw_post_ref, w_out_ref, mod_next_ref, nw_pre_ref, w_in_ref,
                 *refs, segs, rope_groups):
    n_extra = 2 if rope_groups else 0
    extra, out_ref, p_refs = refs[:n_extra], refs[n_extra], refs[n_extra + 1:]
    g = (o_ref[0].astype(F32) * _silu(z_ref[0].astype(F32))).astype(BF16)
    y = _dot(g, w_out_ref[...])
    yn = y * lax.rsqrt(jnp.mean(y * y, axis=-1, keepdims=True) + EPS) * nw_post_ref[...]
    x_new = x_ref[0] + mod_ref[0, :, 2 * D_MODEL:3 * D_MODEL] * yn
    out_ref[0] = x_new
    _project(x_new, mod_next_ref, nw_pre_ref, w_in_ref, tuple(extra) + tuple(p_refs), segs, rope_groups)


def _out_in_proj(o, z_arr, z_block, x, mod, nw_post, w_out, mod_next, nw_pre, w_in, segs, out_dtypes, rope=None):
    width = o.shape[-1]
    bsz = o.shape[0]
    row = lambda cols: pl.BlockSpec((1, ROW_TILE, cols), lambda b, t: (b, t, 0))
    const = lambda shape: pl.BlockSpec(shape, lambda b, t: (0,) * len(shape), pipeline_mode=pl.Buffered(1))
    in_specs = [
        row(width),
        pl.BlockSpec((1, ROW_TILE, width), lambda b, t: (b, t, z_block)),
        row(D_MODEL),
        pl.BlockSpec((1, 1, 3 * D_MODEL), _mod_index),
        pl.BlockSpec((1, D_MODEL), lambda b, t: (0, 0)),
        const((width, D_MODEL)),
        pl.BlockSpec((1, 1, 3 * D_MODEL), _mod_index),
        pl.BlockSpec((1, D_MODEL), lambda b, t: (0, 0)),
        const((D_MODEL, w_in.shape[1])),
    ]
    args = [o, z_arr, x, mod, nw_post.reshape(1, D_MODEL), w_out, mod_next, nw_pre.reshape(1, D_MODEL), w_in]
    rope_groups = ()
    if rope is not None:
        cos, sin, rope_groups = rope
        in_specs += [pl.BlockSpec((ROW_TILE, 128), lambda b, t: (t, 0))] * 2
        args += [cos, sin]
    out_specs = [row(D_MODEL)] + [row(w) for _, w in segs]
    out_shape = ([jax.ShapeDtypeStruct((bsz, L_ALL, D_MODEL), F32)]
                 + [jax.ShapeDtypeStruct((bsz, L_ALL, w), dt) for (_, w), dt in zip(segs, out_dtypes)])
    return pl.pallas_call(
        functools.partial(_out_in_body, segs=tuple(segs), rope_groups=tuple(rope_groups)),
        grid=(bsz, N_ROW_TILES),
        in_specs=in_specs,
        out_specs=out_specs,
        out_shape=out_shape,
        compiler_params=_cparams(("parallel", "parallel")),
    )(*args)


def _conv_rows(ref, r0, rows, width_sl, cw, left_ok, right_ok):
    total = ref.shape[1]
    halo = 2 * SUB
    main = ref[0, pl.ds(r0, rows), width_sl].astype(F32)
    prev = ref[0, pl.ds(pl.multiple_of(jnp.maximum(r0 - halo, 0), halo), halo), width_sl].astype(F32) * left_ok
    nxt = ref[0, pl.ds(pl.multiple_of(jnp.minimum(r0 + rows, total - halo), halo), halo),
              width_sl].astype(F32) * right_ok
    xw = jnp.concatenate([prev, main, nxt], axis=0)
    return (cw[0:1] * xw[halo - 2:halo - 2 + rows] + cw[1:2] * xw[halo - 1:halo - 1 + rows]
            + cw[2:3] * xw[halo:halo + rows] + cw[3:4] * xw[halo + 1:halo + 1 + rows])


def _gdn_consts():
    i = np.arange(PAIR)
    same = (i[:, None] // CHUNK) == (i[None, :] // CHUNK)
    t_i, t_m = i[:, None] % CHUNK, i[None, :] % CHUNK
    low = same & (t_m <= t_i)
    upp = same & (t_m >= t_i)
    both = lambda f, b: np.concatenate([f, b], axis=1)
    eye = np.eye(PAIR, dtype=np.float32)
    mats = np.stack([
        both(np.where(low, 0.0, NEG), np.where(upp, 0.0, NEG)),
        both(same & (t_m < t_i), same & (t_m > t_i)).astype(np.float32),
        both(eye, eye)]).astype(np.float32)
    levels = [(i[:, None] // INV_LEAF) == (i[None, :] // INV_LEAF)]
    size = INV_LEAF
    while size < CHUNK:
        levels.append(((i[:, None] // (2 * size)) == (i[None, :] // (2 * size)))
                      & ((i[:, None] // size) != (i[None, :] // size)))
        size *= 2
    lvl = np.stack([both(m, m) for m in levels]).astype(np.float32)
    return jnp.asarray(mats, F32), jnp.asarray(lvl, BF16)


def _block_diag(x):
    zero = jnp.zeros((PAIR, PAIR), x.dtype)
    return jnp.concatenate([jnp.concatenate([x[:, :PAIR], zero], axis=1),
                            jnp.concatenate([zero, x[:, PAIR:]], axis=1)], axis=0)


def _take_groups(x, keep):
    return jnp.concatenate([x[g * SUB:(g + 1) * SUB] for g, k in enumerate(keep) if k], axis=0)


def _put_groups(y, keep, minuend=None):
    out, j = [], 0
    for g, k in enumerate(keep):
        base = None if minuend is None else minuend[g * SUB:(g + 1) * SUB]
        if k:
            piece = y[j * SUB:(j + 1) * SUB]
            out.append(piece if base is None else base - piece)
            j += 1
        else:
            out.append(jnp.zeros((SUB, y.shape[1]), y.dtype) if base is None else base)
    return jnp.concatenate(out, axis=0)


def _gdn_body(q_ref, k_ref, v_ref, cwq_ref, cwk_ref, cwv_ref, abr_ref, pr_ref,
              mats_ref, lvl_ref, gn_ref, o_ref,
              pad_s, lhs_s, c_s, o0_s, dec_s, st_s, acc_s, *stage_refs):
    pad_s[:, 0:SUB, :] = jnp.zeros((4, SUB, GDN_HD), F32)
    pad_s[:, SUB + CTX_LEN:2 * SUB + CTX_LEN, :] = jnp.zeros((4, SUB, GDN_HD), F32)
    pad_s[:, 2 * SUB + L_ALL:3 * SUB + L_ALL, :] = jnp.zeros((4, SUB, GDN_HD), F32)

    def fill(t, carry):
        src = pl.multiple_of(t * ROW_TILE, ROW_TILE)
        dst = pl.multiple_of(src + jnp.where(t == 0, SUB, 2 * SUB), SUB)
        pad_s[0, pl.ds(dst, ROW_TILE), :] = q_ref[0, pl.ds(src, ROW_TILE), :].astype(F32)
        pad_s[1, pl.ds(dst, ROW_TILE), :] = k_ref[0, pl.ds(src, ROW_TILE), :].astype(F32)
        pad_s[2, pl.ds(dst, ROW_TILE), :] = v_ref[0, pl.ds(src, ROW_TILE), 0:GDN_HD].astype(F32)
        pad_s[3, pl.ds(dst, ROW_TILE), :] = v_ref[0, pl.ds(src, ROW_TILE), GDN_HD:].astype(F32)
        return carry

    lax.fori_loop(0, N_ROW_TILES, fill, 0)
    stage_a, stage_b = stage_refs[:5], stage_refs[5:]
    t_lane = lax.broadcasted_iota(jnp.int32, (A_GROUP * SUB, PAIR), 1)
    t_in_chunk = t_lane % CHUNK

    def conv(which, base, cw):
        taps = [pad_s[which, pl.ds(base + (k - 2), CHUNK), :] for k in range(4)]
        return _silu(cw[0:1] * taps[0] + cw[1:2] * taps[1] + cw[2:3] * taps[2] + cw[3:4] * taps[3])

    def lanes2(f, b):
        return jnp.concatenate([jnp.broadcast_to(f, (PAIR, PAIR)), jnp.broadcast_to(b, (PAIR, PAIR))], axis=1)

    def group_gates(first, n):
        zero = jnp.zeros((SUB - 4, PAIR), F32)
        rows = [x for g in range(n) for x in (abr_ref[0, 0, first + g], zero)]
        rows += [jnp.zeros((SUB, PAIR), F32)] * (A_GROUP - n)
        xr = jnp.concatenate(rows, axis=0)
        g_r = -jnp.exp(jnp.tile(pr_ref[0, 0], (A_GROUP, 1))) * _softplus(xr + jnp.tile(pr_ref[0, 1], (A_GROUP, 1)))
        beta_r = _sigmoid(xr)
        lane, t_chunk = t_lane, t_in_chunk
        cum_f, cum_b = g_r, g_r
        step = 1
        while step < CHUNK:
            cum_f = cum_f + jnp.where(t_chunk >= step, pltpu.roll(cum_f, step, 1), 0.0)
            cum_b = cum_b + jnp.where(t_chunk < CHUNK - step, pltpu.roll(cum_b, PAIR - step, 1), 0.0)
            step *= 2
        tot = cum_f + cum_b - g_r
        other = pltpu.roll(tot, CHUNK, 1)
        tot_h0 = jnp.where(lane < CHUNK, tot, other)
        tot_h1 = jnp.where(lane < CHUNK, other, tot)
        pad = jnp.zeros((PAIR - A_GROUP * SUB, PAIR), F32)
        col = lambda x: jnp.concatenate([x, pad], axis=0).T
        return dict(cum_f=cum_f, cum_b=cum_b, cum_f_t=col(cum_f), cum_b_t=col(cum_b), tot_t=col(tot),
                    beta_t=col(beta_r), dec=jnp.exp(jnp.concatenate([tot_h0, tot_h1], axis=1)))

    def chunk_start(c, g, gt, stage):
        base = c * CHUNK + (SUB if c < CTX_CHUNKS else 2 * SUB)
        cwv = cwv_ref[...]
        qc = conv(0, base, cwq_ref[...])
        kc = conv(1, base, cwk_ref[...])
        v2 = jnp.concatenate([conv(2, base, cwv[:, :GDN_HD]), conv(3, base, cwv[:, GDN_HD:])], axis=0)
        qn = qc * (lax.rsqrt(jnp.sum(qc * qc, axis=-1, keepdims=True) + 1e-6) * (GDN_HD ** -0.5))
        kn = kc * lax.rsqrt(jnp.sum(kc * kc, axis=-1, keepdims=True) + 1e-6)
        q2 = jnp.concatenate([qn, qn], axis=0)
        k2 = jnp.concatenate([kn, kn], axis=0)
        gram = _dot_nt(jnp.concatenate([kn, qn], axis=0).astype(BF16), k2.astype(BF16))
        kk = jnp.concatenate([gram[:CHUNK], gram[:CHUNK]], axis=0)
        qk = jnp.concatenate([gram[CHUNK:], gram[CHUNK:]], axis=0)
        r = g * SUB
        gc_f, gc_b = gt["cum_f"][r:r + 1], gt["cum_b"][r + 2:r + 3]
        gcc_d = [gt["cum_f_t"][:, r:r + 1], gt["cum_b_t"][:, r + 2:r + 3]]
        totc_d = [gt["tot_t"][:, r:r + 1], gt["tot_t"][:, r + 2:r + 3]]
        bc_d = [gt["beta_t"][:, r + 1:r + 2], gt["beta_t"][:, r + 3:r + 4]]
        m = jnp.exp(lanes2(gcc_d[0], gcc_d[1]) - jnp.concatenate([gc_f, gc_b], axis=1) + mats_ref[0])
        a = (jnp.concatenate([kk, kk], axis=1) * m * mats_ref[1]
             * lanes2(bc_d[0], bc_d[1])).astype(BF16)
        qkm = (jnp.concatenate([qk, qk], axis=1) * m).astype(BF16)
        sa, srhs, sqd, skd, sqkm = stage
        sa[g] = a
        for d in range(2):
            gcc, totc, bc = gcc_d[d], totc_d[d], bc_d[d]
            egc = jnp.exp(gcc)
            dec_s[d * N_CHUNKS + c] = jnp.broadcast_to(gt["dec"][r + 2 * d:r + 2 * d + 1], (SUB, 2 * GDN_HD))
            srhs[2 * g + d] = jnp.concatenate([v2 * bc, k2 * (bc * egc)], axis=1).astype(BF16)
            sqd[2 * g + d] = q2 * egc
            skd[2 * g + d] = (k2 * jnp.exp(totc - gcc)).astype(BF16)
            sqkm[2 * g + d] = qkm[:, d * PAIR:(d + 1) * PAIR]

    def prepare(group, stage):
        first, n = group
        gt = group_gates(first, n)
        for g in range(n):
            chunk_start(first + g, g, gt, stage)

    def mm(p, q):
        return _dot(p.astype(BF16), _block_diag(q.astype(BF16)))

    def solve(group, stage, steps):
        sa, srhs, sqd, skd, sqkm = stage
        steps = list(steps)
        first, n_chunks = group

        def tick():
            if steps:
                recur(steps.pop(0))

        chunks = [dict(a=sa[g], dirs=[dict(idx=d * N_CHUNKS + first + g, rhs=srhs[2 * g + d],
                                           qd=sqd[2 * g + d], kd=skd[2 * g + d], qkm=sqkm[2 * g + d])
                                      for d in range(2)]) for g in range(n_chunks)]
        n_levels = lvl_ref.shape[0]
        for ch in chunks:
            leaf = ch["a"] * lvl_ref[0]
            ch["x"] = mats_ref[2] - leaf.astype(F32)
            ch["p"] = mm(leaf, leaf)
        tick()
        for ch in chunks:
            xp = mm(jnp.concatenate([ch["x"], ch["p"]], axis=0), ch["p"])
            ch["x"] = ch["x"] + xp[:PAIR]
            ch["p"] = xp[PAIR:]
        tick()
        for ch in chunks:
            ch["x"] = ch["x"] + mm(ch["x"], ch["p"])
        tick()
        for lv in range(1, n_levels):
            size = INV_LEAF << (lv - 1)
            later = [(gi * SUB // size) % 2 == 1 for gi in range(PAIR // SUB)]
            keep = (later, [not k for k in later])

            def take(x):
                return jnp.concatenate([_take_groups(x[:, d * PAIR:(d + 1) * PAIR], keep[d]) for d in range(2)],
                                       axis=1)

            def put(y, minuend=None):
                return jnp.concatenate(
                    [_put_groups(y[:, d * PAIR:(d + 1) * PAIR], keep[d],
                                 None if minuend is None else minuend[:, d * PAIR:(d + 1) * PAIR])
                     for d in range(2)], axis=1)

            for ch in chunks:
                ch["y"] = mm(take((ch["a"] * lvl_ref[lv]).astype(F32)), ch["x"])
            tick()
            for ch in chunks:
                ch["x"] = put(mm(take(ch["x"]), put(ch["y"])), minuend=ch["x"])
            tick()
        chains = []
        for ch in chunks:
            xb = ch["x"].astype(BF16)
            for d, cd in enumerate(ch["dirs"]):
                cd["uw"] = _dot(xb[:, d * PAIR:(d + 1) * PAIR], cd["rhs"]).astype(BF16)
                chains.append(cd)
        tick()
        for cd in chains:
            wu = jnp.concatenate([cd["uw"][:, GDN_HD:], cd["uw"][:, :GDN_HD]], axis=1)
            cd["nc"] = [_dot_tn(cd["kd"][h * CHUNK:(h + 1) * CHUNK], wu[h * CHUNK:(h + 1) * CHUNK])
                        for h in range(2)]
            cd["qo"] = _dot(cd["qkm"], wu)
        for cd in chains:
            nc, qo, idx = cd["nc"], cd["qo"], cd["idx"]
            qeff = cd["qd"] - qo[:, :GDN_HD]
            lhs_s[idx] = jnp.concatenate([-nc[0][:, :GDN_HD], qeff[:CHUNK], -nc[1][:, :GDN_HD], qeff[CHUNK:]],
                                         axis=0).astype(BF16)
            c_s[idx] = jnp.concatenate([nc[0][:, GDN_HD:], nc[1][:, GDN_HD:]], axis=1)
            o0_s[idx] = qo[:, GDN_HD:].astype(o0_s.dtype)
        while steps:
            tick()

    def recur(step):
        chains = [dict(d=d, c=c, idx=d * N_CHUNKS + c) for d, c in step]
        rows_h = GDN_HD + CHUNK
        for ch in chains:
            ch["st"] = st_s[ch["d"]]
            stb = ch["st"].astype(BF16)
            ch["r"] = [_dot(lhs_s[ch["idx"], h * rows_h:(h + 1) * rows_h], stb[:, h * GDN_HD:(h + 1) * GDN_HD])
                       for h in range(2)]
        for ch in chains:
            r, idx = ch["r"], ch["idx"]
            ns = jnp.concatenate([r[0][:GDN_HD], r[1][:GDN_HD]], axis=1)
            st_s[ch["d"]] = ch["st"] * dec_s[idx][0:1] + ns + c_s[idx]
            o0 = o0_s[idx]
            r0 = ch["c"] * CHUNK
            acc_s[r0:r0 + CHUNK, 0:GDN_HD] += r[0][GDN_HD:] + o0[:CHUNK]
            acc_s[r0:r0 + CHUNK, GDN_HD:] += r[1][GDN_HD:] + o0[CHUNK:]

    st_s[...] = jnp.zeros_like(st_s)
    acc_s[...] = jnp.zeros_like(acc_s)

    order_f = list(range(N_CHUNKS))
    order_b = list(range(CTX_CHUNKS - 1, -1, -1)) + list(range(N_CHUNKS - 1, CTX_CHUNKS - 1, -1))
    group_order, n_groups = GDN_GROUPS, len(GDN_GROUPS)
    stages = (stage_a, stage_b)
    solved, pos = set(), [0, 0]

    def ready_steps():
        out = []
        while True:
            step = [(d, order[pos[d]]) for d, order in enumerate((order_f, order_b))
                    if pos[d] < N_CHUNKS and order[pos[d]] in solved]
            if not step:
                return out
            for d, _ in step:
                pos[d] += 1
            out.append(step)

    prepare(group_order[0], stages[0])
    for n, grp in enumerate(group_order):
        if n + 1 < n_groups:
            prepare(group_order[n + 1], stages[(n + 1) % 2])
        solve(grp, stages[n % 2], ready_steps())
        solved.update(range(grp[0], grp[0] + grp[1]))
    for step in ready_steps():
        recur(step)

    def finish(t, carry):
        r0 = pl.multiple_of(t * ROW_TILE, ROW_TILE)
        for h in range(2):
            sl = slice(h * GDN_HD, (h + 1) * GDN_HD)
            x = acc_s[pl.ds(r0, ROW_TILE), sl]
            o_ref[0, pl.ds(r0, ROW_TILE), sl] = (
                x * lax.rsqrt(jnp.mean(x * x, axis=-1, keepdims=True) + EPS) * gn_ref[...]).astype(o_ref.dtype)
        return carry

    lax.fori_loop(0, N_ROW_TILES, finish, 0, unroll=3)


def _gdn_mixer(p_main, p_ab, conv_w, a_log, dt_bias, g_norm):
    mats, lvl = _gdn_consts()
    bsz = p_main.shape[0]
    ab = p_ab.reshape(bsz, N_CHUNKS, CHUNK, 2, 2, GDN_QK_HEADS, 2)
    abr = ab.transpose(0, 5, 1, 4, 3, 6, 2).reshape(bsz, GDN_QK_HEADS, N_CHUNKS, 4, PAIR)
    prm = jnp.stack([a_log, dt_bias], axis=0).astype(F32).reshape(2, 2, GDN_QK_HEADS, 2)
    zero = jnp.zeros_like(prm[:, 0])
    rows = jnp.stack([prm[:, 0], prm[:, 0], prm[:, 1], prm[:, 1]] + [zero] * (SUB - 4), axis=1)
    pr = jnp.repeat(rows.transpose(2, 0, 1, 3), CHUNK, axis=3)

    grid = (bsz, GDN_QK_HEADS)
    kq = GDN_KEY_DIM // GDN_HD
    in_specs = [
        pl.BlockSpec((1, L_ALL, GDN_HD), lambda b, j: (b, 0, j)),
        pl.BlockSpec((1, L_ALL, GDN_HD), lambda b, j: (b, 0, kq + j)),
        pl.BlockSpec((1, L_ALL, 2 * GDN_HD), lambda b, j: (b, 0, kq + j)),
        pl.BlockSpec((4, GDN_HD), lambda b, j: (0, j)),
        pl.BlockSpec((4, GDN_HD), lambda b, j: (0, kq + j)),
        pl.BlockSpec((4, 2 * GDN_HD), lambda b, j: (0, kq + j)),
        pl.BlockSpec((1, 1, N_CHUNKS, 4, PAIR), lambda b, j: (b, j, 0, 0, 0)),
        pl.BlockSpec((1, 2, SUB, PAIR), lambda b, j: (j, 0, 0, 0)),
        pl.BlockSpec(mats.shape, lambda b, j: (0, 0, 0)),
        pl.BlockSpec(lvl.shape, lambda b, j: (0, 0, 0)),
        pl.BlockSpec((1, GDN_HD), lambda b, j: (0, 0)),
    ]
    return pl.pallas_call(
        _gdn_body,
        grid=grid,
        in_specs=in_specs,
        out_specs=pl.BlockSpec((1, L_ALL, 2 * GDN_HD), lambda b, j: (b, 0, j)),
        out_shape=jax.ShapeDtypeStruct((bsz, L_ALL, GDN_VAL_DIM), ACT),
        scratch_shapes=[
            pltpu.VMEM((4, L_ALL + 3 * SUB, GDN_HD), F32),
            pltpu.VMEM((2 * N_CHUNKS, 3 * GDN_HD, GDN_HD), BF16),
            pltpu.VMEM((2 * N_CHUNKS, GDN_HD, 2 * GDN_HD), F32),
            pltpu.VMEM((2 * N_CHUNKS, PAIR, GDN_HD), BF16),
            pltpu.VMEM((2 * N_CHUNKS, SUB, 2 * GDN_HD), F32),
            pltpu.VMEM((2, GDN_HD, 2 * GDN_HD), F32),
            pltpu.VMEM((L_ALL, 2 * GDN_HD), F32),
        ] + 2 * [
            pltpu.VMEM((A_GROUP, PAIR, 2 * PAIR), BF16),
            pltpu.VMEM((2 * A_GROUP, PAIR, 2 * GDN_HD), BF16),
            pltpu.VMEM((2 * A_GROUP, PAIR, GDN_HD), F32),
            pltpu.VMEM((2 * A_GROUP, PAIR, GDN_HD), BF16),
            pltpu.VMEM((2 * A_GROUP, PAIR, PAIR), BF16),
        ],
        compiler_params=_cparams(("parallel", "parallel")),
    )(p_main, p_main, p_main, conv_w, conv_w, conv_w, abr, pr, mats, lvl,
      g_norm.reshape(1, GDN_HD))


def _lru_body(u_ref, cw_ref, cb_ref, wra_ref, wri_ref, bra_ref, bri_ref, lam_ref, o_ref, a_s, b_s, acc_s):
    full = slice(None)
    n_steps = L_ALL // LRU_ROWS

    def gates(c, carry):
        r0 = pl.multiple_of(c * LRU_ROWS, LRU_ROWS)
        left_ok = jnp.where(c >= 2, 1.0, 0.0)
        right_ok = jnp.where((c == 0) | (c == n_steps - 1), 0.0, 1.0)
        u = _conv_rows(u_ref, r0, LRU_ROWS, full, cw_ref[...], left_ok, right_ok) + cb_ref[...]
        ub = u.astype(BF16)
        blocks = [slice(n * LRU_BLOCK, (n + 1) * LRU_BLOCK) for n in range(LRU_LANES // LRU_BLOCK)]
        for d in range(2):
            r = _sigmoid(jnp.concatenate([_dot(ub[:, sl], wra_ref[d, n]) for n, sl in enumerate(blocks)],
                                               axis=1) + bra_ref[d:d + 1])
            i = _sigmoid(jnp.concatenate([_dot(ub[:, sl], wri_ref[d, n]) for n, sl in enumerate(blocks)],
                                               axis=1) + bri_ref[d:d + 1])
            log_a = -LRU_C * r * _softplus(-lam_ref[d:d + 1])
            a = jnp.exp(log_a)
            one_m_a2 = -jnp.tanh(log_a) * (a * a + 1.0)
            a_s[d, pl.ds(r0, LRU_ROWS), :] = a
            b_s[d, pl.ds(r0, LRU_ROWS), :] = jnp.sqrt(one_m_a2) * (i * u)
        return carry

    lax.fori_loop(0, n_steps, gates, 0)

    acc_s[...] = jnp.zeros_like(acc_s)
    row = lax.broadcasted_iota(jnp.int32, (SUB, LRU_LANES), 0)

    def scan(g, carry):
        hf, hb = carry
        gb = jnp.where(g < CTX_GROUPS, CTX_GROUPS - 1 - g, N_GROUPS + CTX_GROUPS - 1 - g)
        r0 = pl.multiple_of(g * SUB, SUB)
        a, b = a_s[0, pl.ds(r0, SUB), :], b_s[0, pl.ds(r0, SUB), :]
        for s in (1, 2, 4):
            ok = row >= s
            b = jnp.where(ok, a * pltpu.roll(b, s, 0) + b, b)
            a = jnp.where(ok, a * pltpu.roll(a, s, 0), a)
        h = b + a * hf
        acc_s[pl.ds(r0, SUB), :] += h
        hf = h[SUB - 1:SUB]
        r0 = pl.multiple_of(gb * SUB, SUB)
        a, b = a_s[1, pl.ds(r0, SUB), :], b_s[1, pl.ds(r0, SUB), :]
        for s in (1, 2, 4):
            ok = row < SUB - s
            b = jnp.where(ok, a * pltpu.roll(b, SUB - s, 0) + b, b)
            a = jnp.where(ok, a * pltpu.roll(a, SUB - s, 0), a)
        h = b + a * hb
        acc_s[pl.ds(r0, SUB), :] += h
        hb = h[0:1]
        return hf, hb

    zero = jnp.zeros((1, LRU_LANES), F32)
    lax.fori_loop(0, N_GROUPS, scan, (zero, zero), unroll=4)

    def emit(t, carry):
        r0 = pl.multiple_of(t * LRU_ROWS, LRU_ROWS)
        o_ref[0, pl.ds(r0, LRU_ROWS), :] = acc_s[pl.ds(r0, LRU_ROWS), :].astype(o_ref.dtype)
        return carry

    lax.fori_loop(0, n_steps, emit, 0)


def _lru_mixer(p, conv_w, conv_b, w_ra, w_ri, b_ra, b_ri, lam):
    nb = LRU_WIDTH // LRU_LANES
    per = LRU_LANES // LRU_BLOCK
    bsz = p.shape[0]
    vec = lambda rows: pl.BlockSpec((rows, LRU_LANES), lambda b, n: (0, n))
    wspec = pl.BlockSpec((2, per, LRU_BLOCK, LRU_BLOCK), lambda b, n: (0, n, 0, 0))
    return pl.pallas_call(
        _lru_body,
        grid=(bsz, nb),
        in_specs=[
            pl.BlockSpec((1, L_ALL, LRU_LANES), lambda b, n: (b, 0, nb + n)),
            vec(4), vec(1), wspec, wspec, vec(2), vec(2), vec(2),
        ],
        out_specs=pl.BlockSpec((1, L_ALL, LRU_LANES), lambda b, n: (b, 0, n)),
        out_shape=jax.ShapeDtypeStruct((bsz, L_ALL, LRU_WIDTH), ACT),
        scratch_shapes=[pltpu.VMEM((2, L_ALL, LRU_LANES), F32), pltpu.VMEM((2, L_ALL, LRU_LANES), F32),
                        pltpu.VMEM((L_ALL, LRU_LANES), F32)],
        compiler_params=_cparams(("parallel", "parallel")),
    )(p, conv_w, conv_b.reshape(1, LRU_WIDTH), w_ra.astype(BF16), w_ri.astype(BF16), b_ra, b_ri, lam)


def _att_body(sink_ref, q_ref, kc_ref, vc_ref, k0_ref, k1_ref, k2_ref, v0_ref, v1_ref, v2_ref, o_ref):
    i = pl.program_id(1)
    n_ctx_blocks = CTX_LEN // ATT_BLOCK
    li = i - n_ctx_blocks
    qpos = li * ATT_BLOCK + lax.broadcasted_iota(jnp.int32, (ATT_BLOCK, 3 * ATT_BLOCK), 0)
    kpos = (li - 1) * ATT_BLOCK + lax.broadcasted_iota(jnp.int32, (ATT_BLOCK, 3 * ATT_BLOCK), 1)
    band = (jnp.abs(qpos - kpos) <= ATT_WINDOW) & (kpos >= 0) & (kpos < SEQ) & (li >= 0)
    bias = jnp.concatenate([jnp.zeros((ATT_BLOCK, CTX_LEN), F32), jnp.where(band, 0.0, NEG)], axis=1)
    bias4 = jnp.concatenate([bias] * ATT_GROUP, axis=0)
    lane = lax.broadcasted_iota(jnp.int32, (ATT_BLOCK, ATT_QW), 1)
    qmask = [(lane % 128) // 32 == g for g in range(ATT_GROUP)]
    omask = [lane // ATT_HD == g for g in range(ATT_GROUP)]
    rowg = lax.broadcasted_iota(jnp.int32, (ATT_GROUP * ATT_BLOCK, 1), 0) // ATT_BLOCK
    def scores(h):
        sl = slice(h * ATT_QW, (h + 1) * ATT_QW)
        qh = q_ref[0, :, sl]
        qm = jnp.concatenate([jnp.where(qmask[g], qh, 0.0) for g in range(ATT_GROUP)], axis=0).astype(BF16)
        keys = jnp.concatenate([kc_ref[0, :, sl], k0_ref[0, :, sl], k1_ref[0, :, sl], k2_ref[0, :, sl]],
                               axis=0).astype(BF16)
        return _dot_nt(qm, keys)

    s_next = scores(0)
    for h in range(ATT_KV_HEADS):
        sl = slice(h * ATT_QW, (h + 1) * ATT_QW)
        s = s_next + bias4
        if h + 1 < ATT_KV_HEADS:
            s_next = scores(h + 1)
        vals = jnp.concatenate([vc_ref[0, :, sl], v0_ref[0, :, sl], v1_ref[0, :, sl], v2_ref[0, :, sl]],
                               axis=0).astype(BF16)
        sink = jnp.zeros((ATT_GROUP * ATT_BLOCK, 1), F32)
        for g in range(ATT_GROUP):
            sink = jnp.where(rowg == g, sink_ref[h * ATT_GROUP + g], sink)
        mx = jnp.maximum(jnp.max(s, axis=-1, keepdims=True), sink)
        e = jnp.exp(s - mx)
        den = jnp.sum(e, axis=-1, keepdims=True) + jnp.exp(sink - mx)
        r = _dot(e.astype(BF16), vals) / den
        out = jnp.zeros((ATT_BLOCK, ATT_QW), F32)
        for g in range(ATT_GROUP):
            out = jnp.where(omask[g], r[g * ATT_BLOCK:(g + 1) * ATT_BLOCK], out)
        o_ref[0, :, sl] = out.astype(o_ref.dtype)


def _att_mixer(p, sinks):
    n_ctx_blocks = CTX_LEN // ATT_BLOCK
    n_lat_blocks = SEQ // ATT_BLOCK
    bsz = p.shape[0]

    def win(off, col):
        def index(b, i, sink_ref):
            blk = jnp.clip(i - n_ctx_blocks + off, 0, n_lat_blocks - 1)
            return (b, n_ctx_blocks + blk, col)
        return pl.BlockSpec((1, ATT_BLOCK, D_MODEL), index)

    grid_spec = pltpu.PrefetchScalarGridSpec(
        num_scalar_prefetch=1,
        grid=(bsz, L_ALL // ATT_BLOCK),
        in_specs=[
            pl.BlockSpec((1, ATT_BLOCK, D_MODEL), lambda b, i, s: (b, i, 0)),
            pl.BlockSpec((1, CTX_LEN, D_MODEL), lambda b, i, s: (b, 0, 2)),
            pl.BlockSpec((1, CTX_LEN, D_MODEL), lambda b, i, s: (b, 0, 3)),
            win(-1, 2), win(0, 2), win(1, 2), win(-1, 3), win(0, 3), win(1, 3),
        ],
        out_specs=pl.BlockSpec((1, ATT_BLOCK, D_MODEL), lambda b, i, s: (b, i, 0)),
    )
    return pl.pallas_call(
        _att_body,
        grid_spec=grid_spec,
        out_shape=jax.ShapeDtypeStruct((bsz, L_ALL, D_MODEL), ACT),
        compiler_params=_cparams(("parallel", "arbitrary")),
    )(sinks.astype(F32), p, p, p, p, p, p, p, p, p)


def _att_weight_layout(w_in):
    half = ATT_HD // 2
    q_cols, k_cols, v_cols = [], [], []
    for h in range(ATT_KV_HEADS):
        for part in range(2):
            for g in range(ATT_GROUP):
                base = (h * ATT_GROUP + g) * ATT_HD + part * half
                q_cols += list(range(base, base + half))
                kb = D_MODEL + h * ATT_HD + part * half
                k_cols += list(range(kb, kb + half))
        vb = D_MODEL + ATT_KV_HEADS * ATT_HD + h * ATT_HD
        v_cols += list(range(vb, vb + ATT_HD)) * ATT_GROUP
    z0 = D_MODEL + 2 * ATT_KV_HEADS * ATT_HD
    z_cols = list(range(z0, z0 + D_MODEL))
    order = np.asarray(q_cols + z_cols + k_cols + v_cols, np.int32)
    scale = np.ones((ATT_COLS,), np.float32)
    scale[:D_MODEL] = ATT_HD ** -0.5
    return (w_in[:, order] * scale).astype(BF16)


def _rope_tables():
    rows = SEQ // GRID_W
    row = jnp.repeat(jnp.arange(rows), GRID_W)
    col = jnp.tile(jnp.arange(GRID_W), rows)
    n_freq = ATT_HD // 4
    inv = 10000.0 ** (-jnp.arange(n_freq, dtype=F32) / n_freq)
    ang = jnp.concatenate([row[:, None] * inv, col[:, None] * inv], axis=-1)
    cos = jnp.concatenate([jnp.ones((CTX_LEN, ATT_HD // 2), F32), jnp.cos(ang)], axis=0)
    sin = jnp.concatenate([jnp.zeros((CTX_LEN, ATT_HD // 2), F32), jnp.sin(ang)], axis=0)
    return jnp.tile(cos, (1, ATT_GROUP)), jnp.tile(sin, (1, ATT_GROUP))


def kernel(x, c, ctx, c_ctx, ada_w, ada_b, norm_pre, norm_post, gdn_w_in, gdn_conv_w, gdn_a_log, gdn_dt_bias, gdn_g_norm, gdn_w_out, lru_w_in, lru_conv_w, lru_conv_b, lru_w_ra, lru_b_ra, lru_w_ri, lru_b_ri, lru_lam, lru_w_out, att_w_in, att_sinks, att_w_out):
    cc = jnp.zeros((MOD_ROWS, D_MODEL), F32).at[:c.shape[0]].set(c).at[CTX_MOD_ROW].set(c_ctx)
    mod_all = _modulation(cc, ada_w, ada_b).reshape(DEPTH, MOD_ROWS, 1, 3 * D_MODEL)
    xs = jnp.concatenate([ctx, x], axis=1)
    cos, sin = _rope_tables()

    def in_args(i):
        kind, j = i % 3, i // 3
        if kind == 0:
            return dict(w=gdn_w_in[j].astype(BF16), segs=[(0, GDN_MAIN), (GDN_MAIN, GDN_AB)],
                        out_dtypes=[ACT, F32])
        if kind == 1:
            return dict(w=lru_w_in[j].astype(BF16), segs=[(0, 2 * LRU_WIDTH)], out_dtypes=[ACT])
        return dict(w=_att_weight_layout(att_w_in[j]), segs=[(0, ATT_COLS)], out_dtypes=[ACT],
                    rope=(cos, sin, (0, 1, 4, 5)))

    a0 = in_args(0)
    proj = _in_proj(xs, mod_all[0], norm_pre[0], a0.pop("w"), **a0)
    for i in range(DEPTH):
        kind, j = i % 3, i // 3
        if kind == 0:
            o = _gdn_mixer(proj[0], proj[1], gdn_conv_w[j], gdn_a_log[j], gdn_dt_bias[j], gdn_g_norm[j])
            z_block, w_out = 2, gdn_w_out[j]
        elif kind == 1:
            o = _lru_mixer(proj[0], lru_conv_w[j], lru_conv_b[j], lru_w_ra[j], lru_w_ri[j],
                           lru_b_ra[j], lru_b_ri[j], lru_lam[j])
            z_block, w_out = 0, lru_w_out[j]
        else:
            o = _att_mixer(proj[0], att_sinks[j])
            z_block, w_out = 1, att_w_out[j]
        if i == DEPTH - 1:
            return _out_proj(o, proj[0], z_block, xs, mod_all[i], norm_post[i], w_out.astype(BF16),
                             latent_only=True)
        nxt = in_args(i + 1)
        xs, *proj = _out_in_proj(o, proj[0], z_block, xs, mod_all[i], norm_post[i], w_out.astype(BF16),
                                 mod_all[i + 1], norm_pre[i + 1], nxt.pop("w"), **nxt)
```

```python
import functools

import numpy as np
import jax
import jax.numpy as jnp
from jax import lax
from jax.experimental import pallas as pl
from jax.experimental.pallas import tpu as pltpu

F32 = jnp.float32
BF16 = jnp.bfloat16
ACT = jnp.bfloat16

D_MODEL = 1024
BATCH = 8
SEQ = 2048
CTX_LEN = 256
L_ALL = CTX_LEN + SEQ
DEPTH = 4
GRID_W = 64
EPS = 1e-6

ROW_TILE = 256
N_ROW_TILES = L_ALL // ROW_TILE
MOD_ROWS = 16
CTX_MOD_ROW = BATCH
VMEM_LIMIT = 56 * 1024 * 1024

GDN_HD = 128
GDN_QK_HEADS = 8
GDN_V_HEADS = 16
GDN_KEY_DIM = 1024
GDN_VAL_DIM = 2048
GDN_MAIN = 2 * GDN_KEY_DIM + 2 * GDN_VAL_DIM
GDN_AB = 64
CHUNK = 64
N_CHUNKS = L_ALL // CHUNK
CTX_CHUNKS = CTX_LEN // CHUNK
PAIR = 2 * CHUNK
GDN_GROUPS = ((0, 4), (28, 8), (4, 8), (22, 6), (12, 6), (18, 4))
A_GROUP = max(n for _, n in GDN_GROUPS)
assert sorted(c for f, n in GDN_GROUPS for c in range(f, f + n)) == list(range(N_CHUNKS))
INV_LEAF = 8

LRU_WIDTH = 1024
LRU_BLOCK = 128
LRU_LANES = 256
LRU_C = 8.0
LRU_ROWS = 256
SUB = 8
N_GROUPS = L_ALL // SUB
CTX_GROUPS = CTX_LEN // SUB

ATT_HD = 64
ATT_KV_HEADS = 4
ATT_GROUP = 4
ATT_BLOCK = 128
ATT_WINDOW = 128
ATT_QW = ATT_GROUP * ATT_HD
ATT_COLS = 4 * D_MODEL
NEG = -1e30


def _silu(x):
    return x * _sigmoid(x)


def _sigmoid(x):
    return 0.5 * jnp.tanh(0.5 * x) + 0.5


def _softplus(x):
    return jnp.maximum(x, 0.0) + jnp.log1p(jnp.exp(-jnp.abs(x)))


def _cparams(sem):
    return pltpu.CompilerParams(dimension_semantics=sem, vmem_limit_bytes=VMEM_LIMIT)


def _dot(a, b):
    return jnp.dot(a, b, preferred_element_type=F32)


def _dot_nt(a, b):
    return lax.dot_general(a, b, (((1,), (1,)), ((), ())), preferred_element_type=F32)


def _dot_tn(a, b):
    return lax.dot_general(a, b, (((0,), (0,)), ((), ())), preferred_element_type=F32)


def _mod_body(c_ref, w_ref, b_ref, o_ref):
    h = _silu(c_ref[...]).astype(BF16)
    o_ref[0] = _dot(h, w_ref[0].astype(BF16)) + b_ref[0]


def _modulation(cc, ada_w, ada_b):
    n_col = 3
    return pl.pallas_call(
        _mod_body,
        grid=(DEPTH, n_col),
        in_specs=[
            pl.BlockSpec((MOD_ROWS, D_MODEL), lambda i, n: (0, 0)),
            pl.BlockSpec((1, D_MODEL, D_MODEL), lambda i, n: (i, 0, n)),
            pl.BlockSpec((1, 1, D_MODEL), lambda i, n: (i, 0, n)),
        ],
        out_specs=pl.BlockSpec((1, MOD_ROWS, D_MODEL), lambda i, n: (i, 0, n)),
        out_shape=jax.ShapeDtypeStruct((DEPTH, MOD_ROWS, 3 * D_MODEL), F32),
        compiler_params=_cparams(("parallel", "parallel")),
    )(cc, ada_w, ada_b.reshape(DEPTH, 1, 3 * D_MODEL))


def _mod_index(b, t):
    return (jnp.where(t == 0, CTX_MOD_ROW, b), 0, 0)


def _stream_tile(ctx_ref, x_ref):
    return jnp.where(pl.program_id(1) == 0, ctx_ref[0], x_ref[0])


def _stream_specs():
    return [pl.BlockSpec((1, ROW_TILE, D_MODEL), lambda b, t: (b, 0, 0)),
            pl.BlockSpec((1, ROW_TILE, D_MODEL), lambda b, t: (b, jnp.maximum(t - 1, 0), 0))]


def _in_proj_body(*refs, segs, rope_groups, split):
    x = _stream_tile(refs[0], refs[1]) if split else refs[0][0]
    mod_ref, nw_ref, w_ref = refs[1 + split:4 + split]
    _project(x, mod_ref, nw_ref, w_ref, refs[4 + split:], segs, rope_groups)


def _project(x, mod_ref, nw_ref, w_ref, refs, segs, rope_groups):
    if rope_groups:
        cos_ref, sin_ref = refs[:2]
        refs = refs[2:]
    y = x * lax.rsqrt(jnp.mean(x * x, axis=-1, keepdims=True) + EPS) * nw_ref[...]
    shift = mod_ref[0, :, 0:D_MODEL]
    scale = mod_ref[0, :, D_MODEL:2 * D_MODEL]
    h = (y * (1.0 + scale) + shift).astype(BF16)
    for ref, (c0, width) in zip(refs, segs):
        step = min(width, 512)
        for n0 in range(0, width, step):
            acc = _dot(h, w_ref[:, c0 + n0:c0 + n0 + step])
            if (c0 + n0) // 512 in rope_groups:
                cos = cos_ref[...]
                sin = sin_ref[...]
                parts = []
                for g0 in range(0, step, 256):
                    x1 = acc[:, g0:g0 + 128]
                    x2 = acc[:, g0 + 128:g0 + 256]
                    parts += [x1 * cos - x2 * sin, x2 * cos + x1 * sin]
                acc = jnp.concatenate(parts, axis=1)
            ref[0, :, n0:n0 + step] = acc.astype(ref.dtype)


def _in_proj(x, mod, nw, w, segs, out_dtypes, rope=None):
    n_total = w.shape[1]
    split = isinstance(x, tuple)
    in_specs = (_stream_specs() if split else [pl.BlockSpec((1, ROW_TILE, D_MODEL), lambda b, t: (b, t, 0))]) + [
        pl.BlockSpec((1, 1, 3 * D_MODEL), _mod_index),
        pl.BlockSpec((1, D_MODEL), lambda b, t: (0, 0)),
        pl.BlockSpec((D_MODEL, n_total), lambda b, t: (0, 0), pipeline_mode=pl.Buffered(1)),
    ]
    args = (list(x) if split else [x]) + [mod, nw.reshape(1, D_MODEL), w]
    x = x[0] if split else x
    rope_groups = ()
    if rope is not None:
        cos, sin, rope_groups = rope
        in_specs += [pl.BlockSpec((ROW_TILE, 128), lambda b, t: (t, 0))] * 2
        args += [cos, sin]
    out_specs = [pl.BlockSpec((1, ROW_TILE, width), lambda b, t: (b, t, 0)) for _, width in segs]
    bsz = x.shape[0]
    out_shape = [jax.ShapeDtypeStruct((bsz, L_ALL, width), dt) for (_, width), dt in zip(segs, out_dtypes)]
    return pl.pallas_call(
        functools.partial(_in_proj_body, segs=tuple(segs), rope_groups=tuple(rope_groups), split=split),
        grid=(bsz, N_ROW_TILES),
        in_specs=in_specs,
        out_specs=out_specs,
        out_shape=out_shape,
        compiler_params=_cparams(("parallel", "parallel")),
    )(*args)


def _out_proj_body(o_ref, z_ref, x_ref, mod_ref, nw_ref, w_ref, out_ref):
    g = (o_ref[0].astype(F32) * _silu(z_ref[0].astype(F32))).astype(BF16)
    y = _dot(g, w_ref[...])
    yn = y * lax.rsqrt(jnp.mean(y * y, axis=-1, keepdims=True) + EPS) * nw_ref[...]
    gate = mod_ref[0, :, 2 * D_MODEL:3 * D_MODEL]
    out_ref[0] = x_ref[0] + gate * yn


def _out_proj(o, z_arr, z_block, x, mod, nw, w, latent_only=False):
    width = o.shape[-1]
    bsz = o.shape[0]
    skip = CTX_LEN // ROW_TILE if latent_only else 0
    return pl.pallas_call(
        _out_proj_body,
        grid=(bsz, N_ROW_TILES - skip),
        in_specs=[
            pl.BlockSpec((1, ROW_TILE, width), lambda b, t: (b, t + skip, 0)),
            pl.BlockSpec((1, ROW_TILE, width), lambda b, t: (b, t + skip, z_block)),
            pl.BlockSpec((1, ROW_TILE, D_MODEL), lambda b, t: (b, t + skip, 0)),
            pl.BlockSpec((1, 1, 3 * D_MODEL), lambda b, t: _mod_index(b, t + skip)),
            pl.BlockSpec((1, D_MODEL), lambda b, t: (0, 0)),
            pl.BlockSpec((width, D_MODEL), lambda b, t: (0, 0), pipeline_mode=pl.Buffered(1)),
        ],
        out_specs=pl.BlockSpec((1, ROW_TILE, D_MODEL), lambda b, t: (b, t, 0)),
        out_shape=jax.ShapeDtypeStruct((bsz, L_ALL - skip * ROW_TILE, D_MODEL), F32),
        compiler_params=_cparams(("parallel", "parallel")),
    )(o, z_arr, x, mod, nw.reshape(1, D_MODEL), w)


def _out_in_body(o_ref, z_ref, *refs, segs, rope_groups, split):
    x = _stream_tile(refs[0], refs[1]) if split else refs[0][0]
    mod_ref, nw_post_ref, w_out_ref, mod_next_ref, nw_pre_ref, w_in_ref = refs[1 + split:7 + split]
    refs = refs[7 + split:]
    n_extra = 2 if rope_groups else 0
    extra, out_ref, p_refs = refs[:n_extra], refs[n_extra], refs[n_extra + 1:]
    g = (o_ref[0].astype(F32) * _silu(z_ref[0].astype(F32))).astype(BF16)
    y = _dot(g, w_out_ref[...])
    yn = y * lax.rsqrt(jnp.mean(y * y, axis=-1, keepdims=True) + EPS) * nw_post_ref[...]
    x_new = x + mod_ref[0, :, 2 * D_MODEL:3 * D_MODEL] * yn
    out_ref[0] = x_new
    _project(x_new, mod_next_ref, nw_pre_ref, w_in_ref, tuple(extra) + tuple(p_refs), segs, rope_groups)


def _out_in_proj(o, z_arr, z_block, x, mod, nw_post, w_out, mod_next, nw_pre, w_in, segs, out_dtypes, rope=None):
    width = o.shape[-1]
    bsz = o.shape[0]
    row = lambda cols: pl.BlockSpec((1, ROW_TILE, cols), lambda b, t: (b, t, 0))
    const = lambda shape: pl.BlockSpec(shape, lambda b, t: (0,) * len(shape), pipeline_mode=pl.Buffered(1))
    split = isinstance(x, tuple)
    in_specs = [
        row(width),
        pl.BlockSpec((1, ROW_TILE, width), lambda b, t: (b, t, z_block)),
    ] + (_stream_specs() if split else [row(D_MODEL)]) + [
        pl.BlockSpec((1, 1, 3 * D_MODEL), _mod_index),
        pl.BlockSpec((1, D_MODEL), lambda b, t: (0, 0)),
        const((width, D_MODEL)),
        pl.BlockSpec((1, 1, 3 * D_MODEL), _mod_index),
        pl.BlockSpec((1, D_MODEL), lambda b, t: (0, 0)),
        const((D_MODEL, w_in.shape[1])),
    ]
    args = ([o, z_arr] + (list(x) if split else [x])
            + [mod, nw_post.reshape(1, D_MODEL), w_out, mod_next, nw_pre.reshape(1, D_MODEL), w_in])
    rope_groups = ()
    if rope is not None:
        cos, sin, rope_groups = rope
        in_specs += [pl.BlockSpec((ROW_TILE, 128), lambda b, t: (t, 0))] * 2
        args += [cos, sin]
    out_specs = [row(D_MODEL)] + [row(w) for _, w in segs]
    out_shape = ([jax.ShapeDtypeStruct((bsz, L_ALL, D_MODEL), F32)]
                 + [jax.ShapeDtypeStruct((bsz, L_ALL, w), dt) for (_, w), dt in zip(segs, out_dtypes)])
    return pl.pallas_call(
        functools.partial(_out_in_body, segs=tuple(segs), rope_groups=tuple(rope_groups), split=split),
        grid=(bsz, N_ROW_TILES),
        in_specs=in_specs,
        out_specs=out_specs,
        out_shape=out_shape,
        compiler_params=_cparams(("parallel", "parallel")),
    )(*args)


def _conv_rows(ref, r0, rows, width_sl, cw, left_ok, right_ok):
    total = ref.shape[1]
    halo = 2 * SUB
    main = ref[0, pl.ds(r0, rows), width_sl].astype(F32)
    prev = ref[0, pl.ds(pl.multiple_of(jnp.maximum(r0 - halo, 0), halo), halo), width_sl].astype(F32) * left_ok
    nxt = ref[0, pl.ds(pl.multiple_of(jnp.minimum(r0 + rows, total - halo), halo), halo),
              width_sl].astype(F32) * right_ok
    xw = jnp.concatenate([prev, main, nxt], axis=0)
    return (cw[0:1] * xw[halo - 2:halo - 2 + rows] + cw[1:2] * xw[halo - 1:halo - 1 + rows]
            + cw[2:3] * xw[halo:halo + rows] + cw[3:4] * xw[halo + 1:halo + 1 + rows])


def _gdn_consts():
    i = np.arange(PAIR)
    same = (i[:, None] // CHUNK) == (i[None, :] // CHUNK)
    t_i, t_m = i[:, None] % CHUNK, i[None, :] % CHUNK
    low = same & (t_m <= t_i)
    upp = same & (t_m >= t_i)
    both = lambda f, b: np.concatenate([f, b], axis=1)
    eye = np.eye(PAIR, dtype=np.float32)
    mats = np.stack([
        both(np.where(low, 0.0, NEG), np.where(upp, 0.0, NEG)),
        both(same & (t_m < t_i), same & (t_m > t_i)).astype(np.float32),
        both(eye, eye)]).astype(np.float32)
    levels = [(i[:, None] // INV_LEAF) == (i[None, :] // INV_LEAF)]
    size = INV_LEAF
    while size < CHUNK:
        levels.append(((i[:, None] // (2 * size)) == (i[None, :] // (2 * size)))
                      & ((i[:, None] // size) != (i[None, :] // size)))
        size *= 2
    lvl = np.stack([both(m, m) for m in levels]).astype(np.float32)
    return jnp.asarray(mats, F32), jnp.asarray(lvl, BF16)


def _block_diag(x):
    zero = jnp.zeros((PAIR, PAIR), x.dtype)
    return jnp.concatenate([jnp.concatenate([x[:, :PAIR], zero], axis=1),
                            jnp.concatenate([zero, x[:, PAIR:]], axis=1)], axis=0)


def _take_groups(x, keep):
    return jnp.concatenate([x[g * SUB:(g + 1) * SUB] for g, k in enumerate(keep) if k], axis=0)


def _put_groups(y, keep, minuend=None):
    out, j = [], 0
    for g, k in enumerate(keep):
        base = None if minuend is None else minuend[g * SUB:(g + 1) * SUB]
        if k:
            piece = y[j * SUB:(j + 1) * SUB]
            out.append(piece if base is None else base - piece)
            j += 1
        else:
            out.append(jnp.zeros((SUB, y.shape[1]), y.dtype) if base is None else base)
    return jnp.concatenate(out, axis=0)


def _gdn_body(q_ref, k_ref, v_ref, cwq_ref, cwk_ref, cwv_ref, abr_ref, pr_ref,
              mats_ref, lvl_ref, gn_ref, o_ref,
              pad_s, lhs_s, c_s, o0_s, dec_s, st_s, acc_s, *stage_refs):
    pad_s[:, 0:SUB, :] = jnp.zeros((4, SUB, GDN_HD), F32)
    pad_s[:, SUB + CTX_LEN:2 * SUB + CTX_LEN, :] = jnp.zeros((4, SUB, GDN_HD), F32)
    pad_s[:, 2 * SUB + L_ALL:3 * SUB + L_ALL, :] = jnp.zeros((4, SUB, GDN_HD), F32)

    def fill(t, carry):
        src = pl.multiple_of(t * ROW_TILE, ROW_TILE)
        dst = pl.multiple_of(src + jnp.where(t == 0, SUB, 2 * SUB), SUB)
        pad_s[0, pl.ds(dst, ROW_TILE), :] = q_ref[0, pl.ds(src, ROW_TILE), :].astype(F32)
        pad_s[1, pl.ds(dst, ROW_TILE), :] = k_ref[0, pl.ds(src, ROW_TILE), :].astype(F32)
        pad_s[2, pl.ds(dst, ROW_TILE), :] = v_ref[0, pl.ds(src, ROW_TILE), 0:GDN_HD].astype(F32)
        pad_s[3, pl.ds(dst, ROW_TILE), :] = v_ref[0, pl.ds(src, ROW_TILE), GDN_HD:].astype(F32)
        return carry

    lax.fori_loop(0, N_ROW_TILES, fill, 0)
    stage_a, stage_b = stage_refs[:5], stage_refs[5:]
    t_lane = lax.broadcasted_iota(jnp.int32, (A_GROUP * SUB, PAIR), 1)
    t_in_chunk = t_lane % CHUNK

    def conv(which, base, cw):
        taps = [pad_s[which, pl.ds(base + (k - 2), CHUNK), :] for k in range(4)]
        return _silu(cw[0:1] * taps[0] + cw[1:2] * taps[1] + cw[2:3] * taps[2] + cw[3:4] * taps[3])

    def lanes2(f, b):
        return jnp.concatenate([jnp.broadcast_to(f, (PAIR, PAIR)), jnp.broadcast_to(b, (PAIR, PAIR))], axis=1)

    def group_gates(first, n):
        zero = jnp.zeros((SUB - 4, PAIR), F32)
        rows = [x for g in range(n) for x in (abr_ref[0, 0, first + g], zero)]
        rows += [jnp.zeros((SUB, PAIR), F32)] * (A_GROUP - n)
        xr = jnp.concatenate(rows, axis=0)
        g_r = -jnp.exp(jnp.tile(pr_ref[0, 0], (A_GROUP, 1))) * _softplus(xr + jnp.tile(pr_ref[0, 1], (A_GROUP, 1)))
        beta_r = _sigmoid(xr)
        lane, t_chunk = t_lane, t_in_chunk
        cum_f, cum_b = g_r, g_r
        step = 1
        while step < CHUNK:
            cum_f = cum_f + jnp.where(t_chunk >= step, pltpu.roll(cum_f, step, 1), 0.0)
            cum_b = cum_b + jnp.where(t_chunk < CHUNK - step, pltpu.roll(cum_b, PAIR - step, 1), 0.0)
            step *= 2
        tot = cum_f + cum_b - g_r
        other = pltpu.roll(tot, CHUNK, 1)
        tot_h0 = jnp.where(lane < CHUNK, tot, other)
        tot_h1 = jnp.where(lane < CHUNK, other, tot)
        pad = jnp.zeros((PAIR - A_GROUP * SUB, PAIR), F32)
        col = lambda x: jnp.concatenate([x, pad], axis=0).T
        return dict(cum_f=cum_f, cum_b=cum_b, cum_f_t=col(cum_f), cum_b_t=col(cum_b), tot_t=col(tot),
                    beta_t=col(beta_r), dec=jnp.exp(jnp.concatenate([tot_h0, tot_h1], axis=1)))

    def chunk_start(c, g, gt, stage):
        base = c * CHUNK + (SUB if c < CTX_CHUNKS else 2 * SUB)
        cwv = cwv_ref[...]
        qc = conv(0, base, cwq_ref[...])
        kc = conv(1, base, cwk_ref[...])
        v2 = jnp.concatenate([conv(2, base, cwv[:, :GDN_HD]), conv(3, base, cwv[:, GDN_HD:])], axis=0)
        qn = qc * (lax.rsqrt(jnp.sum(qc * qc, axis=-1, keepdims=True) + 1e-6) * (GDN_HD ** -0.5))
        kn = kc * lax.rsqrt(jnp.sum(kc * kc, axis=-1, keepdims=True) + 1e-6)
        q2 = jnp.concatenate([qn, qn], axis=0)
        k2 = jnp.concatenate([kn, kn], axis=0)
        gram = _dot_nt(jnp.concatenate([kn, qn], axis=0).astype(BF16), k2.astype(BF16))
        kk = jnp.concatenate([gram[:CHUNK], gram[:CHUNK]], axis=0)
        qk = jnp.concatenate([gram[CHUNK:], gram[CHUNK:]], axis=0)
        r = g * SUB
        gc_f, gc_b = gt["cum_f"][r:r + 1], gt["cum_b"][r + 2:r + 3]
        gcc_d = [gt["cum_f_t"][:, r:r + 1], gt["cum_b_t"][:, r + 2:r + 3]]
        totc_d = [gt["tot_t"][:, r:r + 1], gt["tot_t"][:, r + 2:r + 3]]
        bc_d = [gt["beta_t"][:, r + 1:r + 2], gt["beta_t"][:, r + 3:r + 4]]
        m = jnp.exp(lanes2(gcc_d[0], gcc_d[1]) - jnp.concatenate([gc_f, gc_b], axis=1) + mats_ref[0])
        a = (jnp.concatenate([kk, kk], axis=1) * m * mats_ref[1]
             * lanes2(bc_d[0], bc_d[1])).astype(BF16)
        qkm = (jnp.concatenate([qk, qk], axis=1) * m).astype(BF16)
        sa, srhs, sqd, skd, sqkm = stage
        sa[g] = a
        for d in range(2):
            gcc, totc, bc = gcc_d[d], totc_d[d], bc_d[d]
            egc = jnp.exp(gcc)
            dec_s[d * N_CHUNKS + c] = jnp.broadcast_to(gt["dec"][r + 2 * d:r + 2 * d + 1], (SUB, 2 * GDN_HD))
            srhs[2 * g + d] = jnp.concatenate([v2 * bc, k2 * (bc * egc)], axis=1).astype(BF16)
            sqd[2 * g + d] = q2 * egc
            skd[2 * g + d] = (k2 * jnp.exp(totc - gcc)).astype(BF16)
            sqkm[2 * g + d] = qkm[:, d * PAIR:(d + 1) * PAIR]

    def prepare(group, stage):
        first, n = group
        gt = group_gates(first, n)
        for g in range(n):
            chunk_start(first + g, g, gt, stage)

    def mm(p, q):
        return _dot(p.astype(BF16), _block_diag(q.astype(BF16)))

    def solve(group, stage, steps):
        sa, srhs, sqd, skd, sqkm = stage
        steps = list(steps)
        first, n_chunks = group

        def tick():
            if steps:
                recur(steps.pop(0))

        chunks = [dict(a=sa[g], dirs=[dict(idx=d * N_CHUNKS + first + g, rhs=srhs[2 * g + d],
                                           qd=sqd[2 * g + d], kd=skd[2 * g + d], qkm=sqkm[2 * g + d])
                                      for d in range(2)]) for g in range(n_chunks)]
        n_levels = lvl_ref.shape[0]
        for ch in chunks:
            leaf = ch["a"] * lvl_ref[0]
            ch["x"] = mats_ref[2] - leaf.astype(F32)
            ch["p"] = mm(leaf, leaf)
        tick()
        for ch in chunks:
            xp = mm(jnp.concatenate([ch["x"], ch["p"]], axis=0), ch["p"])
            ch["x"] = ch["x"] + xp[:PAIR]
            ch["p"] = xp[PAIR:]
        tick()
        for ch in chunks:
            ch["x"] = ch["x"] + mm(ch["x"], ch["p"])
        tick()
        for lv in range(1, n_levels):
            size = INV_LEAF << (lv - 1)
            later = [(gi * SUB // size) % 2 == 1 for gi in range(PAIR // SUB)]
            keep = (later, [not k for k in later])

            def take(x):
                return jnp.concatenate([_take_groups(x[:, d * PAIR:(d + 1) * PAIR], keep[d]) for d in range(2)],
                                       axis=1)

            def put(y, minuend=None):
                return jnp.concatenate(
                    [_put_groups(y[:, d * PAIR:(d + 1) * PAIR], keep[d],
                                 None if minuend is None else minuend[:, d * PAIR:(d + 1) * PAIR])
                     for d in range(2)], axis=1)

            for ch in chunks:
                ch["y"] = mm(take((ch["a"] * lvl_ref[lv]).astype(F32)), ch["x"])
            tick()
            for ch in chunks:
                ch["x"] = put(mm(take(ch["x"]), put(ch["y"])), minuend=ch["x"])
            tick()
        chains = []
        for ch in chunks:
            xb = ch["x"].astype(BF16)
            for d, cd in enumerate(ch["dirs"]):
                cd["uw"] = _dot(xb[:, d * PAIR:(d + 1) * PAIR], cd["rhs"]).astype(BF16)
                chains.append(cd)
        tick()
        for cd in chains:
            wu = jnp.concatenate([cd["uw"][:, GDN_HD:], cd["uw"][:, :GDN_HD]], axis=1)
            cd["nc"] = [_dot_tn(cd["kd"][h * CHUNK:(h + 1) * CHUNK], wu[h * CHUNK:(h + 1) * CHUNK])
                        for h in range(2)]
            cd["qo"] = _dot(cd["qkm"], wu)
        for cd in chains:
            nc, qo, idx = cd["nc"], cd["qo"], cd["idx"]
            qeff = cd["qd"] - qo[:, :GDN_HD]
            lhs_s[idx] = jnp.concatenate([-nc[0][:, :GDN_HD], qeff[:CHUNK], -nc[1][:, :GDN_HD], qeff[CHUNK:]],
                                         axis=0).astype(BF16)
            c_s[idx] = jnp.concatenate([nc[0][:, GDN_HD:], nc[1][:, GDN_HD:]], axis=1)
            o0_s[idx] = qo[:, GDN_HD:].astype(o0_s.dtype)
        while steps:
            tick()

    def recur(step):
        chains = [dict(d=d, c=c, idx=d * N_CHUNKS + c) for d, c in step]
        rows_h = GDN_HD + CHUNK
        for ch in chains:
            ch["st"] = st_s[ch["d"]]
            stb = ch["st"].astype(BF16)
            ch["r"] = [_dot(lhs_s[ch["idx"], h * rows_h:(h + 1) * rows_h], stb[:, h * GDN_HD:(h + 1) * GDN_HD])
                       for h in range(2)]
        for ch in chains:
            r, idx = ch["r"], ch["idx"]
            ns = jnp.concatenate([r[0][:GDN_HD], r[1][:GDN_HD]], axis=1)
            st_s[ch["d"]] = ch["st"] * dec_s[idx][0:1] + ns + c_s[idx]
            o0 = o0_s[idx]
            r0 = ch["c"] * CHUNK
            acc_s[r0:r0 + CHUNK, 0:GDN_HD] += r[0][GDN_HD:] + o0[:CHUNK]
            acc_s[r0:r0 + CHUNK, GDN_HD:] += r[1][GDN_HD:] + o0[CHUNK:]

    st_s[...] = jnp.zeros_like(st_s)
    acc_s[...] = jnp.zeros_like(acc_s)

    order_f = list(range(N_CHUNKS))
    order_b = list(range(CTX_CHUNKS - 1, -1, -1)) + list(range(N_CHUNKS - 1, CTX_CHUNKS - 1, -1))
    group_order, n_groups = GDN_GROUPS, len(GDN_GROUPS)
    stages = (stage_a, stage_b)
    solved, pos = set(), [0, 0]

    def ready_steps():
        out = []
        while True:
            step = [(d, order[pos[d]]) for d, order in enumerate((order_f, order_b))
                    if pos[d] < N_CHUNKS and order[pos[d]] in solved]
            if not step:
                return out
            for d, _ in step:
                pos[d] += 1
            out.append(step)

    prepare(group_order[0], stages[0])
    for n, grp in enumerate(group_order):
        if n + 1 < n_groups:
            prepare(group_order[n + 1], stages[(n + 1) % 2])
        solve(grp, stages[n % 2], ready_steps())
        solved.update(range(grp[0], grp[0] + grp[1]))
    for step in ready_steps():
        recur(step)

    def finish(t, carry):
        r0 = pl.multiple_of(t * ROW_TILE, ROW_TILE)
        for h in range(2):
            sl = slice(h * GDN_HD, (h + 1) * GDN_HD)
            x = acc_s[pl.ds(r0, ROW_TILE), sl]
            o_ref[0, pl.ds(r0, ROW_TILE), sl] = (
                x * lax.rsqrt(jnp.mean(x * x, axis=-1, keepdims=True) + EPS) * gn_ref[...]).astype(o_ref.dtype)
        return carry

    lax.fori_loop(0, N_ROW_TILES, finish, 0, unroll=3)


def _gdn_mixer(p_main, p_ab, conv_w, a_log, dt_bias, g_norm):
    mats, lvl = _gdn_consts()
    bsz = p_main.shape[0]
    ab = p_ab.reshape(bsz, N_CHUNKS, CHUNK, 2, 2, GDN_QK_HEADS, 2)
    abr = ab.transpose(0, 5, 1, 4, 3, 6, 2).reshape(bsz, GDN_QK_HEADS, N_CHUNKS, 4, PAIR)
    prm = jnp.stack([a_log, dt_bias], axis=0).astype(F32).reshape(2, 2, GDN_QK_HEADS, 2)
    zero = jnp.zeros_like(prm[:, 0])
    rows = jnp.stack([prm[:, 0], prm[:, 0], prm[:, 1], prm[:, 1]] + [zero] * (SUB - 4), axis=1)
    pr = jnp.repeat(rows.transpose(2, 0, 1, 3), CHUNK, axis=3)

    grid = (bsz, GDN_QK_HEADS)
    kq = GDN_KEY_DIM // GDN_HD
    in_specs = [
        pl.BlockSpec((1, L_ALL, GDN_HD), lambda b, j: (b, 0, j)),
        pl.BlockSpec((1, L_ALL, GDN_HD), lambda b, j: (b, 0, kq + j)),
        pl.BlockSpec((1, L_ALL, 2 * GDN_HD), lambda b, j: (b, 0, kq + j)),
        pl.BlockSpec((4, GDN_HD), lambda b, j: (0, j)),
        pl.BlockSpec((4, GDN_HD), lambda b, j: (0, kq + j)),
        pl.BlockSpec((4, 2 * GDN_HD), lambda b, j: (0, kq + j)),
        pl.BlockSpec((1, 1, N_CHUNKS, 4, PAIR), lambda b, j: (b, j, 0, 0, 0)),
        pl.BlockSpec((1, 2, SUB, PAIR), lambda b, j: (j, 0, 0, 0)),
        pl.BlockSpec(mats.shape, lambda b, j: (0, 0, 0)),
        pl.BlockSpec(lvl.shape, lambda b, j: (0, 0, 0)),
        pl.BlockSpec((1, GDN_HD), lambda b, j: (0, 0)),
    ]
    return pl.pallas_call(
        _gdn_body,
        grid=grid,
        in_specs=in_specs,
        out_specs=pl.BlockSpec((1, L_ALL, 2 * GDN_HD), lambda b, j: (b, 0, j)),
        out_shape=jax.ShapeDtypeStruct((bsz, L_ALL, GDN_VAL_DIM), ACT),
        scratch_shapes=[
            pltpu.VMEM((4, L_ALL + 3 * SUB, GDN_HD), F32),
            pltpu.VMEM((2 * N_CHUNKS, 3 * GDN_HD, GDN_HD), BF16),
            pltpu.VMEM((2 * N_CHUNKS, GDN_HD, 2 * GDN_HD), F32),
            pltpu.VMEM((2 * N_CHUNKS, PAIR, GDN_HD), BF16),
            pltpu.VMEM((2 * N_CHUNKS, SUB, 2 * GDN_HD), F32),
            pltpu.VMEM((2, GDN_HD, 2 * GDN_HD), F32),
            pltpu.VMEM((L_ALL, 2 * GDN_HD), F32),
        ] + 2 * [
            pltpu.VMEM((A_GROUP, PAIR, 2 * PAIR), BF16),
            pltpu.VMEM((2 * A_GROUP, PAIR, 2 * GDN_HD), BF16),
            pltpu.VMEM((2 * A_GROUP, PAIR, GDN_HD), F32),
            pltpu.VMEM((2 * A_GROUP, PAIR, GDN_HD), BF16),
            pltpu.VMEM((2 * A_GROUP, PAIR, PAIR), BF16),
        ],
        compiler_params=_cparams(("parallel", "parallel")),
    )(p_main, p_main, p_main, conv_w, conv_w, conv_w, abr, pr, mats, lvl,
      g_norm.reshape(1, GDN_HD))


def _lru_body(u_ref, cw_ref, cb_ref, wra_ref, wri_ref, bra_ref, bri_ref, lam_ref, o_ref, a_s, b_s, acc_s):
    full = slice(None)
    n_steps = L_ALL // LRU_ROWS

    def gates(c, carry):
        r0 = pl.multiple_of(c * LRU_ROWS, LRU_ROWS)
        left_ok = jnp.where(c >= 2, 1.0, 0.0)
        right_ok = jnp.where((c == 0) | (c == n_steps - 1), 0.0, 1.0)
        u = _conv_rows(u_ref, r0, LRU_ROWS, full, cw_ref[...], left_ok, right_ok) + cb_ref[...]
        ub = u.astype(BF16)
        blocks = [slice(n * LRU_BLOCK, (n + 1) * LRU_BLOCK) for n in range(LRU_LANES // LRU_BLOCK)]
        for d in range(2):
            r = _sigmoid(jnp.concatenate([_dot(ub[:, sl], wra_ref[d, n]) for n, sl in enumerate(blocks)],
                                               axis=1) + bra_ref[d:d + 1])
            i = _sigmoid(jnp.concatenate([_dot(ub[:, sl], wri_ref[d, n]) for n, sl in enumerate(blocks)],
                                               axis=1) + bri_ref[d:d + 1])
            log_a = -LRU_C * r * _softplus(-lam_ref[d:d + 1])
            a = jnp.exp(log_a)
            one_m_a2 = -jnp.tanh(log_a) * (a * a + 1.0)
            a_s[d, pl.ds(r0, LRU_ROWS), :] = a
            b_s[d, pl.ds(r0, LRU_ROWS), :] = jnp.sqrt(one_m_a2) * (i * u)
        return carry

    lax.fori_loop(0, n_steps, gates, 0)

    acc_s[...] = jnp.zeros_like(acc_s)
    row = lax.broadcasted_iota(jnp.int32, (SUB, LRU_LANES), 0)

    def scan(g, carry):
        hf, hb = carry
        gb = jnp.where(g < CTX_GROUPS, CTX_GROUPS - 1 - g, N_GROUPS + CTX_GROUPS - 1 - g)
        r0 = pl.multiple_of(g * SUB, SUB)
        a, b = a_s[0, pl.ds(r0, SUB), :], b_s[0, pl.ds(r0, SUB), :]
        for s in (1, 2, 4):
            ok = row >= s
            b = jnp.where(ok, a * pltpu.roll(b, s, 0) + b, b)
            a = jnp.where(ok, a * pltpu.roll(a, s, 0), a)
        h = b + a * hf
        acc_s[pl.ds(r0, SUB), :] += h
        hf = h[SUB - 1:SUB]
        r0 = pl.multiple_of(gb * SUB, SUB)
        a, b = a_s[1, pl.ds(r0, SUB), :], b_s[1, pl.ds(r0, SUB), :]
        for s in (1, 2, 4):
            ok = row < SUB - s
            b = jnp.where(ok, a * pltpu.roll(b, SUB - s, 0) + b, b)
            a = jnp.where(ok, a * pltpu.roll(a, SUB - s, 0), a)
        h = b + a * hb
        acc_s[pl.ds(r0, SUB), :] += h
        hb = h[0:1]
        return hf, hb

    zero = jnp.zeros((1, LRU_LANES), F32)
    lax.fori_loop(0, N_GROUPS, scan, (zero, zero), unroll=4)

    def emit(t, carry):
        r0 = pl.multiple_of(t * LRU_ROWS, LRU_ROWS)
        o_ref[0, pl.ds(r0, LRU_ROWS), :] = acc_s[pl.ds(r0, LRU_ROWS), :].astype(o_ref.dtype)
        return carry

    lax.fori_loop(0, n_steps, emit, 0)


def _lru_mixer(p, conv_w, conv_b, w_ra, w_ri, b_ra, b_ri, lam):
    nb = LRU_WIDTH // LRU_LANES
    per = LRU_LANES // LRU_BLOCK
    bsz = p.shape[0]
    vec = lambda rows: pl.BlockSpec((rows, LRU_LANES), lambda b, n: (0, n))
    wspec = pl.BlockSpec((2, per, LRU_BLOCK, LRU_BLOCK), lambda b, n: (0, n, 0, 0))
    return pl.pallas_call(
        _lru_body,
        grid=(bsz, nb),
        in_specs=[
            pl.BlockSpec((1, L_ALL, LRU_LANES), lambda b, n: (b, 0, nb + n)),
            vec(4), vec(1), wspec, wspec, vec(2), vec(2), vec(2),
        ],
        out_specs=pl.BlockSpec((1, L_ALL, LRU_LANES), lambda b, n: (b, 0, n)),
        out_shape=jax.ShapeDtypeStruct((bsz, L_ALL, LRU_WIDTH), ACT),
        scratch_shapes=[pltpu.VMEM((2, L_ALL, LRU_LANES), F32), pltpu.VMEM((2, L_ALL, LRU_LANES), F32),
                        pltpu.VMEM((L_ALL, LRU_LANES), F32)],
        compiler_params=_cparams(("parallel", "parallel")),
    )(p, conv_w, conv_b.reshape(1, LRU_WIDTH), w_ra.astype(BF16), w_ri.astype(BF16), b_ra, b_ri, lam)


def _att_body(sink_ref, q_ref, kc_ref, vc_ref, k0_ref, k1_ref, k2_ref, v0_ref, v1_ref, v2_ref, o_ref):
    i = pl.program_id(1)
    n_ctx_blocks = CTX_LEN // ATT_BLOCK
    li = i - n_ctx_blocks
    qpos = li * ATT_BLOCK + lax.broadcasted_iota(jnp.int32, (ATT_BLOCK, 3 * ATT_BLOCK), 0)
    kpos = (li - 1) * ATT_BLOCK + lax.broadcasted_iota(jnp.int32, (ATT_BLOCK, 3 * ATT_BLOCK), 1)
    band = (jnp.abs(qpos - kpos) <= ATT_WINDOW) & (kpos >= 0) & (kpos < SEQ) & (li >= 0)
    bias = jnp.concatenate([jnp.zeros((ATT_BLOCK, CTX_LEN), F32), jnp.where(band, 0.0, NEG)], axis=1)
    bias4 = jnp.concatenate([bias] * ATT_GROUP, axis=0)
    lane = lax.broadcasted_iota(jnp.int32, (ATT_BLOCK, ATT_QW), 1)
    qmask = [(lane % 128) // 32 == g for g in range(ATT_GROUP)]
    omask = [lane // ATT_HD == g for g in range(ATT_GROUP)]
    rowg = lax.broadcasted_iota(jnp.int32, (ATT_GROUP * ATT_BLOCK, 1), 0) // ATT_BLOCK
    def scores(h):
        sl = slice(h * ATT_QW, (h + 1) * ATT_QW)
        qh = q_ref[0, :, sl]
        qm = jnp.concatenate([jnp.where(qmask[g], qh, 0.0) for g in range(ATT_GROUP)], axis=0).astype(BF16)
        keys = jnp.concatenate([kc_ref[0, :, sl], k0_ref[0, :, sl], k1_ref[0, :, sl], k2_ref[0, :, sl]],
                               axis=0).astype(BF16)
        return _dot_nt(qm, keys)

    s_next = scores(0)
    for h in range(ATT_KV_HEADS):
        sl = slice(h * ATT_QW, (h + 1) * ATT_QW)
        s = s_next + bias4
        if h + 1 < ATT_KV_HEADS:
            s_next = scores(h + 1)
        vals = jnp.concatenate([vc_ref[0, :, sl], v0_ref[0, :, sl], v1_ref[0, :, sl], v2_ref[0, :, sl]],
                               axis=0).astype(BF16)
        sink = jnp.zeros((ATT_GROUP * ATT_BLOCK, 1), F32)
        for g in range(ATT_GROUP):
            sink = jnp.where(rowg == g, sink_ref[h * ATT_GROUP + g], sink)
        mx = jnp.maximum(jnp.max(s, axis=-1, keepdims=True), sink)
        e = jnp.exp(s - mx)
        den = jnp.sum(e, axis=-1, keepdims=True) + jnp.exp(sink - mx)
        r = _dot(e.astype(BF16), vals) / den
        out = jnp.zeros((ATT_BLOCK, ATT_QW), F32)
        for g in range(ATT_GROUP):
            out = jnp.where(omask[g], r[g * ATT_BLOCK:(g + 1) * ATT_BLOCK], out)
        o_ref[0, :, sl] = out.astype(o_ref.dtype)


def _att_mixer(p, sinks):
    n_ctx_blocks = CTX_LEN // ATT_BLOCK
    n_lat_blocks = SEQ // ATT_BLOCK
    bsz = p.shape[0]

    def win(off, col):
        def index(b, i, sink_ref):
            blk = jnp.clip(i - n_ctx_blocks + off, 0, n_lat_blocks - 1)
            return (b, n_ctx_blocks + blk, col)
        return pl.BlockSpec((1, ATT_BLOCK, D_MODEL), index)

    grid_spec = pltpu.PrefetchScalarGridSpec(
        num_scalar_prefetch=1,
        grid=(bsz, L_ALL // ATT_BLOCK),
        in_specs=[
            pl.BlockSpec((1, ATT_BLOCK, D_MODEL), lambda b, i, s: (b, i, 0)),
            pl.BlockSpec((1, CTX_LEN, D_MODEL), lambda b, i, s: (b, 0, 2)),
            pl.BlockSpec((1, CTX_LEN, D_MODEL), lambda b, i, s: (b, 0, 3)),
            win(-1, 2), win(0, 2), win(1, 2), win(-1, 3), win(0, 3), win(1, 3),
        ],
        out_specs=pl.BlockSpec((1, ATT_BLOCK, D_MODEL), lambda b, i, s: (b, i, 0)),
    )
    return pl.pallas_call(
        _att_body,
        grid_spec=grid_spec,
        out_shape=jax.ShapeDtypeStruct((bsz, L_ALL, D_MODEL), ACT),
        compiler_params=_cparams(("parallel", "arbitrary")),
    )(sinks.astype(F32), p, p, p, p, p, p, p, p, p)


def _att_weight_layout(w_in):
    half = ATT_HD // 2
    q_cols, k_cols, v_cols = [], [], []
    for h in range(ATT_KV_HEADS):
        for part in range(2):
            for g in range(ATT_GROUP):
                base = (h * ATT_GROUP + g) * ATT_HD + part * half
                q_cols += list(range(base, base + half))
                kb = D_MODEL + h * ATT_HD + part * half
                k_cols += list(range(kb, kb + half))
        vb = D_MODEL + ATT_KV_HEADS * ATT_HD + h * ATT_HD
        v_cols += list(range(vb, vb + ATT_HD)) * ATT_GROUP
    z0 = D_MODEL + 2 * ATT_KV_HEADS * ATT_HD
    z_cols = list(range(z0, z0 + D_MODEL))
    order = np.asarray(q_cols + z_cols + k_cols + v_cols, np.int32)
    scale = np.ones((ATT_COLS,), np.float32)
    scale[:D_MODEL] = ATT_HD ** -0.5
    return (w_in[:, order] * scale).astype(BF16)


def _rope_tables():
    rows = SEQ // GRID_W
    row = jnp.repeat(jnp.arange(rows), GRID_W)
    col = jnp.tile(jnp.arange(GRID_W), rows)
    n_freq = ATT_HD // 4
    inv = 10000.0 ** (-jnp.arange(n_freq, dtype=F32) / n_freq)
    ang = jnp.concatenate([row[:, None] * inv, col[:, None] * inv], axis=-1)
    cos = jnp.concatenate([jnp.ones((CTX_LEN, ATT_HD // 2), F32), jnp.cos(ang)], axis=0)
    sin = jnp.concatenate([jnp.zeros((CTX_LEN, ATT_HD // 2), F32), jnp.sin(ang)], axis=0)
    return jnp.tile(cos, (1, ATT_GROUP)), jnp.tile(sin, (1, ATT_GROUP))


def kernel(x, c, ctx, c_ctx, ada_w, ada_b, norm_pre, norm_post, gdn_w_in, gdn_conv_w, gdn_a_log, gdn_dt_bias, gdn_g_norm, gdn_w_out, lru_w_in, lru_conv_w, lru_conv_b, lru_w_ra, lru_b_ra, lru_w_ri, lru_b_ri, lru_lam, lru_w_out, att_w_in, att_sinks, att_w_out):
    cc = jnp.zeros((MOD_ROWS, D_MODEL), F32).at[:c.shape[0]].set(c).at[CTX_MOD_ROW].set(c_ctx)
    mod_all = _modulation(cc, ada_w, ada_b).reshape(DEPTH, MOD_ROWS, 1, 3 * D_MODEL)
    xs = (ctx, x)
    cos, sin = _rope_tables()

    def in_args(i):
        kind, j = i % 3, i // 3
        if kind == 0:
            return dict(w=gdn_w_in[j].astype(BF16), segs=[(0, GDN_MAIN), (GDN_MAIN, GDN_AB)],
                        out_dtypes=[ACT, F32])
        if kind == 1:
            return dict(w=lru_w_in[j].astype(BF16), segs=[(0, 2 * LRU_WIDTH)], out_dtypes=[ACT])
        return dict(w=_att_weight_layout(att_w_in[j]), segs=[(0, ATT_COLS)], out_dtypes=[ACT],
                    rope=(cos, sin, (0, 1, 4, 5)))

    a0 = in_args(0)
    proj = _in_proj(xs, mod_all[0], norm_pre[0], a0.pop("w"), **a0)
    for i in range(DEPTH):
        kind, j = i % 3, i // 3
        if kind == 0:
            o = _gdn_mixer(proj[0], proj[1], gdn_conv_w[j], gdn_a_log[j], gdn_dt_bias[j], gdn_g_norm[j])
            z_block, w_out = 2, gdn_w_out[j]
        elif kind == 1:
            o = _lru_mixer(proj[0], lru_conv_w[j], lru_conv_b[j], lru_w_ra[j], lru_w_ri[j],
                           lru_b_ra[j], lru_b_ri[j], lru_lam[j])
            z_block, w_out = 0, lru_w_out[j]
        else:
            o = _att_mixer(proj[0], att_sinks[j])
            z_block, w_out = 1, att_w_out[j]
        if i == DEPTH - 1:
            return _out_proj(o, proj[0], z_block, xs, mod_all[i], norm_post[i], w_out.astype(BF16),
                             latent_only=True)
        nxt = in_args(i + 1)
        xs, *proj = _out_in_proj(o, proj[0], z_block, xs, mod_all[i], norm_post[i], w_out.astype(BF16),
                                 mod_all[i + 1], norm_pre[i + 1], nxt.pop("w"), **nxt)
```
